```python
import math
import jax
import jax.numpy as jnp
from jax import lax
import numpy as np

D_MODEL = 1024
BATCH = 4
SEQ = 8192
DEPTH = 2

GRID_W = 64
CTX_LEN = 256
EPS = 1e-6
ROPE_BASE = 10000.0

A_HEADS = 8
A_DK = 64
A_DV = 64
A_CHUNK = 64
SHORT_CONV = 3
B_HEADS = 4
B_DH = 64
Q_BLOCK = 128
C_GROUPS = 4
PEER_HEADS = 8
PEER_NKEYS = 128
PEER_EXPERTS = PEER_NKEYS * PEER_NKEYS
PEER_DKEY = 256
PEER_TOPK = 16
PEER_BLOCK = 128

N_EVEN = (DEPTH + 1) // 2
N_ODD = DEPTH // 2

A_QK_W = A_HEADS * A_DK
A_V_W = A_HEADS * A_DV
B_QK_W = B_HEADS * 2 * B_DH
B_V_W = B_HEADS * 2 * B_DH
IN_SPLITS = (A_QK_W, A_QK_W, A_V_W, A_V_W, 2 * A_HEADS, 2 * A_HEADS, B_QK_W, B_QK_W, B_V_W)
IN_W = sum(IN_SPLITS)
MIX_W = A_V_W + B_V_W
CONV_CH = 2 * A_QK_W + A_V_W

kernel_name = 'hybrid_gdn_diffattn_fnet_peer_dit'


def rmsnorm(x, w):
    xf = x.astype(jnp.float32)
    y = xf * lax.rsqrt(jnp.mean(xf * xf, axis=-1, keepdims=True) + EPS)
    return (y * w.astype(jnp.float32)).astype(x.dtype)


def l2norm(x):
    return x * lax.rsqrt(jnp.sum(x * x, axis=-1, keepdims=True) + EPS)


def split_cols(p):
    idx, acc = [], 0
    for w in IN_SPLITS[:-1]:
        acc += w
        idx.append(acc)
    return jnp.split(p, idx, axis=-1)


def short_conv(x, w):
    pad = SHORT_CONV // 2
    return lax.conv_general_dilated(x, w[:, None, :].astype(x.dtype), (1,), [(pad, pad)],
                                    dimension_numbers=('NWC', 'WIO', 'NWC'),
                                    feature_group_count=x.shape[-1])


def axial_rope(n_tokens):
    rows = n_tokens // GRID_W
    row = jnp.repeat(jnp.arange(rows, dtype=jnp.float32), GRID_W)
    col = jnp.tile(jnp.arange(GRID_W, dtype=jnp.float32), rows)
    axis_dim = B_DH // 2
    inv = ROPE_BASE ** (-jnp.arange(0, axis_dim, 2, dtype=jnp.float32) / axis_dim)
    ang = jnp.concatenate([row[:, None] * inv, col[:, None] * inv], axis=-1)
    return jnp.cos(ang), jnp.sin(ang)


def apply_rope(x, cos, sin):
    x1, x2 = jnp.split(x, 2, axis=-1)
    c = cos[None, :, None, None, :]
    s = sin[None, :, None, None, :]
    return jnp.concatenate([x1 * c - x2 * s, x1 * s + x2 * c], axis=-1).astype(x.dtype)


def _to_chunks(t, n):
    t = t.reshape((t.shape[0], n, A_CHUNK) + t.shape[2:])
    return jnp.moveaxis(t, (1, 3), (0, 2))


def gated_delta_chunked(q, k, v, beta, g, s0):
    bsz, L, H, _ = k.shape
    n = L // A_CHUNK
    kc, vc, bc, gc = (_to_chunks(t, n) for t in (k, v, beta, g))
    gcum = jnp.cumsum(gc, axis=-1)
    lower = jnp.tril(jnp.ones((A_CHUNK, A_CHUNK), dtype=bool))
    strict = jnp.tril(jnp.ones((A_CHUNK, A_CHUNK), dtype=bool), -1)
    decay = jnp.exp(jnp.where(lower, gcum[..., :, None] - gcum[..., None, :], -jnp.inf))
    kb = kc * bc[..., None]
    m = jnp.where(strict, jnp.einsum('nbhid,nbhjd->nbhij', kb, kc) * decay, 0.0)
    eye = jnp.eye(A_CHUNK, dtype=jnp.float32)
    t_inv = lax.linalg.triangular_solve(eye + m, jnp.broadcast_to(eye, m.shape), left_side=True,
                                        lower=True, unit_diagonal=True)
    u = jnp.einsum('nbhij,nbhjd->nbhid', t_inv, vc * bc[..., None])
    w = jnp.einsum('nbhij,nbhjd->nbhid', t_inv, kb * jnp.exp(gcum)[..., None])
    k_tail = kc * jnp.exp(gcum[..., -1:] - gcum)[..., None]
    g_tot = jnp.exp(gcum[..., -1])

    def advance(S, u_i, w_i, kt_i, gt_i):
        v_new = u_i - jnp.einsum('bhcd,bhde->bhce', w_i, S)
        S_next = S * gt_i[..., None, None] + jnp.einsum('bhcd,bhce->bhde', kt_i, v_new)
        return S_next, v_new

    if q is None:
        def step_state(S, xs):
            S_next, _ = advance(S, *xs)
            return S_next, None
        S_fin, _ = lax.scan(step_state, s0, (u, w, k_tail, g_tot))
        return S_fin, None

    qc = _to_chunks(q, n)
    q_dec = qc * jnp.exp(gcum)[..., None]
    attn = jnp.einsum('nbhid,nbhjd->nbhij', qc, kc) * decay

    def step(S, xs):
        u_i, w_i, kt_i, gt_i, qd_i, at_i = xs
        S_next, v_new = advance(S, u_i, w_i, kt_i, gt_i)
        o = jnp.einsum('bhcd,bhde->bhce', qd_i, S) + jnp.einsum('bhij,bhje->bhie', at_i, v_new)
        return S_next, o

    S_fin, o = lax.scan(step, s0, (u, w, k_tail, g_tot, q_dec, attn))
    o = jnp.moveaxis(o, (0, 2), (1, 3)).reshape(bsz, L, H, v.shape[-1])
    return S_fin, o


def flip_if(t, rev):
    return jnp.flip(t, axis=1) if rev else t


def diff_softmax_mix(q, keys, vals, lam):
    s = jnp.einsum('bqhmd,bkhmd->bhmqk', q.astype(jnp.float32), keys.astype(jnp.float32)) * (B_DH ** -0.5)
    p = jax.nn.softmax(s, axis=-1)
    a = p[:, :, 0] - lam * p[:, :, 1]
    return jnp.einsum('bhqk,bkhe->bqhe', a, vals.astype(jnp.float32))


def mixer_ab(h, h_c, layer, w_in, conv_w, a_log, dt_bias, gdn_norm_w, lam_q1, lam_k1, lam_q2, lam_k2,
             subln_w, w_out, cos, sin, with_ctx_out):
    f32 = jnp.float32
    lam_init = 0.8 - 0.6 * math.exp(-0.3 * layer)
    lam = (jnp.exp(jnp.sum(lam_q1.astype(f32) * lam_k1.astype(f32)))
           - jnp.exp(jnp.sum(lam_q2.astype(f32) * lam_k2.astype(f32))) + lam_init)

    def project(t):
        bsz, n, _ = t.shape
        aq, ak, av, ag, a_beta, a_alpha, bq, bk, bv = split_cols(t @ w_in)
        qkv = jax.nn.silu(short_conv(jnp.concatenate([aq, ak, av], axis=-1), conv_w)).astype(f32)
        q, k, v = jnp.split(qkv, [A_QK_W, 2 * A_QK_W], axis=-1)
        q = l2norm(q.reshape(bsz, n, A_HEADS, A_DK)) * (A_DK ** -0.5)
        k = l2norm(k.reshape(bsz, n, A_HEADS, A_DK))
        v = v.reshape(bsz, n, A_HEADS, A_DV)
        beta = jax.nn.sigmoid(a_beta.astype(f32).reshape(bsz, n, 2, A_HEADS))
        g = -jnp.exp(a_log.astype(f32)) * jax.nn.softplus(
            a_alpha.astype(f32).reshape(bsz, n, 2, A_HEADS) + dt_bias.astype(f32))
        return (q, k, v, beta, g, ag,
                bq.reshape(bsz, n, B_HEADS, 2, B_DH), bk.reshape(bsz, n, B_HEADS, 2, B_DH),
                bv.reshape(bsz, n, B_HEADS, 2 * B_DH))

    ql, kl, vl, betal, gl, gate_l, bql, bkl, bvl = project(h)
    qc, kc, vc, betac, gc, gate_c, bqc, bkc, bvc = project(h_c)
    bsz, L = h.shape[0], h.shape[1]

    s0 = jnp.zeros((bsz, A_HEADS, A_DK, A_DV), f32)
    o_lat_dirs, o_ctx_dirs = [], []
    for d in range(2):
        rev = d == 1
        s_ctx, oc = gated_delta_chunked(flip_if(qc, rev) if with_ctx_out else None, flip_if(kc, rev),
                                        flip_if(vc, rev), flip_if(betac[:, :, d], rev),
                                        flip_if(gc[:, :, d], rev), s0)
        _, ol = gated_delta_chunked(flip_if(ql, rev), flip_if(kl, rev), flip_if(vl, rev),
                                    flip_if(betal[:, :, d], rev), flip_if(gl[:, :, d], rev), s_ctx)
        o_lat_dirs.append(flip_if(ol, rev))
        if with_ctx_out:
            o_ctx_dirs.append(flip_if(oc, rev))

    def gdn_out(o, gate):
        y = rmsnorm(o, gdn_norm_w) * jax.nn.silu(gate.astype(f32)).reshape(o.shape)
        return y.reshape(o.shape[0], o.shape[1], A_V_W)

    def diff_out(o):
        return (rmsnorm(o, subln_w) * (1.0 - lam_init)).reshape(o.shape[0], o.shape[1], B_V_W)

    bql_r = apply_rope(bql, cos, sin)
    bkl_r = apply_rope(bkl, cos, sin)
    keys = jnp.concatenate([bkl_r, bkc], axis=1)
    vals = jnp.concatenate([bvl, bvc], axis=1)
    nb = L // Q_BLOCK
    qb = jnp.moveaxis(bql_r.reshape(bsz, nb, Q_BLOCK, B_HEADS, 2, B_DH), 1, 0)
    ob = lax.map(lambda blk: diff_softmax_mix(blk, keys, vals, lam), qb)
    d_lat = jnp.moveaxis(ob, 0, 1).reshape(bsz, L, B_HEADS, 2 * B_DH)

    y_lat = jnp.concatenate([gdn_out(o_lat_dirs[0] + o_lat_dirs[1], gate_l), diff_out(d_lat)],
                            axis=-1).astype(h.dtype) @ w_out
    y_ctx = None
    if with_ctx_out:
        d_ctx = diff_softmax_mix(bqc, bkc, bvc, lam)
        y_ctx = jnp.concatenate([gdn_out(o_ctx_dirs[0] + o_ctx_dirs[1], gate_c), diff_out(d_ctx)],
                                axis=-1).astype(h_c.dtype) @ w_out
    return y_lat, y_ctx


def fourier_mix(h):
    bsz, L, D = h.shape
    hg = h.astype(jnp.float32).reshape(bsz, L, C_GROUPS, D // C_GROUPS)
    return jnp.fft.fft2(hg, axes=(1, 3), norm='ortho').real.reshape(bsz, L, D).astype(h.dtype)


def peer(h, w_q, sub_keys, u_tab, v_tab):
    bsz, L, D = h.shape
    blocks = h.reshape(-1, PEER_BLOCK, D)

    def block(hb):
        q = (hb @ w_q).astype(jnp.float32).reshape(PEER_BLOCK, PEER_HEADS, 2, PEER_DKEY // 2)
        s = jnp.einsum('thpd,hpnd->thpn', q, sub_keys.astype(jnp.float32))
        s1, i1 = lax.top_k(s[:, :, 0], PEER_TOPK)
        s2, i2 = lax.top_k(s[:, :, 1], PEER_TOPK)
        cand_s = (s1[..., :, None] + s2[..., None, :]).reshape(PEER_BLOCK, PEER_HEADS, PEER_TOPK * PEER_TOPK)
        cand_i = (i1[..., :, None] * PEER_NKEYS + i2[..., None, :]).reshape(PEER_BLOCK, PEER_HEADS, PEER_TOPK * PEER_TOPK)
        top_s, pos = lax.top_k(cand_s, PEER_TOPK)
        idx = jnp.take_along_axis(cand_i, pos, axis=-1)
        gate = jax.nn.softmax(top_s, axis=-1)
        act = jax.nn.gelu(jnp.einsum('td,thkd->thk', hb.astype(jnp.float32), u_tab[idx].astype(jnp.float32)))
        return jnp.einsum('thk,thkd->td', (gate * act).astype(v_tab.dtype), v_tab[idx])

    return lax.map(block, blocks).reshape(bsz, L, D).astype(h.dtype)


def setup_inputs(seed: int = 0) -> dict:
    key = jax.random.key(seed)
    ks = jax.random.split(key, 25)

    def nrm(k, shape, s):
        return jax.random.normal(k, shape, jnp.float32) * s

    dt = jnp.exp(jax.random.uniform(ks[11], (N_EVEN, 2, A_HEADS), jnp.float32,
                                    minval=math.log(1e-3), maxval=math.log(1e-1)))
    return {
        'x': nrm(ks[0], (BATCH, SEQ, D_MODEL), 1.0),
        'c': nrm(ks[1], (BATCH, D_MODEL), 1.0),
        'ctx': nrm(ks[2], (BATCH, CTX_LEN, D_MODEL), 1.0),
        'c_ctx': nrm(ks[3], (D_MODEL,), 1.0),
        'ada_w': nrm(ks[4], (DEPTH, D_MODEL, 6 * D_MODEL), 0.3 * D_MODEL ** -0.5),
        'ada_b': nrm(ks[5], (DEPTH, 6 * D_MODEL), 0.02),
        'norm1_w': 1.0 + nrm(ks[6], (DEPTH, D_MODEL), 0.02),
        'norm2_w': 1.0 + nrm(ks[7], (DEPTH, D_MODEL), 0.02),
        'w_in': nrm(ks[8], (N_EVEN, D_MODEL, IN_W), D_MODEL ** -0.5),
        'conv_w': nrm(ks[9], (N_EVEN, SHORT_CONV, CONV_CH), SHORT_CONV ** -0.5),
        'a_log': jnp.log(jax.random.uniform(ks[10], (N_EVEN, 2, A_HEADS), jnp.float32, minval=1.0, maxval=16.0)),
        'dt_bias': dt + jnp.log(-jnp.expm1(-dt)),
        'gdn_norm_w': 1.0 + nrm(ks[12], (N_EVEN, A_DV), 0.02),
        'lam_q1': nrm(ks[13], (N_EVEN, B_DH), 0.1),
        'lam_k1': nrm(ks[14], (N_EVEN, B_DH), 0.1),
        'lam_q2': nrm(ks[15], (N_EVEN, B_DH), 0.1),
        'lam_k2': nrm(ks[16], (N_EVEN, B_DH), 0.1),
        'subln_w': 1.0 + nrm(ks[17], (N_EVEN, 2 * B_DH), 0.02),
        'w_out_ab': nrm(ks[18], (N_EVEN, MIX_W, D_MODEL), MIX_W ** -0.5),
        'w_out_f': nrm(ks[19], (N_ODD, D_MODEL, D_MODEL), D_MODEL ** -0.5),
        'peer_wq': nrm(ks[20], (DEPTH, D_MODEL, PEER_HEADS * PEER_DKEY), D_MODEL ** -0.5),
        'peer_keys': nrm(ks[21], (DEPTH, PEER_HEADS, 2, PEER_NKEYS, PEER_DKEY // 2), (PEER_DKEY // 2) ** -0.5),
        'peer_u': nrm(ks[22], (DEPTH, PEER_EXPERTS, D_MODEL), D_MODEL ** -0.5),
        'peer_v': nrm(ks[23], (DEPTH, PEER_EXPERTS, D_MODEL), 1.0),
        'final_norm_w': 1.0 + nrm(ks[24], (D_MODEL,), 0.02),
    }


def reference(x, c, ctx, c_ctx, ada_w, ada_b, norm1_w, norm2_w, w_in, conv_w, a_log, dt_bias,
              gdn_norm_w, lam_q1, lam_k1, lam_q2, lam_k2, subln_w, w_out_ab, w_out_f, peer_wq,
              peer_keys, peer_u, peer_v, final_norm_w):
    bsz, n_lat = x.shape[0], x.shape[1]
    cos, sin = axial_rope(n_lat)
    last_ctx_reader = 2 * ((DEPTH - 1) // 2)
    c_act = jax.nn.silu(c)
    cc_act = jax.nn.silu(c_ctx)
    for i in range(DEPTH):
        ctx_in = i <= last_ctx_reader
        ctx_adv = i < last_ctx_reader
        mod = (c_act @ ada_w[i] + ada_b[i]).reshape(bsz, 6, 1, D_MODEL)
        sh1, sc1, g1, sh2, sc2, g2 = (mod[:, j] for j in range(6))
        h = rmsnorm(x, norm1_w[i]) * (1.0 + sc1) + sh1
        if ctx_in:
            mc = (cc_act @ ada_w[i] + ada_b[i]).reshape(6, 1, 1, D_MODEL)
            h_c = rmsnorm(ctx, norm1_w[i]) * (1.0 + mc[1]) + mc[0]
        j = i // 2
        if i % 2 == 0:
            y, y_c = mixer_ab(h, h_c, i, w_in[j], conv_w[j], a_log[j], dt_bias[j], gdn_norm_w[j],
                              lam_q1[j], lam_k1[j], lam_q2[j], lam_k2[j], subln_w[j], w_out_ab[j],
                              cos, sin, ctx_adv)
        else:
            y = fourier_mix(h) @ w_out_f[j]
            y_c = fourier_mix(h_c) @ w_out_f[j] if ctx_adv else None
        x = x + g1 * y
        x = x + g2 * peer(rmsnorm(x, norm2_w[i]) * (1.0 + sc2) + sh2,
                          peer_wq[i], peer_keys[i], peer_u[i], peer_v[i])
        if ctx_adv:
            ctx = ctx + mc[2] * y_c
            ctx = ctx + mc[5] * peer(rmsnorm(ctx, norm2_w[i]) * (1.0 + mc[4]) + mc[3],
                                     peer_wq[i], peer_keys[i], peer_u[i], peer_v[i])
    return rmsnorm(x, final_norm_w)
```

```python
import functools
import math

import numpy as np
import jax
import jax.numpy as jnp
from jax import lax
from jax.experimental import pallas as pl
from jax.experimental.pallas import tpu as pltpu

F32 = jnp.float32
BF16 = jnp.bfloat16
I32 = jnp.int32
U32 = jnp.uint32
HIGHEST = lax.Precision.HIGHEST

EPS = 1e-6
ROPE_BASE = 10000.0
GRID_W = 64
A_HEADS, A_DK, A_DV, A_CHUNK = 8, 64, 64, 64
B_HEADS, B_DH = 4, 64
C_GROUPS = 4
PEER_HEADS, PEER_NKEYS, PEER_DKEY, PEER_TOPK = 8, 128, 256, 16
A_W = A_HEADS * A_DK
B_W = B_HEADS * 2 * B_DH
GDN_BLOCK = 256
DFT_N1 = 128

VMEM_LIMIT_BYTES = 48 * 1024 * 1024


def _cparams(*sem):
    return pltpu.CompilerParams(dimension_semantics=sem, vmem_limit_bytes=VMEM_LIMIT_BYTES)


def _dot(a, b):
    return lax.dot_general(a, b, (((1,), (0,)), ((), ())), preferred_element_type=F32)


def _dot_nt(a, b):
    return lax.dot_general(a, b, (((1,), (1,)), ((), ())), preferred_element_type=F32)


def _dot_tn(a, b):
    return lax.dot_general(a, b, (((0,), (0,)), ((), ())), preferred_element_type=F32)


def _dot_f32(a, b):
    return lax.dot_general(a, b, (((1,), (0,)), ((), ())), precision=HIGHEST,
                           preferred_element_type=F32)


def _split(a):
    hi = a.astype(BF16)
    lo = (a - hi.astype(F32)).astype(BF16)
    return hi, lo


def _mm(a, b, passes=1, dot=_dot):
    if passes == 1:
        return dot(a.astype(BF16), b.astype(BF16))
    ah, al = _split(a)
    bh, bl = _split(b)
    return dot(ah, bh) + (dot(ah, bl) + dot(al, bh))


def _mm_lhs2(a, b_bf16):
    ah, al = _split(a)
    return _dot(ah, b_bf16) + _dot(al, b_bf16)


def _silu(x):
    return x * jax.nn.sigmoid(x)


def _softplus(x):
    return jnp.maximum(x, 0.0) + jnp.log(1.0 + jnp.exp(-jnp.abs(x)))


def _normmod(x, nw, sc, sh):
    ms = jnp.mean(x * x, axis=-1, keepdims=True)
    return (x * lax.rsqrt(ms + EPS) * nw) * (1.0 + sc) + sh


def _block_ones(n, blk):
    i = np.arange(n) // blk
    return jnp.asarray((i[:, None] == i[None, :]).astype(np.float32), dtype=BF16)


def _mod_kernel(c_ref, w_ref, b_ref, o_ref):
    o_ref[...] = _dot_f32(_silu(c_ref[...]), w_ref[...]) + b_ref[...]


def _modulation(cmat, w, b):
    rows, d = cmat.shape
    n = w.shape[1]
    return pl.pallas_call(
        _mod_kernel,
        grid=(n // d,),
        in_specs=[pl.BlockSpec((rows, d), lambda j: (0, 0)),
                  pl.BlockSpec((d, d), lambda j: (0, j)),
                  pl.BlockSpec((1, d), lambda j: (0, j))],
        out_specs=pl.BlockSpec((rows, d), lambda j: (0, j)),
        out_shape=jax.ShapeDtypeStruct((rows, n), F32),
        compiler_params=_cparams("arbitrary"),
    )(cmat, w, b.reshape(1, n))


def _nm_matmul_kernel(*refs, widths, passes):
    x_ref, nw_ref, sc_ref, sh_ref = refs[:4]
    n_w = len(widths)
    w_refs = refs[4:4 + n_w * (2 if passes == 3 else 1)]
    outs = refs[4 + len(w_refs):]
    h = _normmod(x_ref[0], nw_ref[...], sc_ref[0, 0], sh_ref[0, 0])
    if passes == 1:
        hb = h.astype(BF16)
        for j in range(n_w):
            outs[j][0] = _dot(hb, w_refs[j][...]).astype(outs[j].dtype)
    else:
        hh, hl = _split(h)
        for j in range(n_w):
            wh, wl = w_refs[2 * j][...], w_refs[2 * j + 1][...]
            outs[j][0] = (_dot(hh, wh) + (_dot(hh, wl) + _dot(hl, wh))).astype(outs[j].dtype)
    if len(outs) > n_w:
        outs[n_w][0] = h


def _nm_matmul(x, nw, sc, sh, weights, out_dtypes, *, tm, n_first=0, passes=1, emit_h=False):
    bsz, n, d = x.shape
    seg = sc.shape[1]
    widths = tuple(w.shape[1] for w in weights)
    w_in = []
    for w in weights:
        if passes == 3:
            wh = w.astype(BF16)
            w_in += [wh, (w - wh.astype(F32)).astype(BF16)]
        else:
            w_in.append(w.astype(BF16))

    def mod_map(b, i):
        return (b, jnp.where(i < n_first, 0, seg - 1), 0, 0)

    in_specs = [pl.BlockSpec((1, tm, d), lambda b, i: (b, i, 0)),
                pl.BlockSpec((1, d), lambda b, i: (0, 0)),
                pl.BlockSpec((1, 1, 1, d), mod_map),
                pl.BlockSpec((1, 1, 1, d), mod_map)]
    in_specs += [pl.BlockSpec(w.shape, lambda b, i: (0, 0)) for w in w_in]
    out_specs = [pl.BlockSpec((1, tm, wd), lambda b, i: (b, i, 0)) for wd in widths]
    out_shape = [jax.ShapeDtypeStruct((bsz, n, wd), dt) for wd, dt in zip(widths, out_dtypes)]
    if emit_h:
        out_specs.append(pl.BlockSpec((1, tm, d), lambda b, i: (b, i, 0)))
        out_shape.append(jax.ShapeDtypeStruct((bsz, n, d), F32))
    return pl.pallas_call(
        functools.partial(_nm_matmul_kernel, widths=widths, passes=passes),
        grid=(bsz, n // tm),
        in_specs=in_specs, out_specs=out_specs, out_shape=out_shape,
        compiler_params=_cparams("parallel", "arbitrary"),
    )(x, nw.reshape(1, d), sc, sh, *w_in)


def _prep_kernel(qkv_ref, prev_ref, next_ref, cw_ref, bd_ref, bq_ref, bk_ref, cos_ref, sin_ref,
                 q_out, k_out, v_out, bqr_out, bkr_out):
    x = qkv_ref[0]
    tm = x.shape[0]
    row = lax.broadcasted_iota(I32, x.shape, 0)
    xm = jnp.where(row == 0, prev_ref[0, 0], pltpu.roll(x, 1, 0))
    xp = jnp.where(row == tm - 1, next_ref[0, 0], pltpu.roll(x, tm - 1, 0))
    cw = cw_ref[...]
    y = _silu(xm * cw[0:1] + x * cw[1:2] + xp * cw[2:3])
    bd = bd_ref[...]

    def l2(t):
        return t * lax.rsqrt(_mm_lhs2(t * t, bd) + EPS)

    q_out[0] = l2(y[:, :A_W]) * (A_DK ** -0.5)
    k_out[0] = l2(y[:, A_W:2 * A_W])
    v_out[0] = y[:, 2 * A_W:]

    cos = jnp.concatenate([cos_ref[...]] * (B_W // 128), axis=1)
    sin = jnp.concatenate([sin_ref[...]] * (B_W // 128), axis=1)
    lane = lax.broadcasted_iota(I32, (tm, B_W), 1)
    first = (lane & (B_DH - 1)) < (B_DH // 2)

    def rope(t):
        rot = jnp.where(first, pltpu.roll(t, B_W - B_DH // 2, 1), pltpu.roll(t, B_DH // 2, 1))
        return t * cos + rot * sin

    bqr_out[0] = (rope(bq_ref[0]) * (B_DH ** -0.5)).astype(BF16)
    bkr_out[0] = rope(bk_ref[0]).astype(BF16)


def _prep(qkv, prev_rows, next_rows, conv_w, bq, bk, cos128, sin128, *, tm):
    bsz, n, cw = qkv.shape
    tok = lambda b, i: (b, i, 0)
    halo = pl.BlockSpec((1, 1, 1, cw), lambda b, i: (b, i, 0, 0))
    return pl.pallas_call(
        _prep_kernel,
        grid=(bsz, n // tm),
        in_specs=[pl.BlockSpec((1, tm, cw), tok), halo, halo,
                  pl.BlockSpec(conv_w.shape, lambda b, i: (0, 0)),
                  pl.BlockSpec((A_W, A_W), lambda b, i: (0, 0)),
                  pl.BlockSpec((1, tm, B_W), tok), pl.BlockSpec((1, tm, B_W), tok),
                  pl.BlockSpec((tm, 128), lambda b, i: (i, 0)),
                  pl.BlockSpec((tm, 128), lambda b, i: (i, 0))],
        out_specs=[pl.BlockSpec((1, tm, A_W), tok)] * 3 + [pl.BlockSpec((1, tm, B_W), tok)] * 2,
        out_shape=[jax.ShapeDtypeStruct((bsz, n, A_W), F32)] * 3
        + [jax.ShapeDtypeStruct((bsz, n, B_W), BF16)] * 2,
        compiler_params=_cparams("parallel", "arbitrary"),
    )(qkv, prev_rows, next_rows, conv_w, _block_ones(A_W, A_DK), bq, bk, cos128, sin128)


def _tri_inverse(a, eye, d16, o32, o64, passes):
    n = -(a * d16)
    x = eye + n
    p = n
    for _ in range(3):
        p = _mm(p, p, passes)
        x = x + _mm(x, p, passes)
    for om in (o32, o64):
        x = x - _mm(_mm(x, a * om, passes), x, passes)
    return x


def _gdn_kernel(q_ref, k_ref, v_ref, ba_ref, bat_ref, alr_ref, dtr_ref, alc_ref, dtc_ref,
                lm_ref, lmt_ref, ltot_ref, e_ref, o_ref, s_ref, *, d, inv_passes):
    rev = d == 1
    nh = A_HEADS

    @pl.when(pl.program_id(1) == 0)
    def _():
        s_ref[...] = jnp.zeros_like(s_ref)

    q, k, v = q_ref[0], k_ref[0], v_ref[0]
    ba, bat = ba_ref[0], bat_ref[0]
    n = q.shape[0]
    beta_c = jax.nn.sigmoid(ba[:, d * nh:(d + 1) * nh])
    g_c = -jnp.exp(alr_ref[...]) * _softplus(ba[:, 2 * nh + d * nh:2 * nh + (d + 1) * nh] + dtr_ref[...])
    g_t = -jnp.exp(alc_ref[...]) * _softplus(bat[2 * nh + d * nh:2 * nh + (d + 1) * nh, :] + dtc_ref[...])
    lm, ltot, e = lm_ref[...], ltot_ref[...], e_ref[...]
    gcum_c = _dot_f32(lm, g_c)
    gcum_t = _dot_f32(g_t, lmt_ref[...])
    gcum_e = _dot_f32(gcum_c, e)
    gtot_e = _dot_f32(_dot_f32(ltot, g_c), e)
    beta_e = _dot_f32(beta_c, e)
    eg = jnp.exp(gcum_e)
    kb = k * beta_e
    vb = v * beta_e
    kbe = kb * eg
    qd = q * eg
    ktail = k * jnp.exp(gtot_e - gcum_e)
    egt = jnp.exp(gtot_e)

    ri = lax.broadcasted_iota(I32, (n, n), 0)
    ci = lax.broadcasted_iota(I32, (n, n), 1)
    eye = (ri == ci).astype(F32)
    same = lambda s: (ri >> s) == (ci >> s)
    d16 = same(4).astype(F32)
    o32 = (same(5) & ~same(4)).astype(F32)
    o64 = (same(6) & ~same(5)).astype(F32)
    incl = lm > 0.0
    strict = lm * (1.0 - eye)

    order = range(n // A_CHUNK - 1, -1, -1) if rev else range(n // A_CHUNK)
    heads = []
    for h in range(nh):
        sl = slice(h * A_DK, (h + 1) * A_DK)
        decay = jnp.exp(jnp.where(incl, gcum_c[:, h:h + 1] - gcum_t[h:h + 1, :], -1e30))
        a = _mm(kb[:, sl], k[:, sl], 1, _dot_nt) * decay * strict
        t_inv = _tri_inverse(a, eye, d16, o32, o64, inv_passes)
        uw = _mm(t_inv, jnp.concatenate([vb[:, sl], kbe[:, sl]], axis=1))
        attn = _mm(q[:, sl], k[:, sl], 1, _dot_nt) * decay
        s = s_ref[h]
        parts = {}
        for c in order:
            r = slice(c * A_CHUNK, (c + 1) * A_CHUNK)
            v_new = uw[r, :A_DV] - _mm(uw[r, A_DV:], s)
            parts[c] = _mm(qd[r, sl], s) + _mm(attn[r, r], v_new)
            s = s * egt[c * A_CHUNK:c * A_CHUNK + 1, sl] + _mm(ktail[r, sl], v_new, 1, _dot_tn)
        s_ref[h] = s
        heads.append(jnp.concatenate([parts[c] for c in range(n // A_CHUNK)], axis=0))
    o_ref[0] = jnp.concatenate(heads, axis=1)


def _gdn_constants(d):
    n = GDN_BLOCK
    i = np.arange(n)
    same = (i[:, None] // A_CHUNK) == (i[None, :] // A_CHUNK)
    tri = (i[None, :] >= i[:, None]) if d == 1 else (i[None, :] <= i[:, None])
    lm = (same & tri).astype(np.float32)
    e = (np.arange(A_W)[None, :] // A_DK == np.arange(A_HEADS)[:, None]).astype(np.float32)
    return jnp.asarray(lm), jnp.asarray(lm.T.copy()), jnp.asarray(same.astype(np.float32)), jnp.asarray(e)


def _gdn(q, k, v, ba, bat, a_log, dt_bias, *, d, n_ctx, inv_passes):
    bsz, n, _ = q.shape
    nb, nc = n // GDN_BLOCK, n_ctx // GDN_BLOCK
    blk = (lambda s: jnp.where(s < nc, nc - 1 - s, nb - 1 + nc - s)) if d == 1 else (lambda s: s)
    tok = lambda b, s: (b, blk(s), 0)
    full = lambda shape: pl.BlockSpec(shape, lambda b, s: (0,) * len(shape))
    lm, lmt, ltot, e = _gdn_constants(d)
    return pl.pallas_call(
        functools.partial(_gdn_kernel, d=d, inv_passes=inv_passes),
        grid=(bsz, nb),
        in_specs=[pl.BlockSpec((1, GDN_BLOCK, A_W), tok)] * 3
        + [pl.BlockSpec((1, GDN_BLOCK, ba.shape[2]), tok),
           pl.BlockSpec((1, bat.shape[1], GDN_BLOCK), lambda b, s: (b, 0, blk(s))),
           full((1, A_HEADS)), full((1, A_HEADS)), full((A_HEADS, 1)), full((A_HEADS, 1)),
           full(lm.shape), full(lm.shape), full(lm.shape), full(e.shape)],
        out_specs=pl.BlockSpec((1, GDN_BLOCK, A_W), tok),
        out_shape=jax.ShapeDtypeStruct((bsz, n, A_W), F32),
        scratch_shapes=[pltpu.VMEM((A_HEADS, A_DK, A_DV), F32)],
        compiler_params=_cparams("parallel", "arbitrary"),
    )(q, k, v, ba, bat, a_log[d].reshape(1, -1), dt_bias[d].reshape(1, -1),
      a_log[d].reshape(-1, 1), dt_bias[d].reshape(-1, 1), lm, lmt, ltot, e)


def _attn_kernel(q_ref, k_ref, v_ref, lq1_ref, lk1_ref, lq2_ref, lk2_ref, o_ref, *, lam_init, tk):
    q = q_ref[0]
    tq = q.shape[0]
    nk = k_ref.shape[1]
    lam = (jnp.exp(jnp.sum(lq1_ref[...] * lk1_ref[...], keepdims=True))
           - jnp.exp(jnp.sum(lq2_ref[...] * lk2_ref[...], keepdims=True)) + lam_init)
    outs = []
    for m in range(2):
        qm = q[:, m * B_DH:(m + 1) * B_DH]
        mx = jnp.full((tq, 1), -jnp.inf, F32)
        l = jnp.zeros((tq, 1), F32)
        acc = jnp.zeros((tq, 2 * B_DH), F32)
        for c in range(nk // tk):
            kc = k_ref[0, c * tk:(c + 1) * tk, m * B_DH:(m + 1) * B_DH]
            s = _dot_nt(qm, kc)
            m_new = jnp.maximum(mx, jnp.max(s, axis=-1, keepdims=True))
            alpha = jnp.exp(mx - m_new)
            p = jnp.exp(s - m_new)
            l = alpha * l + jnp.sum(p, axis=-1, keepdims=True)
            acc = alpha * acc + _dot(p.astype(BF16), v_ref[0, c * tk:(c + 1) * tk, :])
            mx = m_new
        outs.append(acc / l)
    o_ref[0] = outs[0] - lam * outs[1]


def _diff_attention(bqr, bkr, bv, lam_params, *, lam_init, n_ctx, tq, tk):
    bsz, n, _ = bqr.shape
    n_lat = n - n_ctx
    hw = 2 * B_DH
    lam_spec = pl.BlockSpec((1, B_DH), lambda b, h, i: (0, 0))
    return pl.pallas_call(
        functools.partial(_attn_kernel, lam_init=lam_init, tk=tk),
        grid=(bsz, B_HEADS, n_lat // tq),
        in_specs=[pl.BlockSpec((1, tq, hw), lambda b, h, i: (b, i + n_ctx // tq, h)),
                  pl.BlockSpec((1, n, hw), lambda b, h, i: (b, 0, h)),
                  pl.BlockSpec((1, n, hw), lambda b, h, i: (b, 0, h))] + [lam_spec] * 4,
        out_specs=pl.BlockSpec((1, tq, hw), lambda b, h, i: (b, i, h)),
        out_shape=jax.ShapeDtypeStruct((bsz, n_lat, B_W), F32),
        compiler_params=_cparams("parallel", "parallel", "arbitrary"),
    )(bqr, bkr, bv, *[p.reshape(1, B_DH) for p in lam_params])


def _outproj_kernel(of_ref, ob_ref, gate_ref, d_ref, x_ref, g1_ref, gw_ref, sw_ref, bd64_ref,
                    bd128_ref, w_ref, o_ref, *, lam_init):
    o = of_ref[0] + ob_ref[0]
    ms = _mm_lhs2(o * o, bd64_ref[...]) * (1.0 / A_DV)
    y1 = (o * lax.rsqrt(ms + EPS) * gw_ref[...]) * _silu(gate_ref[0])
    dd = d_ref[0]
    ms2 = _mm_lhs2(dd * dd, bd128_ref[...]) * (1.0 / (2 * B_DH))
    y2 = (dd * lax.rsqrt(ms2 + EPS) * sw_ref[...]) * (1.0 - lam_init)
    ycat = jnp.concatenate([y1, y2], axis=1).astype(BF16)
    o_ref[0] = x_ref[0] + g1_ref[0] * _dot(ycat, w_ref[...])


def _outproj(o_f, o_b, gate, d_lat, x, g1, gdn_norm_w, subln_w, w_out, *, lam_init, n_ctx, tm):
    bsz, n_lat, d = x.shape
    off = n_ctx // tm
    tok = lambda b, i: (b, i, 0)
    tok_off = lambda b, i: (b, i + off, 0)
    full = lambda shape: pl.BlockSpec(shape, lambda b, i: (0,) * len(shape))
    return pl.pallas_call(
        functools.partial(_outproj_kernel, lam_init=lam_init),
        grid=(bsz, n_lat // tm),
        in_specs=[pl.BlockSpec((1, tm, A_W), tok_off)] * 3
        + [pl.BlockSpec((1, tm, B_W), tok), pl.BlockSpec((1, tm, d), tok),
           pl.BlockSpec((1, 1, d), lambda b, i: (b, 0, 0)),
           full((1, A_W)), full((1, B_W)), full((A_W, A_W)), full((B_W, B_W)), full(w_out.shape)],
        out_specs=pl.BlockSpec((1, tm, d), tok),
        out_shape=jax.ShapeDtypeStruct((bsz, n_lat, d), F32),
        compiler_params=_cparams("parallel", "arbitrary"),
    )(o_f, o_b, gate, d_lat, x, g1, jnp.tile(gdn_norm_w, A_HEADS).reshape(1, A_W),
      jnp.tile(subln_w, B_HEADS).reshape(1, B_W), _block_ones(A_W, A_DV), _block_ones(B_W, 2 * B_DH),
      w_out.astype(BF16))


def _dft1_kernel(x_ref, nw_ref, sc_ref, sh_ref, f_ref, twr_ref, twi_ref, re_ref, im_ref, *, d, passes):
    n1 = f_ref.shape[1]
    f = f_ref[...]
    for t in range(x_ref.shape[2] // d):
        cs = slice(t * d, (t + 1) * d)
        h = _normmod(x_ref[0, :, cs], nw_ref[...], sc_ref[0], sh_ref[0])
        a = _mm(f, h, passes)
        a_re, a_im = a[:n1], a[n1:]
        twr = jnp.concatenate([twr_ref[:, t * 128:(t + 1) * 128]] * (d // 128), axis=1)
        twi = jnp.concatenate([twi_ref[:, t * 128:(t + 1) * 128]] * (d // 128), axis=1)
        re_ref[0, :, cs] = a_re * twr - a_im * twi
        im_ref[0, :, cs] = a_re * twi + a_im * twr


def _dft2_kernel(re_ref, im_ref, x_ref, g1_ref, f_ref, cc_ref, sc_ref, w_ref, o_ref, *, d, passes):
    n2 = f_ref.shape[1]
    f = f_ref[...]
    dg = d // C_GROUPS
    for t in range(re_ref.shape[1]):
        pr = _mm(f, re_ref[0, t], passes)
        pi = _mm(f, im_ref[0, t], passes)
        x_re = pr[:n2] + pi[n2:]
        x_im = pi[:n2] - pr[n2:]
        y = jnp.concatenate(
            [_mm(x_re[:, g * dg:(g + 1) * dg], cc_ref[...], passes)
             + _mm(x_im[:, g * dg:(g + 1) * dg], sc_ref[...], passes) for g in range(C_GROUPS)], axis=1)
        cs = slice(t * d, (t + 1) * d)
        o_ref[0, :, cs] = x_ref[0, :, cs] + g1_ref[0] * _dot(y.astype(BF16), w_ref[...])


def _fourier_layer(x, nw, sc, sh, g1, w_out, *, passes=3, nb=2, kb=8):
    bsz, n, d = x.shape
    n1, n2 = DFT_N1, n // DFT_N1
    dg = d // C_GROUPS
    k1 = np.arange(n1)
    ang1 = 2 * np.pi * np.outer(k1, k1) / n1
    f1 = jnp.asarray(np.concatenate([np.cos(ang1), -np.sin(ang1)], 0), F32)
    angt = 2 * np.pi * np.outer(k1, np.arange(n2)) / n
    twr = jnp.asarray(np.repeat(np.cos(angt), 128, axis=1), F32)
    twi = jnp.asarray(np.repeat(-np.sin(angt), 128, axis=1), F32)
    k2 = np.arange(n2)
    ang2 = 2 * np.pi * np.outer(k2, k2) / n2
    f2 = jnp.asarray(np.concatenate([np.cos(ang2), np.sin(ang2)], 0), F32)
    angc = 2 * np.pi * np.outer(np.arange(dg), np.arange(dg)) / dg
    scale = 1.0 / math.sqrt(n * dg)
    cc = jnp.asarray(np.cos(angc) * scale, F32)
    sn = jnp.asarray(np.sin(angc) * scale, F32)

    xv = x.reshape(bsz, n1, n2 * d)
    col = lambda b, j: (b, 0, j)
    vec = pl.BlockSpec((1, 1, d), lambda b, j: (b, 0, 0))
    full = lambda shape: pl.BlockSpec(shape, lambda b, j: (0,) * len(shape))
    a_re, a_im = pl.pallas_call(
        functools.partial(_dft1_kernel, d=d, passes=passes),
        grid=(bsz, n2 // nb),
        in_specs=[pl.BlockSpec((1, n1, nb * d), col), full((1, d)), vec, vec, full(f1.shape),
                  pl.BlockSpec((n1, nb * 128), lambda b, j: (0, j)),
                  pl.BlockSpec((n1, nb * 128), lambda b, j: (0, j))],
        out_specs=[pl.BlockSpec((1, n1, nb * d), col)] * 2,
        out_shape=[jax.ShapeDtypeStruct((bsz, n1, n2 * d), F32)] * 2,
        compiler_params=_cparams("parallel", "arbitrary"),
    )(xv, nw.reshape(1, d), sc, sh, f1, twr, twi)

    xo = x.reshape(bsz, n2, n1 * d)
    out = pl.pallas_call(
        functools.partial(_dft2_kernel, d=d, passes=passes),
        grid=(bsz, n1 // kb),
        in_specs=[pl.BlockSpec((1, kb, n2, d), lambda b, j: (b, j, 0, 0))] * 2
        + [pl.BlockSpec((1, n2, kb * d), col), vec, full(f2.shape), full(cc.shape), full(sn.shape),
           full(w_out.shape)],
        out_specs=pl.BlockSpec((1, n2, kb * d), col),
        out_shape=jax.ShapeDtypeStruct((bsz, n2, n1 * d), F32),
        compiler_params=_cparams("parallel", "arbitrary"),
    )(a_re.reshape(bsz, n1, n2, d), a_im.reshape(bsz, n1, n2, d), xo, g1, f2, cc, sn,
      w_out.astype(BF16))
    return out.reshape(bsz, n, d)


def _top16(s, payload=None):
    rows = lax.broadcasted_iota(I32, s.shape, 0)
    big = s.shape[0]
    vals, idxs = [], []
    for _ in range(PEER_TOPK):
        m = jnp.max(s, axis=0, keepdims=True)
        am = jnp.min(jnp.where(s == m, rows, big), axis=0, keepdims=True)
        sel = rows == am
        vals.append(m)
        if payload is None:
            idxs.append(am)
        else:
            idxs.append(jnp.max(jnp.where(sel, payload, -1), axis=0, keepdims=True))
        s = jnp.where(sel, -jnp.inf, s)
    return jnp.concatenate(vals, axis=0), jnp.concatenate(idxs, axis=0)


def _peer_topk_kernel(x_ref, nw_ref, sc_ref, sh_ref, wh_ref, wl_ref, keys_ref,
                      h_out, idx_out, gate_out):
    h = _normmod(x_ref[0], nw_ref[...], sc_ref[0], sh_ref[0])
    h_out[0] = h
    hh, hl = _split(h)
    q = _dot(hh, wh_ref[...]) + (_dot(hh, wl_ref[...]) + _dot(hl, wh_ref[...]))
    half = PEER_DKEY // 2
    idx_rows, gate_rows = [], []
    for hd in range(PEER_HEADS):
        tops = []
        for p in range(2):
            j = hd * 2 + p
            st = _mm(keys_ref[j], q[:, j * half:(j + 1) * half], 3, _dot_nt)
            tops.append(_top16(st))
        (s1, i1), (s2, i2) = tops
        cand = jnp.concatenate([s1[a:a + 1] + s2 for a in range(PEER_TOPK)], axis=0)
        cand_i = jnp.concatenate([i1[a:a + 1] * PEER_NKEYS + i2 for a in range(PEER_TOPK)], axis=0)
        top_s, top_i = _top16(cand, cand_i)
        ex = jnp.exp(top_s - top_s[0:1])
        gate_rows.append(ex / jnp.sum(ex, axis=0, keepdims=True))
        idx_rows.append(top_i)
    idx_out[0] = jnp.concatenate(idx_rows, axis=0).T
    gate_out[0] = jnp.concatenate(gate_rows, axis=0).T


def _peer_topk(x, nw, sc, sh, w_q, keys, *, tb):
    bsz, n, d = x.shape
    nk = PEER_HEADS * PEER_TOPK
    wh = w_q.astype(BF16)
    wl = (w_q - wh.astype(F32)).astype(BF16)
    keys2 = keys.reshape(PEER_HEADS * 2, PEER_NKEYS, PEER_DKEY // 2)
    tok = lambda b, i: (b, i, 0)
    vec = pl.BlockSpec((1, 1, d), lambda b, i: (b, 0, 0))
    full = lambda shape: pl.BlockSpec(shape, lambda b, i: (0,) * len(shape))
    return pl.pallas_call(
        _peer_topk_kernel,
        grid=(bsz, n // tb),
        in_specs=[pl.BlockSpec((1, tb, d), tok), full((1, d)), vec, vec, full(wh.shape), full(wl.shape),
                  full(keys2.shape)],
        out_specs=[pl.BlockSpec((1, tb, d), tok), pl.BlockSpec((1, tb, nk), tok),
                   pl.BlockSpec((1, tb, nk), tok)],
        out_shape=[jax.ShapeDtypeStruct((bsz, n, d), F32), jax.ShapeDtypeStruct((bsz, n, nk), I32),
                   jax.ShapeDtypeStruct((bsz, n, nk), F32)],
        compiler_params=_cparams("parallel", "arbitrary"),
    )(x, nw.reshape(1, d), sc, sh, wh, wl, keys2)


def _pack_table(tab):
    e, d = tab.shape
    bits = lax.bitcast_convert_type(tab.astype(BF16), jnp.uint16).astype(U32)
    packed = (bits[0::2] << 16) | bits[1::2]
    return packed.reshape(e // 2, d // 128, 128)


def _expert_row(tab_ref, e):
    word = tab_ref[lax.shift_right_logical(e, 1)]
    shift = ((e & 1) * 16).astype(U32)
    return lax.bitcast_convert_type((word << shift) & jnp.uint32(0xFFFF0000), F32)


def _gelu_tanh(x):
    return 0.5 * x * (1.0 + jnp.tanh(math.sqrt(2.0 / math.pi) * (x + 0.044715 * (x * x * x))))


def _peer_act_kernel(idx_ref, h_ref, gate_ref, tab_ref, ones_ref, w_out, act_ref):
    tb, nk = gate_ref.shape
    sub = lax.broadcasted_iota(I32, h_ref.shape[1:], 0)
    ones = ones_ref[...]

    def token(t, carry):
        h = h_ref[t]
        groups = []
        for g in range(nk // 8):
            qg = jnp.zeros(h.shape, F32)
            for j in range(8):
                prod = _expert_row(tab_ref, idx_ref[t, g * 8 + j]) * h
                qg = jnp.where(sub == j, jnp.sum(prod, axis=0, keepdims=True), qg)
            groups.append(qg)
        qh, ql = _split(jnp.concatenate(groups, axis=0))
        act = _dot_nt(ones, qh) + _dot_nt(ones, ql)
        act_ref[pl.ds(t, 1), :] = act[0:1]
        return carry

    lax.fori_loop(0, tb, token, 0)
    w_out[...] = gate_ref[...] * _gelu_tanh(act_ref[...])


def _peer_mix_kernel(idx_ref, w_ref, x_ref, g2_ref, tab_ref, o_ref):
    tb, nk = idx_ref.shape
    g2 = g2_ref[0]

    def token(t, carry):
        accs = [jnp.zeros(g2.shape, F32) for _ in range(4)]
        for kk in range(nk):
            accs[kk % 4] = accs[kk % 4] + w_ref[t, kk] * _expert_row(tab_ref, idx_ref[t, kk])
        o_ref[t] = x_ref[t] + g2 * ((accs[0] + accs[1]) + (accs[2] + accs[3]))
        return carry

    lax.fori_loop(0, tb, token, 0)


def _peer_gather(x, h, idx, gate, g2, u_tab, v_tab, *, tb):
    bsz, n, d = x.shape
    t_all = bsz * n
    nk = idx.shape[-1]
    sl = d // 128
    idx2 = idx.reshape(t_all, nk)
    tab_spec = pl.BlockSpec((u_tab.shape[0] // 2, sl, 128), lambda i: (0, 0, 0),
                            pipeline_mode=pl.Buffered(1))
    smem = lambda: pl.BlockSpec((tb, nk), lambda i: (i, 0), memory_space=pltpu.SMEM)
    tok3 = pl.BlockSpec((tb, sl, 128), lambda i: (i, 0, 0))
    w = pl.pallas_call(
        _peer_act_kernel,
        grid=(t_all // tb,),
        in_specs=[smem(), tok3, pl.BlockSpec((tb, nk), lambda i: (i, 0)), tab_spec,
                  pl.BlockSpec((8, 128), lambda i: (0, 0))],
        out_specs=pl.BlockSpec((tb, nk), lambda i: (i, 0)),
        out_shape=jax.ShapeDtypeStruct((t_all, nk), F32),
        scratch_shapes=[pltpu.VMEM((tb, nk), F32)],
        compiler_params=_cparams("arbitrary"),
    )(idx2, h.reshape(t_all, sl, 128), gate.reshape(t_all, nk), _pack_table(u_tab),
      jnp.ones((8, 128), BF16))
    out = pl.pallas_call(
        _peer_mix_kernel,
        grid=(t_all // tb,),
        in_specs=[smem(), smem(), tok3,
                  pl.BlockSpec((1, sl, 128), lambda i: ((i * tb) // n, 0, 0)), tab_spec],
        out_specs=tok3,
        out_shape=jax.ShapeDtypeStruct((t_all, sl, 128), F32),
        compiler_params=_cparams("arbitrary"),
    )(idx2, w, x.reshape(t_all, sl, 128), g2.reshape(bsz, sl, 128), _pack_table(v_tab))
    return out.reshape(bsz, n, d)


def _peer_layer(x, nw, sc, sh, g2, w_q, keys, u_tab, v_tab):
    h, idx, gate = _peer_topk(x, nw, sc, sh, w_q, keys, tb=256)
    return _peer_gather(x, h, idx, gate, g2, u_tab, v_tab, tb=128)


def _final_norm_kernel(x_ref, w_ref, o_ref):
    x = x_ref[0]
    o_ref[0] = x * lax.rsqrt(jnp.mean(x * x, axis=-1, keepdims=True) + EPS) * w_ref[...]


def _final_norm(x, w, *, tm):
    bsz, n, d = x.shape
    return pl.pallas_call(
        _final_norm_kernel,
        grid=(bsz, n // tm),
        in_specs=[pl.BlockSpec((1, tm, d), lambda b, i: (b, i, 0)), pl.BlockSpec((1, d), lambda b, i: (0, 0))],
        out_specs=pl.BlockSpec((1, tm, d), lambda b, i: (b, i, 0)),
        out_shape=jax.ShapeDtypeStruct((bsz, n, d), F32),
        compiler_params=_cparams("parallel", "arbitrary"),
    )(x, w.reshape(1, d))


def _rope_tables(n_ctx, n_lat):
    pos = np.arange(n_lat)
    axis_dim = B_DH // 2
    inv = ROPE_BASE ** (-np.arange(0, axis_dim, 2, dtype=np.float32) / axis_dim)
    ang = np.concatenate([(pos // GRID_W)[:, None] * inv, (pos % GRID_W)[:, None] * inv], axis=-1)
    ang = np.concatenate([np.zeros((n_ctx, ang.shape[1])), ang], axis=0).astype(np.float32)
    cos = np.cos(ang)
    sin = np.sin(ang)
    cos64 = np.concatenate([cos, cos], axis=1)
    sin64 = np.concatenate([-sin, sin], axis=1)
    return (jnp.asarray(np.tile(cos64, (1, 2)), F32), jnp.asarray(np.tile(sin64, (1, 2)), F32))


def _mixer_ab_layer(x, ctx, mod, mod_c, layer, norm1_w, w_in, conv_w, a_log, dt_bias, gdn_norm_w,
                    lam_params, subln_w, w_out):
    bsz, n_lat, d = x.shape
    n_ctx = ctx.shape[1]
    n = n_ctx + n_lat
    tm = GDN_BLOCK
    lam_init = 0.8 - 0.6 * math.exp(-0.3 * layer)

    xc = jnp.concatenate([ctx, x], axis=1)
    seg = lambda j: jnp.stack([jnp.broadcast_to(mod_c[j], (bsz, d)), mod[:, j]], axis=1)[:, :, None, :]
    cuts = np.cumsum([0, 3 * A_W, A_W, 4 * A_HEADS, B_W, B_W, B_W])
    weights = [w_in[:, a:b] for a, b in zip(cuts[:-1], cuts[1:])]
    qkv, gate, ba, bq, bk, bv = _nm_matmul(
        xc, norm1_w, seg(1), seg(0), weights, [F32, F32, F32, F32, F32, BF16],
        tm=tm, n_first=n_ctx // tm)

    last = qkv[:, tm - 1::tm]
    first = qkv[:, 0::tm]
    zero = jnp.zeros_like(last[:, :1])
    prev_rows = jnp.concatenate([zero, last[:, :-1]], axis=1)
    next_rows = jnp.concatenate([first[:, 1:], zero], axis=1)
    blk = jnp.arange(n // tm)[None, :, None]
    prev_rows = jnp.where(blk == n_ctx // tm, 0.0, prev_rows)[:, :, None, :]
    next_rows = jnp.where(blk == n_ctx // tm - 1, 0.0, next_rows)[:, :, None, :]
    cos128, sin128 = _rope_tables(n_ctx, n_lat)
    q, k, v, bqr, bkr = _prep(qkv, prev_rows, next_rows, conv_w, bq, bk, cos128, sin128, tm=tm)

    bat = jnp.swapaxes(ba, 1, 2)
    o_f = _gdn(q, k, v, ba, bat, a_log, dt_bias, d=0, n_ctx=n_ctx, inv_passes=3)
    o_b = _gdn(q, k, v, ba, bat, a_log, dt_bias, d=1, n_ctx=n_ctx, inv_passes=3)
    d_lat = _diff_attention(bqr, bkr, bv, lam_params, lam_init=lam_init, n_ctx=n_ctx, tq=256,
                            tk=n // 3 if n % (3 * 128) == 0 else n)
    return _outproj(o_f, o_b, gate, d_lat, x, mod[:, 2][:, None, :], gdn_norm_w, subln_w, w_out,
                    lam_init=lam_init, n_ctx=n_ctx, tm=tm)


def kernel(x, c, ctx, c_ctx, ada_w, ada_b, norm1_w, norm2_w, w_in, conv_w, a_log, dt_bias, gdn_norm_w,
           lam_q1, lam_k1, lam_q2, lam_k2, subln_w, w_out_ab, w_out_f, peer_wq, peer_keys, peer_u,
           peer_v, final_norm_w):
    bsz, n_lat, d = x.shape
    depth = ada_w.shape[0]
    last_ctx_reader = 2 * ((depth - 1) // 2)
    cmat = jnp.concatenate([c, c_ctx[None, :], jnp.zeros((8 - bsz - 1, d), F32)], axis=0)
    for i in range(depth):
        if i < last_ctx_reader:
            raise NotImplementedError("context stream advance (depth > 2) is not implemented")
        mod_all = _modulation(cmat, ada_w[i], ada_b[i])
        mod = mod_all[:bsz].reshape(bsz, 6, d)
        mod_c = mod_all[bsz].reshape(6, d)
        vec = lambda j: mod[:, j][:, None, :]
        j = i // 2
        if i % 2 == 0:
            x = _mixer_ab_layer(x, ctx, mod, mod_c, i, norm1_w[i], w_in[j], conv_w[j], a_log[j], dt_bias[j],
                                gdn_norm_w[j], (lam_q1[j], lam_k1[j], lam_q2[j], lam_k2[j]), subln_w[j],
                                w_out_ab[j])
        else:
            x = _fourier_layer(x, norm1_w[i], vec(1), vec(0), vec(2), w_out_f[j])
        x = _peer_layer(x, norm2_w[i], vec(4), vec(3), vec(5), peer_wq[i], peer_keys[i], peer_u[i],
                        peer_v[i])
    return _final_norm(x, final_norm_w, tm=512)
```

```python
import functools
import math

import numpy as np
import jax
import jax.numpy as jnp
from jax import lax
from jax.experimental import pallas as pl
from jax.experimental.pallas import tpu as pltpu

F32 = jnp.float32
BF16 = jnp.bfloat16
I32 = jnp.int32
U32 = jnp.uint32
HIGHEST = lax.Precision.HIGHEST

EPS = 1e-6
ROPE_BASE = 10000.0
GRID_W = 64
A_HEADS, A_DK, A_DV, A_CHUNK = 8, 64, 64, 64
B_HEADS, B_DH = 4, 64
C_GROUPS = 4
PEER_HEADS, PEER_NKEYS, PEER_DKEY, PEER_TOPK = 8, 128, 256, 16
A_W = A_HEADS * A_DK
B_W = B_HEADS * 2 * B_DH
GDN_BLOCK = 256
DFT_N1 = 128

VMEM_LIMIT_BYTES = 48 * 1024 * 1024


def _cparams(*sem):
    return pltpu.CompilerParams(dimension_semantics=sem, vmem_limit_bytes=VMEM_LIMIT_BYTES)


def _dot(a, b):
    return lax.dot_general(a, b, (((1,), (0,)), ((), ())), preferred_element_type=F32)


def _dot_nt(a, b):
    return lax.dot_general(a, b, (((1,), (1,)), ((), ())), preferred_element_type=F32)


def _dot_tn(a, b):
    return lax.dot_general(a, b, (((0,), (0,)), ((), ())), preferred_element_type=F32)


def _dot_f32(a, b):
    return lax.dot_general(a, b, (((1,), (0,)), ((), ())), precision=HIGHEST,
                           preferred_element_type=F32)


def _split(a):
    hi = a.astype(BF16)
    lo = (a - hi.astype(F32)).astype(BF16)
    return hi, lo


def _mm(a, b, passes=1, dot=_dot):
    if passes == 1:
        return dot(a.astype(BF16), b.astype(BF16))
    ah, al = _split(a)
    bh, bl = _split(b)
    return dot(ah, bh) + (dot(ah, bl) + dot(al, bh))


def _mm_lhs2(a, b_bf16):
    ah, al = _split(a)
    return _dot(ah, b_bf16) + _dot(al, b_bf16)


def _silu(x):
    return x * jax.nn.sigmoid(x)


def _softplus(x):
    return jnp.maximum(x, 0.0) + jnp.log(1.0 + jnp.exp(-jnp.abs(x)))


def _normmod(x, nw, sc, sh):
    ms = jnp.mean(x * x, axis=-1, keepdims=True)
    return (x * lax.rsqrt(ms + EPS) * nw) * (1.0 + sc) + sh


def _block_ones(n, blk):
    i = np.arange(n) // blk
    return jnp.asarray((i[:, None] == i[None, :]).astype(np.float32), dtype=BF16)


def _mod_kernel(c_ref, w_ref, b_ref, o_ref):
    o_ref[...] = _dot_f32(_silu(c_ref[...]), w_ref[...]) + b_ref[...]


def _modulation(cmat, w, b):
    rows, d = cmat.shape
    n = w.shape[1]
    return pl.pallas_call(
        _mod_kernel,
        grid=(n // d,),
        in_specs=[pl.BlockSpec((rows, d), lambda j: (0, 0)),
                  pl.BlockSpec((d, d), lambda j: (0, j)),
                  pl.BlockSpec((1, d), lambda j: (0, j))],
        out_specs=pl.BlockSpec((rows, d), lambda j: (0, j)),
        out_shape=jax.ShapeDtypeStruct((rows, n), F32),
        compiler_params=_cparams("arbitrary"),
    )(cmat, w, b.reshape(1, n))


def _nm_matmul_kernel(*refs, widths, passes):
    x_ref, nw_ref, sc_ref, sh_ref = refs[:4]
    n_w = len(widths)
    w_refs = refs[4:4 + n_w * (2 if passes == 3 else 1)]
    outs = refs[4 + len(w_refs):]
    h = _normmod(x_ref[0], nw_ref[...], sc_ref[0, 0], sh_ref[0, 0])
    if passes == 1:
        hb = h.astype(BF16)
        for j in range(n_w):
            outs[j][0] = _dot(hb, w_refs[j][...]).astype(outs[j].dtype)
    else:
        hh, hl = _split(h)
        for j in range(n_w):
            wh, wl = w_refs[2 * j][...], w_refs[2 * j + 1][...]
            outs[j][0] = (_dot(hh, wh) + (_dot(hh, wl) + _dot(hl, wh))).astype(outs[j].dtype)
    if len(outs) > n_w:
        outs[n_w][0] = h


def _nm_matmul(x, nw, sc, sh, weights, out_dtypes, *, tm, n_first=0, passes=1, emit_h=False):
    bsz, n, d = x.shape
    seg = sc.shape[1]
    widths = tuple(w.shape[1] for w in weights)
    w_in = []
    for w in weights:
        if passes == 3:
            wh = w.astype(BF16)
            w_in += [wh, (w - wh.astype(F32)).astype(BF16)]
        else:
            w_in.append(w.astype(BF16))

    def mod_map(b, i):
        return (b, jnp.where(i < n_first, 0, seg - 1), 0, 0)

    in_specs = [pl.BlockSpec((1, tm, d), lambda b, i: (b, i, 0)),
                pl.BlockSpec((1, d), lambda b, i: (0, 0)),
                pl.BlockSpec((1, 1, 1, d), mod_map),
                pl.BlockSpec((1, 1, 1, d), mod_map)]
    in_specs += [pl.BlockSpec(w.shape, lambda b, i: (0, 0)) for w in w_in]
    out_specs = [pl.BlockSpec((1, tm, wd), lambda b, i: (b, i, 0)) for wd in widths]
    out_shape = [jax.ShapeDtypeStruct((bsz, n, wd), dt) for wd, dt in zip(widths, out_dtypes)]
    if emit_h:
        out_specs.append(pl.BlockSpec((1, tm, d), lambda b, i: (b, i, 0)))
        out_shape.append(jax.ShapeDtypeStruct((bsz, n, d), F32))
    return pl.pallas_call(
        functools.partial(_nm_matmul_kernel, widths=widths, passes=passes),
        grid=(bsz, n // tm),
        in_specs=in_specs, out_specs=out_specs, out_shape=out_shape,
        compiler_params=_cparams("parallel", "arbitrary"),
    )(x, nw.reshape(1, d), sc, sh, *w_in)


def _prep_kernel(qkv_ref, prev_ref, next_ref, cw_ref, bd_ref, bq_ref, bk_ref, cos_ref, sin_ref,
                 q_out, k_out, v_out, bqr_out, bkr_out):
    x = qkv_ref[0]
    tm = x.shape[0]
    row = lax.broadcasted_iota(I32, x.shape, 0)
    xm = jnp.where(row == 0, prev_ref[0, 0], pltpu.roll(x, 1, 0))
    xp = jnp.where(row == tm - 1, next_ref[0, 0], pltpu.roll(x, tm - 1, 0))
    cw = cw_ref[...]
    y = _silu(xm * cw[0:1] + x * cw[1:2] + xp * cw[2:3])
    bd = bd_ref[...]

    def l2(t):
        return t * lax.rsqrt(_mm_lhs2(t * t, bd) + EPS)

    q_out[0] = l2(y[:, :A_W]) * (A_DK ** -0.5)
    k_out[0] = l2(y[:, A_W:2 * A_W])
    v_out[0] = y[:, 2 * A_W:]

    cos = jnp.concatenate([cos_ref[...]] * (B_W // 128), axis=1)
    sin = jnp.concatenate([sin_ref[...]] * (B_W // 128), axis=1)
    lane = lax.broadcasted_iota(I32, (tm, B_W), 1)
    first = (lane & (B_DH - 1)) < (B_DH // 2)

    def rope(t):
        rot = jnp.where(first, pltpu.roll(t, B_W - B_DH // 2, 1), pltpu.roll(t, B_DH // 2, 1))
        return t * cos + rot * sin

    bqr_out[0] = (rope(bq_ref[0]) * (B_DH ** -0.5)).astype(BF16)
    bkr_out[0] = rope(bk_ref[0]).astype(BF16)


def _prep(qkv, prev_rows, next_rows, conv_w, bq, bk, cos128, sin128, *, tm):
    bsz, n, cw = qkv.shape
    tok = lambda b, i: (b, i, 0)
    halo = pl.BlockSpec((1, 1, 1, cw), lambda b, i: (b, i, 0, 0))
    return pl.pallas_call(
        _prep_kernel,
        grid=(bsz, n // tm),
        in_specs=[pl.BlockSpec((1, tm, cw), tok), halo, halo,
                  pl.BlockSpec(conv_w.shape, lambda b, i: (0, 0)),
                  pl.BlockSpec((A_W, A_W), lambda b, i: (0, 0)),
                  pl.BlockSpec((1, tm, B_W), tok), pl.BlockSpec((1, tm, B_W), tok),
                  pl.BlockSpec((tm, 128), lambda b, i: (i, 0)),
                  pl.BlockSpec((tm, 128), lambda b, i: (i, 0))],
        out_specs=[pl.BlockSpec((1, tm, A_W), tok)] * 3 + [pl.BlockSpec((1, tm, B_W), tok)] * 2,
        out_shape=[jax.ShapeDtypeStruct((bsz, n, A_W), F32)] * 3
        + [jax.ShapeDtypeStruct((bsz, n, B_W), BF16)] * 2,
        compiler_params=_cparams("parallel", "arbitrary"),
    )(qkv, prev_rows, next_rows, conv_w, _block_ones(A_W, A_DK), bq, bk, cos128, sin128)


def _tri_inverse(a, eye, d16, o32, o64, passes):
    n = -(a * d16)
    x = eye + n
    p = n
    for _ in range(3):
        p = _mm(p, p, passes)
        x = x + _mm(x, p, passes)
    for om in (o32, o64):
        x = x - _mm(_mm(x, a * om, passes), x, passes)
    return x


def _gdn_kernel(q_ref, k_ref, v_ref, ba_ref, bat_ref, alr_ref, dtr_ref, alc_ref, dtc_ref,
                lm_ref, lmt_ref, ltot_ref, e_ref, o_ref, s_ref, *, d, inv_passes):
    rev = d == 1
    nh = A_HEADS

    @pl.when(pl.program_id(1) == 0)
    def _():
        s_ref[...] = jnp.zeros_like(s_ref)

    q, k, v = q_ref[0], k_ref[0], v_ref[0]
    ba, bat = ba_ref[0], bat_ref[0]
    n = q.shape[0]
    beta_c = jax.nn.sigmoid(ba[:, d * nh:(d + 1) * nh])
    g_c = -jnp.exp(alr_ref[...]) * _softplus(ba[:, 2 * nh + d * nh:2 * nh + (d + 1) * nh] + dtr_ref[...])
    g_t = -jnp.exp(alc_ref[...]) * _softplus(bat[2 * nh + d * nh:2 * nh + (d + 1) * nh, :] + dtc_ref[...])
    lm, ltot, e = lm_ref[...], ltot_ref[...], e_ref[...]
    gcum_c = _dot_f32(lm, g_c)
    gcum_t = _dot_f32(g_t, lmt_ref[...])
    gcum_e = _dot_f32(gcum_c, e)
    gtot_e = _dot_f32(_dot_f32(ltot, g_c), e)
    beta_e = _dot_f32(beta_c, e)
    eg = jnp.exp(gcum_e)
    kb = k * beta_e
    vb = v * beta_e
    kbe = kb * eg
    qd = q * eg
    ktail = k * jnp.exp(gtot_e - gcum_e)
    egt = jnp.exp(gtot_e)

    ri = lax.broadcasted_iota(I32, (n, n), 0)
    ci = lax.broadcasted_iota(I32, (n, n), 1)
    eye = (ri == ci).astype(F32)
    same = lambda s: (ri >> s) == (ci >> s)
    d16 = same(4).astype(F32)
    o32 = (same(5) & ~same(4)).astype(F32)
    o64 = (same(6) & ~same(5)).astype(F32)
    incl = lm > 0.0
    strict = lm * (1.0 - eye)

    order = range(n // A_CHUNK - 1, -1, -1) if rev else range(n // A_CHUNK)
    heads = []
    for h in range(nh):
        sl = slice(h * A_DK, (h + 1) * A_DK)
        decay = jnp.exp(jnp.where(incl, gcum_c[:, h:h + 1] - gcum_t[h:h + 1, :], -1e30))
        a = _mm(kb[:, sl], k[:, sl], 1, _dot_nt) * decay * strict
        t_inv = _tri_inverse(a, eye, d16, o32, o64, inv_passes)
        uw = _mm(t_inv, jnp.concatenate([vb[:, sl], kbe[:, sl]], axis=1))
        attn = _mm(q[:, sl], k[:, sl], 1, _dot_nt) * decay
        s = s_ref[h]
        parts = {}
        for c in order:
            r = slice(c * A_CHUNK, (c + 1) * A_CHUNK)
            v_new = uw[r, :A_DV] - _mm(uw[r, A_DV:], s)
            parts[c] = _mm(qd[r, sl], s) + _mm(attn[r, r], v_new)
            s = s * egt[c * A_CHUNK:c * A_CHUNK + 1, sl] + _mm(ktail[r, sl], v_new, 1, _dot_tn)
        s_ref[h] = s
        heads.append(jnp.concatenate([parts[c] for c in range(n // A_CHUNK)], axis=0))
    o_ref[0] = jnp.concatenate(heads, axis=1)


def _gdn_constants(d):
    n = GDN_BLOCK
    i = np.arange(n)
    same = (i[:, None] // A_CHUNK) == (i[None, :] // A_CHUNK)
    tri = (i[None, :] >= i[:, None]) if d == 1 else (i[None, :] <= i[:, None])
    lm = (same & tri).astype(np.float32)
    e = (np.arange(A_W)[None, :] // A_DK == np.arange(A_HEADS)[:, None]).astype(np.float32)
    return jnp.asarray(lm), jnp.asarray(lm.T.copy()), jnp.asarray(same.astype(np.float32)), jnp.asarray(e)


def _gdn(q, k, v, ba, bat, a_log, dt_bias, *, d, n_ctx, inv_passes):
    bsz, n, _ = q.shape
    nb, nc = n // GDN_BLOCK, n_ctx // GDN_BLOCK
    blk = (lambda s: jnp.where(s < nc, nc - 1 - s, nb - 1 + nc - s)) if d == 1 else (lambda s: s)
    tok = lambda b, s: (b, blk(s), 0)
    full = lambda shape: pl.BlockSpec(shape, lambda b, s: (0,) * len(shape))
    lm, lmt, ltot, e = _gdn_constants(d)
    return pl.pallas_call(
        functools.partial(_gdn_kernel, d=d, inv_passes=inv_passes),
        grid=(bsz, nb),
        in_specs=[pl.BlockSpec((1, GDN_BLOCK, A_W), tok)] * 3
        + [pl.BlockSpec((1, GDN_BLOCK, ba.shape[2]), tok),
           pl.BlockSpec((1, bat.shape[1], GDN_BLOCK), lambda b, s: (b, 0, blk(s))),
           full((1, A_HEADS)), full((1, A_HEADS)), full((A_HEADS, 1)), full((A_HEADS, 1)),
           full(lm.shape), full(lm.shape), full(lm.shape), full(e.shape)],
        out_specs=pl.BlockSpec((1, GDN_BLOCK, A_W), tok),
        out_shape=jax.ShapeDtypeStruct((bsz, n, A_W), F32),
        scratch_shapes=[pltpu.VMEM((A_HEADS, A_DK, A_DV), F32)],
        compiler_params=_cparams("parallel", "arbitrary"),
    )(q, k, v, ba, bat, a_log[d].reshape(1, -1), dt_bias[d].reshape(1, -1),
      a_log[d].reshape(-1, 1), dt_bias[d].reshape(-1, 1), lm, lmt, ltot, e)


def _attn_kernel(q_ref, k_ref, v_ref, lq1_ref, lk1_ref, lq2_ref, lk2_ref, o_ref, *, lam_init, tk):
    q = q_ref[0]
    tq = q.shape[0]
    nk = k_ref.shape[1]
    lam = (jnp.exp(jnp.sum(lq1_ref[...] * lk1_ref[...], keepdims=True))
           - jnp.exp(jnp.sum(lq2_ref[...] * lk2_ref[...], keepdims=True)) + lam_init)
    outs = []
    for m in range(2):
        qm = q[:, m * B_DH:(m + 1) * B_DH]
        mx = jnp.full((tq, 1), -jnp.inf, F32)
        l = jnp.zeros((tq, 1), F32)
        acc = jnp.zeros((tq, 2 * B_DH), F32)
        for c in range(nk // tk):
            kc = k_ref[0, c * tk:(c + 1) * tk, m * B_DH:(m + 1) * B_DH]
            s = _dot_nt(qm, kc)
            m_new = jnp.maximum(mx, jnp.max(s, axis=-1, keepdims=True))
            alpha = jnp.exp(mx - m_new)
            p = jnp.exp(s - m_new)
            l = alpha * l + jnp.sum(p, axis=-1, keepdims=True)
            acc = alpha * acc + _dot(p.astype(BF16), v_ref[0, c * tk:(c + 1) * tk, :])
            mx = m_new
        outs.append(acc / l)
    o_ref[0] = outs[0] - lam * outs[1]


def _diff_attention(bqr, bkr, bv, lam_params, *, lam_init, n_ctx, tq, tk):
    bsz, n, _ = bqr.shape
    n_lat = n - n_ctx
    hw = 2 * B_DH
    lam_spec = pl.BlockSpec((1, B_DH), lambda b, h, i: (0, 0))
    return pl.pallas_call(
        functools.partial(_attn_kernel, lam_init=lam_init, tk=tk),
        grid=(bsz, B_HEADS, n_lat // tq),
        in_specs=[pl.BlockSpec((1, tq, hw), lambda b, h, i: (b, i + n_ctx // tq, h)),
                  pl.BlockSpec((1, n, hw), lambda b, h, i: (b, 0, h)),
                  pl.BlockSpec((1, n, hw), lambda b, h, i: (b, 0, h))] + [lam_spec] * 4,
        out_specs=pl.BlockSpec((1, tq, hw), lambda b, h, i: (b, i, h)),
        out_shape=jax.ShapeDtypeStruct((bsz, n_lat, B_W), F32),
        compiler_params=_cparams("parallel", "parallel", "arbitrary"),
    )(bqr, bkr, bv, *[p.reshape(1, B_DH) for p in lam_params])


def _outproj_kernel(of_ref, ob_ref, gate_ref, d_ref, x_ref, g1_ref, gw_ref, sw_ref, bd64_ref,
                    bd128_ref, w_ref, o_ref, *, lam_init):
    o = of_ref[0] + ob_ref[0]
    ms = _mm_lhs2(o * o, bd64_ref[...]) * (1.0 / A_DV)
    y1 = (o * lax.rsqrt(ms + EPS) * gw_ref[...]) * _silu(gate_ref[0])
    dd = d_ref[0]
    ms2 = _mm_lhs2(dd * dd, bd128_ref[...]) * (1.0 / (2 * B_DH))
    y2 = (dd * lax.rsqrt(ms2 + EPS) * sw_ref[...]) * (1.0 - lam_init)
    ycat = jnp.concatenate([y1, y2], axis=1).astype(BF16)
    o_ref[0] = x_ref[0] + g1_ref[0] * _dot(ycat, w_ref[...])


def _outproj(o_f, o_b, gate, d_lat, x, g1, gdn_norm_w, subln_w, w_out, *, lam_init, n_ctx, tm):
    bsz, n_lat, d = x.shape
    off = n_ctx // tm
    tok = lambda b, i: (b, i, 0)
    tok_off = lambda b, i: (b, i + off, 0)
    full = lambda shape: pl.BlockSpec(shape, lambda b, i: (0,) * len(shape))
    return pl.pallas_call(
        functools.partial(_outproj_kernel, lam_init=lam_init),
        grid=(bsz, n_lat // tm),
        in_specs=[pl.BlockSpec((1, tm, A_W), tok_off)] * 3
        + [pl.BlockSpec((1, tm, B_W), tok), pl.BlockSpec((1, tm, d), tok),
           pl.BlockSpec((1, 1, d), lambda b, i: (b, 0, 0)),
           full((1, A_W)), full((1, B_W)), full((A_W, A_W)), full((B_W, B_W)), full(w_out.shape)],
        out_specs=pl.BlockSpec((1, tm, d), tok),
        out_shape=jax.ShapeDtypeStruct((bsz, n_lat, d), F32),
        compiler_params=_cparams("parallel", "arbitrary"),
    )(o_f, o_b, gate, d_lat, x, g1, jnp.tile(gdn_norm_w, A_HEADS).reshape(1, A_W),
      jnp.tile(subln_w, B_HEADS).reshape(1, B_W), _block_ones(A_W, A_DV), _block_ones(B_W, 2 * B_DH),
      w_out.astype(BF16))


def _dft1_kernel(x_ref, nw_ref, sc_ref, sh_ref, f_ref, twr_ref, twi_ref, re_ref, im_ref, *, d, passes):
    n1 = f_ref.shape[1]
    f = f_ref[...]
    for t in range(x_ref.shape[2] // d):
        cs = slice(t * d, (t + 1) * d)
        h = _normmod(x_ref[0, :, cs], nw_ref[...], sc_ref[0], sh_ref[0])
        a = _mm(f, h, passes)
        a_re, a_im = a[:n1], a[n1:]
        twr = jnp.concatenate([twr_ref[:, t * 128:(t + 1) * 128]] * (d // 128), axis=1)
        twi = jnp.concatenate([twi_ref[:, t * 128:(t + 1) * 128]] * (d // 128), axis=1)
        re_ref[0, :, cs] = a_re * twr - a_im * twi
        im_ref[0, :, cs] = a_re * twi + a_im * twr


def _dft2_kernel(re_ref, im_ref, x_ref, g1_ref, f_ref, cc_ref, sc_ref, w_ref, o_ref, *, d, passes):
    n2 = f_ref.shape[1]
    f = f_ref[...]
    dg = d // C_GROUPS
    for t in range(re_ref.shape[1]):
        pr = _mm(f, re_ref[0, t], passes)
        pi = _mm(f, im_ref[0, t], passes)
        x_re = pr[:n2] + pi[n2:]
        x_im = pi[:n2] - pr[n2:]
        y = jnp.concatenate(
            [_mm(x_re[:, g * dg:(g + 1) * dg], cc_ref[...], passes)
             + _mm(x_im[:, g * dg:(g + 1) * dg], sc_ref[...], passes) for g in range(C_GROUPS)], axis=1)
        cs = slice(t * d, (t + 1) * d)
        o_ref[0, :, cs] = x_ref[0, :, cs] + g1_ref[0] * _dot(y.astype(BF16), w_ref[...])


def _fourier_layer(x, nw, sc, sh, g1, w_out, *, passes=3, nb=2, kb=8):
    bsz, n, d = x.shape
    n1, n2 = DFT_N1, n // DFT_N1
    dg = d // C_GROUPS
    k1 = np.arange(n1)
    ang1 = 2 * np.pi * np.outer(k1, k1) / n1
    f1 = jnp.asarray(np.concatenate([np.cos(ang1), -np.sin(ang1)], 0), F32)
    angt = 2 * np.pi * np.outer(k1, np.arange(n2)) / n
    twr = jnp.asarray(np.repeat(np.cos(angt), 128, axis=1), F32)
    twi = jnp.asarray(np.repeat(-np.sin(angt), 128, axis=1), F32)
    k2 = np.arange(n2)
    ang2 = 2 * np.pi * np.outer(k2, k2) / n2
    f2 = jnp.asarray(np.concatenate([np.cos(ang2), np.sin(ang2)], 0), F32)
    angc = 2 * np.pi * np.outer(np.arange(dg), np.arange(dg)) / dg
    scale = 1.0 / math.sqrt(n * dg)
    cc = jnp.asarray(np.cos(angc) * scale, F32)
    sn = jnp.asarray(np.sin(angc) * scale, F32)

    xv = x.reshape(bsz, n1, n2 * d)
    col = lambda b, j: (b, 0, j)
    vec = pl.BlockSpec((1, 1, d), lambda b, j: (b, 0, 0))
    full = lambda shape: pl.BlockSpec(shape, lambda b, j: (0,) * len(shape))
    a_re, a_im = pl.pallas_call(
        functools.partial(_dft1_kernel, d=d, passes=passes),
        grid=(bsz, n2 // nb),
        in_specs=[pl.BlockSpec((1, n1, nb * d), col), full((1, d)), vec, vec, full(f1.shape),
                  pl.BlockSpec((n1, nb * 128), lambda b, j: (0, j)),
                  pl.BlockSpec((n1, nb * 128), lambda b, j: (0, j))],
        out_specs=[pl.BlockSpec((1, n1, nb * d), col)] * 2,
        out_shape=[jax.ShapeDtypeStruct((bsz, n1, n2 * d), F32)] * 2,
        compiler_params=_cparams("parallel", "arbitrary"),
    )(xv, nw.reshape(1, d), sc, sh, f1, twr, twi)

    xo = x.reshape(bsz, n2, n1 * d)
    out = pl.pallas_call(
        functools.partial(_dft2_kernel, d=d, passes=passes),
        grid=(bsz, n1 // kb),
        in_specs=[pl.BlockSpec((1, kb, n2, d), lambda b, j: (b, j, 0, 0))] * 2
        + [pl.BlockSpec((1, n2, kb * d), col), vec, full(f2.shape), full(cc.shape), full(sn.shape),
           full(w_out.shape)],
        out_specs=pl.BlockSpec((1, n2, kb * d), col),
        out_shape=jax.ShapeDtypeStruct((bsz, n2, n1 * d), F32),
        compiler_params=_cparams("parallel", "arbitrary"),
    )(a_re.reshape(bsz, n1, n2, d), a_im.reshape(bsz, n1, n2, d), xo, g1, f2, cc, sn,
      w_out.astype(BF16))
    return out.reshape(bsz, n, d)


def _top16(s, payload=None):
    rows = lax.broadcasted_iota(I32, s.shape, 0)
    big = s.shape[0]
    vals, idxs = [], []
    for _ in range(PEER_TOPK):
        m = jnp.max(s, axis=0, keepdims=True)
        am = jnp.min(jnp.where(s == m, rows, big), axis=0, keepdims=True)
        sel = rows == am
        vals.append(m)
        if payload is None:
            idxs.append(am)
        else:
            idxs.append(jnp.max(jnp.where(sel, payload, -1), axis=0, keepdims=True))
        s = jnp.where(sel, -jnp.inf, s)
    return jnp.concatenate(vals, axis=0), jnp.concatenate(idxs, axis=0)


def _peer_topk_kernel(x_ref, nw_ref, sc_ref, sh_ref, wh_ref, wl_ref, keys_ref,
                      h_out, row_out, par_out, gate_out):
    h = _normmod(x_ref[0], nw_ref[...], sc_ref[0], sh_ref[0])
    h_out[0] = h
    hh, hl = _split(h)
    q = _dot(hh, wh_ref[...]) + (_dot(hh, wl_ref[...]) + _dot(hl, wh_ref[...]))
    half = PEER_DKEY // 2
    idx_rows, gate_rows = [], []
    for hd in range(PEER_HEADS):
        tops = []
        for p in range(2):
            j = hd * 2 + p
            st = _mm(keys_ref[j], q[:, j * half:(j + 1) * half], 3, _dot_nt)
            tops.append(_top16(st))
        (s1, i1), (s2, i2) = tops
        cand = jnp.concatenate([s1[a:a + 1] + s2 for a in range(PEER_TOPK)], axis=0)
        cand_i = jnp.concatenate([i1[a:a + 1] * PEER_NKEYS + i2 for a in range(PEER_TOPK)], axis=0)
        top_s, top_i = _top16(cand, cand_i)
        ex = jnp.exp(top_s - top_s[0:1])
        gate_rows.append(ex / jnp.sum(ex, axis=0, keepdims=True))
        idx_rows.append(top_i)
    idx = jnp.concatenate(idx_rows, axis=0).T
    row_out[0] = lax.shift_right_logical(idx, 1) * SUB
    par_out[0] = idx & 1
    gate_out[0] = jnp.concatenate(gate_rows, axis=0).T


def _peer_topk(x, nw, sc, sh, w_q, keys, *, tb):
    bsz, n, d = x.shape
    nk = PEER_HEADS * PEER_TOPK
    wh = w_q.astype(BF16)
    wl = (w_q - wh.astype(F32)).astype(BF16)
    keys2 = keys.reshape(PEER_HEADS * 2, PEER_NKEYS, PEER_DKEY // 2)
    tok = lambda b, i: (b, i, 0)
    vec = pl.BlockSpec((1, 1, d), lambda b, i: (b, 0, 0))
    full = lambda shape: pl.BlockSpec(shape, lambda b, i: (0,) * len(shape))
    return pl.pallas_call(
        _peer_topk_kernel,
        grid=(bsz, n // tb),
        in_specs=[pl.BlockSpec((1, tb, d), tok), full((1, d)), vec, vec, full(wh.shape), full(wl.shape),
                  full(keys2.shape)],
        out_specs=[pl.BlockSpec((1, tb, d), tok)] + [pl.BlockSpec((1, tb, nk), tok)] * 3,
        out_shape=[jax.ShapeDtypeStruct((bsz, n, d), F32), jax.ShapeDtypeStruct((bsz, n, nk), I32),
                   jax.ShapeDtypeStruct((bsz, n, nk), I32), jax.ShapeDtypeStruct((bsz, n, nk), F32)],
        compiler_params=_cparams("parallel", "arbitrary"),
    )(x, nw.reshape(1, d), sc, sh, wh, wl, keys2)


SUB = 8
PAIR_COLS = 2 * SUB
GATHER_UNROLL = 4


def _pack_table(tab):
    e, d = tab.shape
    bits = lax.bitcast_convert_type(tab.astype(BF16), jnp.uint16).astype(U32)
    packed = (bits[0::2] << 16) | bits[1::2]
    return packed.reshape(e // 2 * (d // 128), 128)


def _gather_tiles(tab_ref, row_ref, t, nk):
    tiles = [tab_ref[pl.ds(pl.multiple_of(row_ref[t, kk], SUB), SUB), :] for kk in range(nk)]
    return pltpu.bitcast(jnp.concatenate(tiles, axis=0), BF16)


def _gather_constants(nk):
    c = np.arange(nk * PAIR_COLS)
    pair = (c[None, :] // PAIR_COLS == np.arange(nk)[:, None])
    hi = (c % 2 == 1)[None, :]
    smask = ((c[None, :] % PAIR_COLS) // 2 == np.arange(SUB)[:, None]).astype(np.float32)
    as_bf16 = lambda m: jnp.asarray(m.astype(np.float32), dtype=BF16)
    return dict(e_all=as_bf16(pair), e_hi=as_bf16(pair & hi), e_lo=as_bf16(pair & ~hi),
                collapse=as_bf16(pair.T), smask=jnp.asarray(smask), hi_row=jnp.asarray(hi.astype(np.float32)))


def _gelu_tanh(x):
    return 0.5 * x * (1.0 + jnp.tanh(math.sqrt(2.0 / math.pi) * (x + 0.044715 * (x * x * x))))


def _peer_act_kernel(row_ref, par_ref, h_ref, gate_ref, tab_ref, eall_ref, coll_ref, smask_ref, hi_ref,
                     w_out, rsum_ref):
    tb, nk = gate_ref.shape
    smask = smask_ref[...]

    def token(t, carry):
        g = _gather_tiles(tab_ref, row_ref, t, nk)
        hh, hl = _split(h_ref[t])
        r = _dot_nt(jnp.concatenate([hh, hl], axis=0), g)
        rsum_ref[pl.ds(t, 1), :] = jnp.sum((r[:SUB] + r[SUB:]) * smask, axis=0, keepdims=True)
        return carry

    lax.fori_loop(0, tb, token, 0, unroll=GATHER_UNROLL)
    want_hi = 1.0 - _dot(par_ref[...].astype(F32).astype(BF16), eall_ref[...])
    picked = jnp.where(want_hi == hi_ref[...], rsum_ref[...], 0.0)
    w_out[...] = gate_ref[...] * _gelu_tanh(_mm_lhs2(picked, coll_ref[...]))


def _peer_mix_kernel(row_ref, par_ref, w_ref, x_ref, g2_ref, tab_ref, ehi_ref, elo_ref, smask_ref,
                     o_ref, wcols_ref):
    tb, nk = w_ref.shape
    g2 = g2_ref[0]
    smask = smask_ref[...]
    par = par_ref[...].astype(F32)
    w = w_ref[...]
    wcols_ref[...] = _mm_lhs2(w * (1.0 - par), ehi_ref[...]) + _mm_lhs2(w * par, elo_ref[...])

    def token(t, carry):
        g = _gather_tiles(tab_ref, row_ref, t, nk)
        wh, wl = _split(wcols_ref[pl.ds(t, 1), :] * smask)
        res = _dot(jnp.concatenate([wh, wl], axis=0), g)
        o_ref[t] = x_ref[t] + g2 * (res[:SUB] + res[SUB:])
        return carry

    lax.fori_loop(0, tb, token, 0, unroll=GATHER_UNROLL)


def _peer_gather(x, h, row, par, gate, g2, u_tab, v_tab, *, tb):
    bsz, n, d = x.shape
    t_all = bsz * n
    nk = row.shape[-1]
    sl = d // 128
    assert sl == SUB
    row2, par2 = row.reshape(t_all, nk), par.reshape(t_all, nk)
    cst = _gather_constants(nk)
    ncol = nk * PAIR_COLS
    tab_spec = pl.BlockSpec((u_tab.shape[0] // 2 * sl, 128), lambda i: (0, 0), pipeline_mode=pl.Buffered(1))
    smem = pl.BlockSpec((tb, nk), lambda i: (i, 0), memory_space=pltpu.SMEM)
    tokk = pl.BlockSpec((tb, nk), lambda i: (i, 0))
    tok3 = pl.BlockSpec((tb, sl, 128), lambda i: (i, 0, 0))
    full = lambda a: pl.BlockSpec(a.shape, lambda i: (0,) * a.ndim)
    w = pl.pallas_call(
        _peer_act_kernel,
        grid=(t_all // tb,),
        in_specs=[smem, tokk, tok3, tokk, tab_spec, full(cst["e_all"]), full(cst["collapse"]),
                  full(cst["smask"]), full(cst["hi_row"])],
        out_specs=tokk,
        out_shape=jax.ShapeDtypeStruct((t_all, nk), F32),
        scratch_shapes=[pltpu.VMEM((tb, ncol), F32)],
        compiler_params=_cparams("arbitrary"),
    )(row2, par2, h.reshape(t_all, sl, 128), gate.reshape(t_all, nk), _pack_table(u_tab),
      cst["e_all"], cst["collapse"], cst["smask"], cst["hi_row"])
    out = pl.pallas_call(
        _peer_mix_kernel,
        grid=(t_all // tb,),
        in_specs=[smem, tokk, tokk, tok3, pl.BlockSpec((1, sl, 128), lambda i: ((i * tb) // n, 0, 0)),
                  tab_spec, full(cst["e_hi"]), full(cst["e_lo"]), full(cst["smask"])],
        out_specs=tok3,
        out_shape=jax.ShapeDtypeStruct((t_all, sl, 128), F32),
        scratch_shapes=[pltpu.VMEM((tb, ncol), F32)],
        compiler_params=_cparams("arbitrary"),
    )(row2, par2, w, x.reshape(t_all, sl, 128), g2.reshape(bsz, sl, 128), _pack_table(v_tab),
      cst["e_hi"], cst["e_lo"], cst["smask"])
    return out.reshape(bsz, n, d)


def _peer_layer(x, nw, sc, sh, g2, w_q, keys, u_tab, v_tab):
    h, row, par, gate = _peer_topk(x, nw, sc, sh, w_q, keys, tb=256)
    return _peer_gather(x, h, row, par, gate, g2, u_tab, v_tab, tb=128)


def _final_norm_kernel(x_ref, w_ref, o_ref):
    x = x_ref[0]
    o_ref[0] = x * lax.rsqrt(jnp.mean(x * x, axis=-1, keepdims=True) + EPS) * w_ref[...]


def _final_norm(x, w, *, tm):
    bsz, n, d = x.shape
    return pl.pallas_call(
        _final_norm_kernel,
        grid=(bsz, n // tm),
        in_specs=[pl.BlockSpec((1, tm, d), lambda b, i: (b, i, 0)), pl.BlockSpec((1, d), lambda b, i: (0, 0))],
        out_specs=pl.BlockSpec((1, tm, d), lambda b, i: (b, i, 0)),
        out_shape=jax.ShapeDtypeStruct((bsz, n, d), F32),
        compiler_params=_cparams("parallel", "arbitrary"),
    )(x, w.reshape(1, d))


def _rope_tables(n_ctx, n_lat):
    pos = np.arange(n_lat)
    axis_dim = B_DH // 2
    inv = ROPE_BASE ** (-np.arange(0, axis_dim, 2, dtype=np.float32) / axis_dim)
    ang = np.concatenate([(pos // GRID_W)[:, None] * inv, (pos % GRID_W)[:, None] * inv], axis=-1)
    ang = np.concatenate([np.zeros((n_ctx, ang.shape[1])), ang], axis=0).astype(np.float32)
    cos = np.cos(ang)
    sin = np.sin(ang)
    cos64 = np.concatenate([cos, cos], axis=1)
    sin64 = np.concatenate([-sin, sin], axis=1)
    return (jnp.asarray(np.tile(cos64, (1, 2)), F32), jnp.asarray(np.tile(sin64, (1, 2)), F32))


def _mixer_ab_layer(x, ctx, mod, mod_c, layer, norm1_w, w_in, conv_w, a_log, dt_bias, gdn_norm_w,
                    lam_params, subln_w, w_out):
    bsz, n_lat, d = x.shape
    n_ctx = ctx.shape[1]
    n = n_ctx + n_lat
    tm = GDN_BLOCK
    lam_init = 0.8 - 0.6 * math.exp(-0.3 * layer)

    xc = jnp.concatenate([ctx, x], axis=1)
    seg = lambda j: jnp.stack([jnp.broadcast_to(mod_c[j], (bsz, d)), mod[:, j]], axis=1)[:, :, None, :]
    cuts = np.cumsum([0, 3 * A_W, A_W, 4 * A_HEADS, B_W, B_W, B_W])
    weights = [w_in[:, a:b] for a, b in zip(cuts[:-1], cuts[1:])]
    qkv, gate, ba, bq, bk, bv = _nm_matmul(
        xc, norm1_w, seg(1), seg(0), weights, [F32, F32, F32, F32, F32, BF16],
        tm=tm, n_first=n_ctx // tm)

    last = qkv[:, tm - 1::tm]
    first = qkv[:, 0::tm]
    zero = jnp.zeros_like(last[:, :1])
    prev_rows = jnp.concatenate([zero, last[:, :-1]], axis=1)
    next_rows = jnp.concatenate([first[:, 1:], zero], axis=1)
    blk = jnp.arange(n // tm)[None, :, None]
    prev_rows = jnp.where(blk == n_ctx // tm, 0.0, prev_rows)[:, :, None, :]
    next_rows = jnp.where(blk == n_ctx // tm - 1, 0.0, next_rows)[:, :, None, :]
    cos128, sin128 = _rope_tables(n_ctx, n_lat)
    q, k, v, bqr, bkr = _prep(qkv, prev_rows, next_rows, conv_w, bq, bk, cos128, sin128, tm=tm)

    bat = jnp.swapaxes(ba, 1, 2)
    o_f = _gdn(q, k, v, ba, bat, a_log, dt_bias, d=0, n_ctx=n_ctx, inv_passes=3)
    o_b = _gdn(q, k, v, ba, bat, a_log, dt_bias, d=1, n_ctx=n_ctx, inv_passes=3)
    d_lat = _diff_attention(bqr, bkr, bv, lam_params, lam_init=lam_init, n_ctx=n_ctx, tq=256,
                            tk=n // 3 if n % (3 * 128) == 0 else n)
    return _outproj(o_f, o_b, gate, d_lat, x, mod[:, 2][:, None, :], gdn_norm_w, subln_w, w_out,
                    lam_init=lam_init, n_ctx=n_ctx, tm=tm)


def kernel(x, c, ctx, c_ctx, ada_w, ada_b, norm1_w, norm2_w, w_in, conv_w, a_log, dt_bias, gdn_norm_w,
           lam_q1, lam_k1, lam_q2, lam_k2, subln_w, w_out_ab, w_out_f, peer_wq, peer_keys, peer_u,
           peer_v, final_norm_w):
    bsz, n_lat, d = x.shape
    depth = ada_w.shape[0]
    last_ctx_reader = 2 * ((depth - 1) // 2)
    cmat = jnp.concatenate([c, c_ctx[None, :], jnp.zeros((8 - bsz - 1, d), F32)], axis=0)
    for i in range(depth):
        if i < last_ctx_reader:
            raise NotImplementedError("context stream advance (depth > 2) is not implemented")
        mod_all = _modulation(cmat, ada_w[i], ada_b[i])
        mod = mod_all[:bsz].reshape(bsz, 6, d)
        mod_c = mod_all[bsz].reshape(6, d)
        vec = lambda j: mod[:, j][:, None, :]
        j = i // 2
        if i % 2 == 0:
            x = _mixer_ab_layer(x, ctx, mod, mod_c, i, norm1_w[i], w_in[j], conv_w[j], a_log[j], dt_bias[j],
                                gdn_norm_w[j], (lam_q1[j], lam_k1[j], lam_q2[j], lam_k2[j]), subln_w[j],
                                w_out_ab[j])
        else:
            x = _fourier_layer(x, norm1_w[i], vec(1), vec(0), vec(2), w_out_f[j])
        x = _peer_layer(x, norm2_w[i], vec(4), vec(3), vec(5), peer_wq[i], peer_keys[i], peer_u[i],
                        peer_v[i])
    return _final_norm(x, final_norm_w, tm=512)
```

```python
import functools
import math

import numpy as np
import jax
import jax.numpy as jnp
from jax import lax
from jax.experimental import pallas as pl
from jax.experimental.pallas import tpu as pltpu

F32 = jnp.float32
BF16 = jnp.bfloat16
I32 = jnp.int32
U32 = jnp.uint32
HIGHEST = lax.Precision.HIGHEST

EPS = 1e-6
ROPE_BASE = 10000.0
GRID_W = 64
A_HEADS, A_DK, A_DV, A_CHUNK = 8, 64, 64, 64
B_HEADS, B_DH = 4, 64
C_GROUPS = 4
PEER_HEADS, PEER_NKEYS, PEER_DKEY, PEER_TOPK = 8, 128, 256, 16
A_W = A_HEADS * A_DK
B_W = B_HEADS * 2 * B_DH
GDN_BLOCK = 256
GDN_HEAD_GROUP = 4
DFT_N1 = 128
SUB = 8
PAIR_COLS = 2 * SUB
GATHER_UNROLL = 8

VMEM_LIMIT_BYTES = 48 * 1024 * 1024


def _cparams(*sem):
    return pltpu.CompilerParams(dimension_semantics=sem, vmem_limit_bytes=VMEM_LIMIT_BYTES)


def _dot(a, b):
    return lax.dot_general(a, b, (((1,), (0,)), ((), ())), preferred_element_type=F32)


def _dot_nt(a, b):
    return lax.dot_general(a, b, (((1,), (1,)), ((), ())), preferred_element_type=F32)


def _dot_tn(a, b):
    return lax.dot_general(a, b, (((0,), (0,)), ((), ())), preferred_element_type=F32)


def _dot_f32(a, b):
    return lax.dot_general(a, b, (((1,), (0,)), ((), ())), precision=HIGHEST,
                           preferred_element_type=F32)


def _split(a):
    hi = a.astype(BF16)
    lo = (a - hi.astype(F32)).astype(BF16)
    return hi, lo


def _mm(a, b, passes=1, dot=_dot):
    if passes == 1:
        return dot(a.astype(BF16), b.astype(BF16))
    ah, al = _split(a)
    bh, bl = _split(b)
    return dot(ah, bh) + (dot(ah, bl) + dot(al, bh))


def _mm_lhs2(a, b_bf16):
    ah, al = _split(a)
    return _dot(ah, b_bf16) + _dot(al, b_bf16)


def _silu(x):
    return x * jax.nn.sigmoid(x)


def _softplus(x):
    return jnp.maximum(x, 0.0) + jnp.log(1.0 + jnp.exp(-jnp.abs(x)))


def _normmod(x, nw, sc, sh):
    ms = jnp.mean(x * x, axis=-1, keepdims=True)
    return (x * lax.rsqrt(ms + EPS) * nw) * (1.0 + sc) + sh


def _block_ones(n, blk):
    i = np.arange(n) // blk
    return jnp.asarray((i[:, None] == i[None, :]).astype(np.float32), dtype=BF16)


def _mod_kernel(c_ref, w_ref, b_ref, o_ref):
    o_ref[...] = _dot_f32(_silu(c_ref[...]), w_ref[...]) + b_ref[...]


def _modulation(cmat, w, b):
    rows, d = cmat.shape
    n = w.shape[1]
    return pl.pallas_call(
        _mod_kernel,
        grid=(n // d,),
        in_specs=[pl.BlockSpec((rows, d), lambda j: (0, 0)),
                  pl.BlockSpec((d, d), lambda j: (0, j)),
                  pl.BlockSpec((1, d), lambda j: (0, j))],
        out_specs=pl.BlockSpec((rows, d), lambda j: (0, j)),
        out_shape=jax.ShapeDtypeStruct((rows, n), F32),
        compiler_params=_cparams("arbitrary"),
    )(cmat, w, b.reshape(1, n))


def _nm_matmul_kernel(*refs, widths, passes):
    x_ref, nw_ref, sc_ref, sh_ref = refs[:4]
    n_w = len(widths)
    w_refs = refs[4:4 + n_w * (2 if passes == 3 else 1)]
    outs = refs[4 + len(w_refs):]
    h = _normmod(x_ref[0], nw_ref[...], sc_ref[0, 0], sh_ref[0, 0])
    if passes == 1:
        hb = h.astype(BF16)
        for j in range(n_w):
            outs[j][0] = _dot(hb, w_refs[j][...]).astype(outs[j].dtype)
    else:
        hh, hl = _split(h)
        for j in range(n_w):
            wh, wl = w_refs[2 * j][...], w_refs[2 * j + 1][...]
            outs[j][0] = (_dot(hh, wh) + (_dot(hh, wl) + _dot(hl, wh))).astype(outs[j].dtype)
    if len(outs) > n_w:
        outs[n_w][0] = h


def _nm_matmul(x, nw, sc, sh, weights, out_dtypes, *, tm, n_first=0, passes=1, emit_h=False):
    bsz, n, d = x.shape
    seg = sc.shape[1]
    widths = tuple(w.shape[1] for w in weights)
    w_in = []
    for w in weights:
        if passes == 3:
            wh = w.astype(BF16)
            w_in += [wh, (w - wh.astype(F32)).astype(BF16)]
        else:
            w_in.append(w.astype(BF16))

    def mod_map(b, i):
        return (b, jnp.where(i < n_first, 0, seg - 1), 0, 0)

    in_specs = [pl.BlockSpec((1, tm, d), lambda b, i: (b, i, 0)),
                pl.BlockSpec((1, d), lambda b, i: (0, 0)),
                pl.BlockSpec((1, 1, 1, d), mod_map),
                pl.BlockSpec((1, 1, 1, d), mod_map)]
    in_specs += [pl.BlockSpec(w.shape, lambda b, i: (0, 0)) for w in w_in]
    out_specs = [pl.BlockSpec((1, tm, wd), lambda b, i: (b, i, 0)) for wd in widths]
    out_shape = [jax.ShapeDtypeStruct((bsz, n, wd), dt) for wd, dt in zip(widths, out_dtypes)]
    if emit_h:
        out_specs.append(pl.BlockSpec((1, tm, d), lambda b, i: (b, i, 0)))
        out_shape.append(jax.ShapeDtypeStruct((bsz, n, d), F32))
    return pl.pallas_call(
        functools.partial(_nm_matmul_kernel, widths=widths, passes=passes),
        grid=(bsz, n // tm),
        in_specs=in_specs, out_specs=out_specs, out_shape=out_shape,
        compiler_params=_cparams("parallel", "arbitrary"),
    )(x, nw.reshape(1, d), sc, sh, *w_in)


def _prep_kernel(qkv_ref, prev_ref, next_ref, cw_ref, bd_ref, bq_ref, bk_ref, cos_ref, sin_ref,
                 q_out, k_out, v_out, bqr_out, bkr_out):
    x = qkv_ref[0]
    tm = x.shape[0]
    row = lax.broadcasted_iota(I32, x.shape, 0)
    xm = jnp.where(row == 0, prev_ref[0, 0], pltpu.roll(x, 1, 0))
    xp = jnp.where(row == tm - 1, next_ref[0, 0], pltpu.roll(x, tm - 1, 0))
    cw = cw_ref[...]
    y = _silu(xm * cw[0:1] + x * cw[1:2] + xp * cw[2:3])
    bd = bd_ref[...]

    def l2(t):
        return t * lax.rsqrt(_mm_lhs2(t * t, bd) + EPS)

    q_out[0] = l2(y[:, :A_W]) * (A_DK ** -0.5)
    k_out[0] = l2(y[:, A_W:2 * A_W])
    v_out[0] = y[:, 2 * A_W:]

    cos = jnp.concatenate([cos_ref[...]] * (B_W // 128), axis=1)
    sin = jnp.concatenate([sin_ref[...]] * (B_W // 128), axis=1)
    lane = lax.broadcasted_iota(I32, (tm, B_W), 1)
    first = (lane & (B_DH - 1)) < (B_DH // 2)

    def rope(t):
        rot = jnp.where(first, pltpu.roll(t, B_W - B_DH // 2, 1), pltpu.roll(t, B_DH // 2, 1))
        return t * cos + rot * sin

    bqr_out[0] = (rope(bq_ref[0]) * (B_DH ** -0.5)).astype(BF16)
    bkr_out[0] = rope(bk_ref[0]).astype(BF16)


def _prep(qkv, prev_rows, next_rows, conv_w, bq, bk, cos128, sin128, *, tm):
    bsz, n, cw = qkv.shape
    tok = lambda b, i: (b, i, 0)
    halo = pl.BlockSpec((1, 1, 1, cw), lambda b, i: (b, i, 0, 0))
    return pl.pallas_call(
        _prep_kernel,
        grid=(bsz, n // tm),
        in_specs=[pl.BlockSpec((1, tm, cw), tok), halo, halo,
                  pl.BlockSpec(conv_w.shape, lambda b, i: (0, 0)),
                  pl.BlockSpec((A_W, A_W), lambda b, i: (0, 0)),
                  pl.BlockSpec((1, tm, B_W), tok), pl.BlockSpec((1, tm, B_W), tok),
                  pl.BlockSpec((tm, 128), lambda b, i: (i, 0)),
                  pl.BlockSpec((tm, 128), lambda b, i: (i, 0))],
        out_specs=[pl.BlockSpec((1, tm, A_W), tok)] * 3 + [pl.BlockSpec((1, tm, B_W), tok)] * 2,
        out_shape=[jax.ShapeDtypeStruct((bsz, n, A_W), F32)] * 3
        + [jax.ShapeDtypeStruct((bsz, n, B_W), BF16)] * 2,
        compiler_params=_cparams("parallel", "arbitrary"),
    )(qkv, prev_rows, next_rows, conv_w, _block_ones(A_W, A_DK), bq, bk, cos128, sin128)


def _gdn_kernel(q_ref, k_ref, v_ref, ba_ref, bat_ref, alr_ref, dtr_ref, alc_ref, dtc_ref,
                lm_ref, lmt_ref, ltot_ref, e_ref, o_ref, s_ref, *, d, inv_passes):
    rev = d == 1
    nh = A_HEADS

    @pl.when(pl.program_id(1) == 0)
    def _():
        s_ref[...] = jnp.zeros_like(s_ref)

    q, k, v = q_ref[0], k_ref[0], v_ref[0]
    ba, bat = ba_ref[0], bat_ref[0]
    n = q.shape[0]
    beta_c = jax.nn.sigmoid(ba[:, d * nh:(d + 1) * nh])
    g_c = -jnp.exp(alr_ref[...]) * _softplus(ba[:, 2 * nh + d * nh:2 * nh + (d + 1) * nh] + dtr_ref[...])
    g_t = -jnp.exp(alc_ref[...]) * _softplus(bat[2 * nh + d * nh:2 * nh + (d + 1) * nh, :] + dtc_ref[...])
    lm, ltot, e = lm_ref[...], ltot_ref[...], e_ref[...]
    gcum_c = _dot_f32(lm, g_c)
    gcum_t = _dot_f32(g_t, lmt_ref[...])
    gcum_e = _dot_f32(gcum_c, e)
    gtot_e = _dot_f32(_dot_f32(ltot, g_c), e)
    beta_e = _dot_f32(beta_c, e)
    eg = jnp.exp(gcum_e)
    kb = k * beta_e
    vb = v * beta_e
    kbe = kb * eg
    qd = q * eg
    ktail = k * jnp.exp(gtot_e - gcum_e)
    egt = jnp.exp(gtot_e)

    ri = lax.broadcasted_iota(I32, (n, n), 0)
    ci = lax.broadcasted_iota(I32, (n, n), 1)
    eye = (ri == ci).astype(F32)
    same = lambda s: (ri >> s) == (ci >> s)
    d16 = same(4).astype(F32)
    o32 = (same(5) & ~same(4)).astype(F32)
    o64 = (same(6) & ~same(5)).astype(F32)
    incl = lm > 0.0
    strict = lm * (1.0 - eye)

    order = range(n // A_CHUNK - 1, -1, -1) if rev else range(n // A_CHUNK)
    sls = [slice(h * A_DK, (h + 1) * A_DK) for h in range(nh)]
    heads = [None] * nh
    for g0 in range(0, nh, GDN_HEAD_GROUP):
        hs = range(g0, g0 + GDN_HEAD_GROUP)
        decay = {h: jnp.exp(jnp.where(incl, gcum_c[:, h:h + 1] - gcum_t[h:h + 1, :], -1e30)) for h in hs}
        a = {h: _mm(kb[:, sls[h]], k[:, sls[h]], 1, _dot_nt) * decay[h] * strict for h in hs}
        p = {h: -(a[h] * d16) for h in hs}
        x = {h: eye + p[h] for h in hs}
        for _ in range(3):
            p = {h: _mm(p[h], p[h], inv_passes) for h in hs}
            x = {h: x[h] + _mm(x[h], p[h], inv_passes) for h in hs}
        for om in (o32, o64):
            y = {h: _mm(x[h], a[h] * om, inv_passes) for h in hs}
            x = {h: x[h] - _mm(y[h], x[h], inv_passes) for h in hs}
        uw = {h: _mm(x[h], jnp.concatenate([vb[:, sls[h]], kbe[:, sls[h]]], axis=1)) for h in hs}
        attn = {h: _mm(q[:, sls[h]], k[:, sls[h]], 1, _dot_nt) * decay[h] for h in hs}
        s = {h: s_ref[h] for h in hs}
        parts = {h: {} for h in hs}
        for c in order:
            r = slice(c * A_CHUNK, (c + 1) * A_CHUNK)
            v_new = {h: uw[h][r, :A_DV] - _mm(uw[h][r, A_DV:], s[h]) for h in hs}
            for h in hs:
                parts[h][c] = _mm(qd[r, sls[h]], s[h]) + _mm(attn[h][r, r], v_new[h])
            s = {h: s[h] * egt[c * A_CHUNK:c * A_CHUNK + 1, sls[h]]
                 + _mm(ktail[r, sls[h]], v_new[h], 1, _dot_tn) for h in hs}
        for h in hs:
            s_ref[h] = s[h]
            heads[h] = jnp.concatenate([parts[h][c] for c in range(n // A_CHUNK)], axis=0)
    o_ref[0] = jnp.concatenate(heads, axis=1)


def _gdn_constants(d):
    n = GDN_BLOCK
    i = np.arange(n)
    same = (i[:, None] // A_CHUNK) == (i[None, :] // A_CHUNK)
    tri = (i[None, :] >= i[:, None]) if d == 1 else (i[None, :] <= i[:, None])
    lm = (same & tri).astype(np.float32)
    e = (np.arange(A_W)[None, :] // A_DK == np.arange(A_HEADS)[:, None]).astype(np.float32)
    return jnp.asarray(lm), jnp.asarray(lm.T.copy()), jnp.asarray(same.astype(np.float32)), jnp.asarray(e)


def _gdn(q, k, v, ba, bat, a_log, dt_bias, *, d, n_ctx, inv_passes):
    bsz, n, _ = q.shape
    nb, nc = n // GDN_BLOCK, n_ctx // GDN_BLOCK
    blk = (lambda s: jnp.where(s < nc, nc - 1 - s, nb - 1 + nc - s)) if d == 1 else (lambda s: s)
    tok = lambda b, s: (b, blk(s), 0)
    full = lambda shape: pl.BlockSpec(shape, lambda b, s: (0,) * len(shape))
    lm, lmt, ltot, e = _gdn_constants(d)
    return pl.pallas_call(
        functools.partial(_gdn_kernel, d=d, inv_passes=inv_passes),
        grid=(bsz, nb),
        in_specs=[pl.BlockSpec((1, GDN_BLOCK, A_W), tok)] * 3
        + [pl.BlockSpec((1, GDN_BLOCK, ba.shape[2]), tok),
           pl.BlockSpec((1, bat.shape[1], GDN_BLOCK), lambda b, s: (b, 0, blk(s))),
           full((1, A_HEADS)), full((1, A_HEADS)), full((A_HEADS, 1)), full((A_HEADS, 1)),
           full(lm.shape), full(lm.shape), full(lm.shape), full(e.shape)],
        out_specs=pl.BlockSpec((1, GDN_BLOCK, A_W), tok),
        out_shape=jax.ShapeDtypeStruct((bsz, n, A_W), F32),
        scratch_shapes=[pltpu.VMEM((A_HEADS, A_DK, A_DV), F32)],
        compiler_params=_cparams("parallel", "arbitrary"),
    )(q, k, v, ba, bat, a_log[d].reshape(1, -1), dt_bias[d].reshape(1, -1),
      a_log[d].reshape(-1, 1), dt_bias[d].reshape(-1, 1), lm, lmt, ltot, e)


def _attn_kernel(q_ref, k_ref, v_ref, lq1_ref, lk1_ref, lq2_ref, lk2_ref, o_ref, *, lam_init, tk):
    q = q_ref[0]
    tq = q.shape[0]
    nk = k_ref.shape[1]
    lam = (jnp.exp(jnp.sum(lq1_ref[...] * lk1_ref[...], keepdims=True))
           - jnp.exp(jnp.sum(lq2_ref[...] * lk2_ref[...], keepdims=True)) + lam_init)
    outs = []
    for m in range(2):
        qm = q[:, m * B_DH:(m + 1) * B_DH]
        mx = jnp.full((tq, 1), -jnp.inf, F32)
        l = jnp.zeros((tq, 1), F32)
        acc = jnp.zeros((tq, 2 * B_DH), F32)
        for c in range(nk // tk):
            kc = k_ref[0, c * tk:(c + 1) * tk, m * B_DH:(m + 1) * B_DH]
            s = _dot_nt(qm, kc)
            m_new = jnp.maximum(mx, jnp.max(s, axis=-1, keepdims=True))
            alpha = jnp.exp(mx - m_new)
            p = jnp.exp(s - m_new)
            l = alpha * l + jnp.sum(p, axis=-1, keepdims=True)
            acc = alpha * acc + _dot(p.astype(BF16), v_ref[0, c * tk:(c + 1) * tk, :])
            mx = m_new
        outs.append(acc / l)
    o_ref[0] = outs[0] - lam * outs[1]


def _diff_attention(bqr, bkr, bv, lam_params, *, lam_init, n_ctx, tq, tk):
    bsz, n, _ = bqr.shape
    n_lat = n - n_ctx
    hw = 2 * B_DH
    lam_spec = pl.BlockSpec((1, B_DH), lambda b, h, i: (0, 0))
    return pl.pallas_call(
        functools.partial(_attn_kernel, lam_init=lam_init, tk=tk),
        grid=(bsz, B_HEADS, n_lat // tq),
        in_specs=[pl.BlockSpec((1, tq, hw), lambda b, h, i: (b, i + n_ctx // tq, h)),
                  pl.BlockSpec((1, n, hw), lambda b, h, i: (b, 0, h)),
                  pl.BlockSpec((1, n, hw), lambda b, h, i: (b, 0, h))] + [lam_spec] * 4,
        out_specs=pl.BlockSpec((1, tq, hw), lambda b, h, i: (b, i, h)),
        out_shape=jax.ShapeDtypeStruct((bsz, n_lat, B_W), F32),
        compiler_params=_cparams("parallel", "parallel", "arbitrary"),
    )(bqr, bkr, bv, *[p.reshape(1, B_DH) for p in lam_params])


def _outproj_kernel(of_ref, ob_ref, gate_ref, d_ref, x_ref, g1_ref, gw_ref, sw_ref, bd64_ref,
                    bd128_ref, w_ref, o_ref, *, lam_init):
    o = of_ref[0] + ob_ref[0]
    ms = _mm_lhs2(o * o, bd64_ref[...]) * (1.0 / A_DV)
    y1 = (o * lax.rsqrt(ms + EPS) * gw_ref[...]) * _silu(gate_ref[0])
    dd = d_ref[0]
    ms2 = _mm_lhs2(dd * dd, bd128_ref[...]) * (1.0 / (2 * B_DH))
    y2 = (dd * lax.rsqrt(ms2 + EPS) * sw_ref[...]) * (1.0 - lam_init)
    ycat = jnp.concatenate([y1, y2], axis=1).astype(BF16)
    o_ref[0] = x_ref[0] + g1_ref[0] * _dot(ycat, w_ref[...])


def _outproj(o_f, o_b, gate, d_lat, x, g1, gdn_norm_w, subln_w, w_out, *, lam_init, n_ctx, tm):
    bsz, n_lat, d = x.shape
    off = n_ctx // tm
    tok = lambda b, i: (b, i, 0)
    tok_off = lambda b, i: (b, i + off, 0)
    full = lambda shape: pl.BlockSpec(shape, lambda b, i: (0,) * len(shape))
    return pl.pallas_call(
        functools.partial(_outproj_kernel, lam_init=lam_init),
        grid=(bsz, n_lat // tm),
        in_specs=[pl.BlockSpec((1, tm, A_W), tok_off)] * 3
        + [pl.BlockSpec((1, tm, B_W), tok), pl.BlockSpec((1, tm, d), tok),
           pl.BlockSpec((1, 1, d), lambda b, i: (b, 0, 0)),
           full((1, A_W)), full((1, B_W)), full((A_W, A_W)), full((B_W, B_W)), full(w_out.shape)],
        out_specs=pl.BlockSpec((1, tm, d), tok),
        out_shape=jax.ShapeDtypeStruct((bsz, n_lat, d), F32),
        compiler_params=_cparams("parallel", "arbitrary"),
    )(o_f, o_b, gate, d_lat, x, g1, jnp.tile(gdn_norm_w, A_HEADS).reshape(1, A_W),
      jnp.tile(subln_w, B_HEADS).reshape(1, B_W), _block_ones(A_W, A_DV), _block_ones(B_W, 2 * B_DH),
      w_out.astype(BF16))


def _dft1_kernel(x_ref, nw_ref, sc_ref, sh_ref, f_ref, twr_ref, twi_ref, re_ref, im_ref, *, d, passes):
    n1 = f_ref.shape[1]
    f = f_ref[...]
    for t in range(x_ref.shape[2] // d):
        cs = slice(t * d, (t + 1) * d)
        h = _normmod(x_ref[0, :, cs], nw_ref[...], sc_ref[0], sh_ref[0])
        a = _mm(f, h, passes)
        a_re, a_im = a[:n1], a[n1:]
        twr = jnp.concatenate([twr_ref[:, t * 128:(t + 1) * 128]] * (d // 128), axis=1)
        twi = jnp.concatenate([twi_ref[:, t * 128:(t + 1) * 128]] * (d // 128), axis=1)
        re_ref[0, :, cs] = a_re * twr - a_im * twi
        im_ref[0, :, cs] = a_re * twi + a_im * twr


def _dft2_kernel(re_ref, im_ref, x_ref, g1_ref, f_ref, cc_ref, sc_ref, w_ref, o_ref, *, d, passes):
    n2 = f_ref.shape[1]
    f = f_ref[...]
    dg = d // C_GROUPS
    for t in range(re_ref.shape[1]):
        pr = _mm(f, re_ref[0, t], passes)
        pi = _mm(f, im_ref[0, t], passes)
        x_re = pr[:n2] + pi[n2:]
        x_im = pi[:n2] - pr[n2:]
        y = jnp.concatenate(
            [_mm(x_re[:, g * dg:(g + 1) * dg], cc_ref[...], passes)
             + _mm(x_im[:, g * dg:(g + 1) * dg], sc_ref[...], passes) for g in range(C_GROUPS)], axis=1)
        cs = slice(t * d, (t + 1) * d)
        o_ref[0, :, cs] = x_ref[0, :, cs] + g1_ref[0] * _dot(y.astype(BF16), w_ref[...])


def _fourier_layer(x, nw, sc, sh, g1, w_out, *, passes=3, nb=2, kb=8):
    bsz, n, d = x.shape
    n1, n2 = DFT_N1, n // DFT_N1
    dg = d // C_GROUPS
    k1 = np.arange(n1)
    ang1 = 2 * np.pi * np.outer(k1, k1) / n1
    f1 = jnp.asarray(np.concatenate([np.cos(ang1), -np.sin(ang1)], 0), F32)
    angt = 2 * np.pi * np.outer(k1, np.arange(n2)) / n
    twr = jnp.asarray(np.repeat(np.cos(angt), 128, axis=1), F32)
    twi = jnp.asarray(np.repeat(-np.sin(angt), 128, axis=1), F32)
    k2 = np.arange(n2)
    ang2 = 2 * np.pi * np.outer(k2, k2) / n2
    f2 = jnp.asarray(np.concatenate([np.cos(ang2), np.sin(ang2)], 0), F32)
    angc = 2 * np.pi * np.outer(np.arange(dg), np.arange(dg)) / dg
    scale = 1.0 / math.sqrt(n * dg)
    cc = jnp.asarray(np.cos(angc) * scale, F32)
    sn = jnp.asarray(np.sin(angc) * scale, F32)

    xv = x.reshape(bsz, n1, n2 * d)
    col = lambda b, j: (b, 0, j)
    vec = pl.BlockSpec((1, 1, d), lambda b, j: (b, 0, 0))
    full = lambda shape: pl.BlockSpec(shape, lambda b, j: (0,) * len(shape))
    a_re, a_im = pl.pallas_call(
        functools.partial(_dft1_kernel, d=d, passes=passes),
        grid=(bsz, n2 // nb),
        in_specs=[pl.BlockSpec((1, n1, nb * d), col), full((1, d)), vec, vec, full(f1.shape),
                  pl.BlockSpec((n1, nb * 128), lambda b, j: (0, j)),
                  pl.BlockSpec((n1, nb * 128), lambda b, j: (0, j))],
        out_specs=[pl.BlockSpec((1, n1, nb * d), col)] * 2,
        out_shape=[jax.ShapeDtypeStruct((bsz, n1, n2 * d), F32)] * 2,
        compiler_params=_cparams("parallel", "arbitrary"),
    )(xv, nw.reshape(1, d), sc, sh, f1, twr, twi)

    xo = x.reshape(bsz, n2, n1 * d)
    out = pl.pallas_call(
        functools.partial(_dft2_kernel, d=d, passes=passes),
        grid=(bsz, n1 // kb),
        in_specs=[pl.BlockSpec((1, kb, n2, d), lambda b, j: (b, j, 0, 0))] * 2
        + [pl.BlockSpec((1, n2, kb * d), col), vec, full(f2.shape), full(cc.shape), full(sn.shape),
           full(w_out.shape)],
        out_specs=pl.BlockSpec((1, n2, kb * d), col),
        out_shape=jax.ShapeDtypeStruct((bsz, n2, n1 * d), F32),
        compiler_params=_cparams("parallel", "arbitrary"),
    )(a_re.reshape(bsz, n1, n2, d), a_im.reshape(bsz, n1, n2, d), xo, g1, f2, cc, sn,
      w_out.astype(BF16))
    return out.reshape(bsz, n, d)


def _top16(s, payload=None):
    rows = lax.broadcasted_iota(I32, s.shape, 0)
    big = s.shape[0]
    vals, idxs = [], []
    for _ in range(PEER_TOPK):
        m = jnp.max(s, axis=0, keepdims=True)
        am = jnp.min(jnp.where(s == m, rows, big), axis=0, keepdims=True)
        sel = rows == am
        vals.append(m)
        if payload is None:
            idxs.append(am)
        else:
            idxs.append(jnp.max(jnp.where(sel, payload, -1), axis=0, keepdims=True))
        s = jnp.where(sel, -jnp.inf, s)
    return jnp.concatenate(vals, axis=0), jnp.concatenate(idxs, axis=0)


def _staircase_candidates(s1, i1, s2, i2):
    k = PEER_TOPK
    sub = lax.broadcasted_iota(I32, (SUB,) + s1.shape[1:], 0)
    cs = [s1[0:1] + s2, s1[1:2] + s2[:SUB]]
    ci = [i1[0:1] * PEER_NKEYS + i2, i1[1:2] * PEER_NKEYS + i2[:SUB]]
    for a in range(2, SUB):
        keep = sub < k // (a + 1)
        cs.append(jnp.where(keep, s1[a:a + 1] + s2[:SUB], -jnp.inf))
        ci.append(jnp.where(keep, i1[a:a + 1] * PEER_NKEYS + i2[:SUB], -1))
    cs.append(s1[SUB:] + s2[0:1])
    ci.append(i1[SUB:] * PEER_NKEYS + i2[0:1])
    return jnp.concatenate(cs, axis=0), jnp.concatenate(ci, axis=0)


def _peer_topk_kernel(x_ref, nw_ref, sc_ref, sh_ref, wh_ref, wl_ref, keys_ref,
                      h_out, row_out, par_out, gate_out):
    h = _normmod(x_ref[0], nw_ref[...], sc_ref[0], sh_ref[0])
    h_out[0] = h
    hh, hl = _split(h)
    q = _dot(hh, wh_ref[...]) + (_dot(hh, wl_ref[...]) + _dot(hl, wh_ref[...]))
    half = PEER_DKEY // 2
    idx_rows, gate_rows = [], []
    for hd in range(PEER_HEADS):
        tops = []
        for p in range(2):
            j = hd * 2 + p
            st = _mm(keys_ref[j], q[:, j * half:(j + 1) * half], 3, _dot_nt)
            tops.append(_top16(st))
        (s1, i1), (s2, i2) = tops
        top_s, top_i = _top16(*_staircase_candidates(s1, i1, s2, i2))
        ex = jnp.exp(top_s - top_s[0:1])
        gate_rows.append(ex / jnp.sum(ex, axis=0, keepdims=True))
        idx_rows.append(top_i)
    idx = jnp.concatenate(idx_rows, axis=0).T
    row_out[0] = lax.shift_right_logical(idx, 1) * SUB
    par_out[0] = idx & 1
    gate_out[0] = jnp.concatenate(gate_rows, axis=0).T


def _peer_topk(x, nw, sc, sh, w_q, keys, *, tb):
    bsz, n, d = x.shape
    nk = PEER_HEADS * PEER_TOPK
    wh = w_q.astype(BF16)
    wl = (w_q - wh.astype(F32)).astype(BF16)
    keys2 = keys.reshape(PEER_HEADS * 2, PEER_NKEYS, PEER_DKEY // 2)
    tok = lambda b, i: (b, i, 0)
    vec = pl.BlockSpec((1, 1, d), lambda b, i: (b, 0, 0))
    full = lambda shape: pl.BlockSpec(shape, lambda b, i: (0,) * len(shape))
    return pl.pallas_call(
        _peer_topk_kernel,
        grid=(bsz, n // tb),
        in_specs=[pl.BlockSpec((1, tb, d), tok), full((1, d)), vec, vec, full(wh.shape), full(wl.shape),
                  full(keys2.shape)],
        out_specs=[pl.BlockSpec((1, tb, d), tok)] + [pl.BlockSpec((1, tb, nk), tok)] * 3,
        out_shape=[jax.ShapeDtypeStruct((bsz, n, d), F32), jax.ShapeDtypeStruct((bsz, n, nk), I32),
                   jax.ShapeDtypeStruct((bsz, n, nk), I32), jax.ShapeDtypeStruct((bsz, n, nk), F32)],
        compiler_params=_cparams("parallel", "arbitrary"),
    )(x, nw.reshape(1, d), sc, sh, wh, wl, keys2)


def _pack_table(tab):
    e, d = tab.shape
    bits = lax.bitcast_convert_type(tab.astype(BF16), jnp.uint16).astype(U32)
    packed = (bits[0::2] << 16) | bits[1::2]
    return packed.reshape(e // 2 * (d // 128), 128)


def _gather_tiles(tab_ref, row_ref, t, nk):
    tiles = [tab_ref[pl.ds(pl.multiple_of(row_ref[t, kk], SUB), SUB), :] for kk in range(nk)]
    return pltpu.bitcast(jnp.concatenate(tiles, axis=0), BF16)


def _gather_constants(nk):
    c = np.arange(nk * PAIR_COLS)
    pair = (c[None, :] // PAIR_COLS == np.arange(nk)[:, None])
    hi = (c % 2 == 1)[None, :]
    smask = ((c[None, :] % PAIR_COLS) // 2 == np.arange(SUB)[:, None]).astype(np.float32)
    as_bf16 = lambda m: jnp.asarray(m.astype(np.float32), dtype=BF16)
    return dict(e_all=as_bf16(pair), e_hi=as_bf16(pair & hi), e_lo=as_bf16(pair & ~hi),
                collapse=as_bf16(pair.T), smask=jnp.asarray(smask), hi_row=jnp.asarray(hi.astype(np.float32)))


def _gelu_tanh(x):
    return 0.5 * x * (1.0 + jnp.tanh(math.sqrt(2.0 / math.pi) * (x + 0.044715 * (x * x * x))))


def _peer_act_kernel(row_ref, par_ref, h_ref, gate_ref, tab_ref, eall_ref, coll_ref, smask_ref, hi_ref,
                     w_out, rsum_ref):
    tb, nk = gate_ref.shape
    smask = smask_ref[...]

    def token(t, carry):
        g = _gather_tiles(tab_ref, row_ref, t, nk)
        hh, hl = _split(h_ref[t])
        r = _dot_nt(jnp.concatenate([hh, hl], axis=0), g)
        rsum_ref[pl.ds(t, 1), :] = jnp.sum((r[:SUB] + r[SUB:]) * smask, axis=0, keepdims=True)
        return carry

    lax.fori_loop(0, tb, token, 0, unroll=GATHER_UNROLL)
    want_hi = 1.0 - _dot(par_ref[...].astype(F32).astype(BF16), eall_ref[...])
    picked = jnp.where(want_hi == hi_ref[...], rsum_ref[...], 0.0)
    w_out[...] = gate_ref[...] * _gelu_tanh(_mm_lhs2(picked, coll_ref[...]))


def _peer_mix_kernel(row_ref, par_ref, w_ref, x_ref, g2_ref, tab_ref, ehi_ref, elo_ref, smask_ref,
                     o_ref, wcols_ref):
    tb, nk = w_ref.shape
    g2 = g2_ref[0]
    smask = smask_ref[...]
    par = par_ref[...].astype(F32)
    w = w_ref[...]
    wcols_ref[...] = _mm_lhs2(w * (1.0 - par), ehi_ref[...]) + _mm_lhs2(w * par, elo_ref[...])

    def token(t, carry):
        g = _gather_tiles(tab_ref, row_ref, t, nk)
        wh, wl = _split(wcols_ref[pl.ds(t, 1), :] * smask)
        res = _dot(jnp.concatenate([wh, wl], axis=0), g)
        o_ref[t] = x_ref[t] + g2 * (res[:SUB] + res[SUB:])
        return carry

    lax.fori_loop(0, tb, token, 0, unroll=GATHER_UNROLL)


def _peer_gather(x, h, row, par, gate, g2, u_tab, v_tab, *, tb):
    bsz, n, d = x.shape
    t_all = bsz * n
    nk = row.shape[-1]
    sl = d // 128
    assert sl == SUB
    row2, par2 = row.reshape(t_all, nk), par.reshape(t_all, nk)
    cst = _gather_constants(nk)
    ncol = nk * PAIR_COLS
    tab_spec = pl.BlockSpec((u_tab.shape[0] // 2 * sl, 128), lambda i: (0, 0), pipeline_mode=pl.Buffered(1))
    smem = pl.BlockSpec((tb, nk), lambda i: (i, 0), memory_space=pltpu.SMEM)
    tokk = pl.BlockSpec((tb, nk), lambda i: (i, 0))
    tok3 = pl.BlockSpec((tb, sl, 128), lambda i: (i, 0, 0))
    full = lambda a: pl.BlockSpec(a.shape, lambda i: (0,) * a.ndim)
    w = pl.pallas_call(
        _peer_act_kernel,
        grid=(t_all // tb,),
        in_specs=[smem, tokk, tok3, tokk, tab_spec, full(cst["e_all"]), full(cst["collapse"]),
                  full(cst["smask"]), full(cst["hi_row"])],
        out_specs=tokk,
        out_shape=jax.ShapeDtypeStruct((t_all, nk), F32),
        scratch_shapes=[pltpu.VMEM((tb, ncol), F32)],
        compiler_params=_cparams("arbitrary"),
    )(row2, par2, h.reshape(t_all, sl, 128), gate.reshape(t_all, nk), _pack_table(u_tab),
      cst["e_all"], cst["collapse"], cst["smask"], cst["hi_row"])
    out = pl.pallas_call(
        _peer_mix_kernel,
        grid=(t_all // tb,),
        in_specs=[smem, tokk, tokk, tok3, pl.BlockSpec((1, sl, 128), lambda i: ((i * tb) // n, 0, 0)),
                  tab_spec, full(cst["e_hi"]), full(cst["e_lo"]), full(cst["smask"])],
        out_specs=tok3,
        out_shape=jax.ShapeDtypeStruct((t_all, sl, 128), F32),
        scratch_shapes=[pltpu.VMEM((tb, ncol), F32)],
        compiler_params=_cparams("arbitrary"),
    )(row2, par2, w, x.reshape(t_all, sl, 128), g2.reshape(bsz, sl, 128), _pack_table(v_tab),
      cst["e_hi"], cst["e_lo"], cst["smask"])
    return out.reshape(bsz, n, d)


def _peer_layer(x, nw, sc, sh, g2, w_q, keys, u_tab, v_tab):
    h, row, par, gate = _peer_topk(x, nw, sc, sh, w_q, keys, tb=128)
    return _peer_gather(x, h, row, par, gate, g2, u_tab, v_tab, tb=128)


def _final_norm_kernel(x_ref, w_ref, o_ref):
    x = x_ref[0]
    o_ref[0] = x * lax.rsqrt(jnp.mean(x * x, axis=-1, keepdims=True) + EPS) * w_ref[...]


def _final_norm(x, w, *, tm):
    bsz, n, d = x.shape
    return pl.pallas_call(
        _final_norm_kernel,
        grid=(bsz, n // tm),
        in_specs=[pl.BlockSpec((1, tm, d), lambda b, i: (b, i, 0)), pl.BlockSpec((1, d), lambda b, i: (0, 0))],
        out_specs=pl.BlockSpec((1, tm, d), lambda b, i: (b, i, 0)),
        out_shape=jax.ShapeDtypeStruct((bsz, n, d), F32),
        compiler_params=_cparams("parallel", "arbitrary"),
    )(x, w.reshape(1, d))


def _rope_tables(n_ctx, n_lat):
    pos = np.arange(n_lat)
    axis_dim = B_DH // 2
    inv = ROPE_BASE ** (-np.arange(0, axis_dim, 2, dtype=np.float32) / axis_dim)
    ang = np.concatenate([(pos // GRID_W)[:, None] * inv, (pos % GRID_W)[:, None] * inv], axis=-1)
    ang = np.concatenate([np.zeros((n_ctx, ang.shape[1])), ang], axis=0).astype(np.float32)
    cos = np.cos(ang)
    sin = np.sin(ang)
    cos64 = np.concatenate([cos, cos], axis=1)
    sin64 = np.concatenate([-sin, sin], axis=1)
    return (jnp.asarray(np.tile(cos64, (1, 2)), F32), jnp.asarray(np.tile(sin64, (1, 2)), F32))


def _mixer_ab_layer(x, ctx, mod, mod_c, layer, norm1_w, w_in, conv_w, a_log, dt_bias, gdn_norm_w,
                    lam_params, subln_w, w_out):
    bsz, n_lat, d = x.shape
    n_ctx = ctx.shape[1]
    n = n_ctx + n_lat
    tm = GDN_BLOCK
    lam_init = 0.8 - 0.6 * math.exp(-0.3 * layer)

    xc = jnp.concatenate([ctx, x], axis=1)
    seg = lambda j: jnp.stack([jnp.broadcast_to(mod_c[j], (bsz, d)), mod[:, j]], axis=1)[:, :, None, :]
    cuts = np.cumsum([0, 3 * A_W, A_W, 4 * A_HEADS, B_W, B_W, B_W])
    weights = [w_in[:, a:b] for a, b in zip(cuts[:-1], cuts[1:])]
    qkv, gate, ba, bq, bk, bv = _nm_matmul(
        xc, norm1_w, seg(1), seg(0), weights, [F32, F32, F32, F32, F32, BF16],
        tm=tm, n_first=n_ctx // tm)

    last = qkv[:, tm - 1::tm]
    first = qkv[:, 0::tm]
    zero = jnp.zeros_like(last[:, :1])
    prev_rows = jnp.concatenate([zero, last[:, :-1]], axis=1)
    next_rows = jnp.concatenate([first[:, 1:], zero], axis=1)
    blk = jnp.arange(n // tm)[None, :, None]
    prev_rows = jnp.where(blk == n_ctx // tm, 0.0, prev_rows)[:, :, None, :]
    next_rows = jnp.where(blk == n_ctx // tm - 1, 0.0, next_rows)[:, :, None, :]
    cos128, sin128 = _rope_tables(n_ctx, n_lat)
    q, k, v, bqr, bkr = _prep(qkv, prev_rows, next_rows, conv_w, bq, bk, cos128, sin128, tm=tm)

    bat = jnp.swapaxes(ba, 1, 2)
    o_f = _gdn(q, k, v, ba, bat, a_log, dt_bias, d=0, n_ctx=n_ctx, inv_passes=1)
    o_b = _gdn(q, k, v, ba, bat, a_log, dt_bias, d=1, n_ctx=n_ctx, inv_passes=1)
    d_lat = _diff_attention(bqr, bkr, bv, lam_params, lam_init=lam_init, n_ctx=n_ctx, tq=256,
                            tk=n // 3 if n % (3 * 128) == 0 else n)
    return _outproj(o_f, o_b, gate, d_lat, x, mod[:, 2][:, None, :], gdn_norm_w, subln_w, w_out,
                    lam_init=lam_init, n_ctx=n_ctx, tm=tm)


def kernel(x, c, ctx, c_ctx, ada_w, ada_b, norm1_w, norm2_w, w_in, conv_w, a_log, dt_bias, gdn_norm_w,
           lam_q1, lam_k1, lam_q2, lam_k2, subln_w, w_out_ab, w_out_f, peer_wq, peer_keys, peer_u,
           peer_v, final_norm_w):
    bsz, n_lat, d = x.shape
    depth = ada_w.shape[0]
    last_ctx_reader = 2 * ((depth - 1) // 2)
    cmat = jnp.concatenate([c, c_ctx[None, :], jnp.zeros((8 - bsz - 1, d), F32)], axis=0)
    for i in range(depth):
        if i < last_ctx_reader:
            raise NotImplementedError("context stream advance (depth > 2) is not implemented")
        mod_all = _modulation(cmat, ada_w[i], ada_b[i])
        mod = mod_all[:bsz].reshape(bsz, 6, d)
        mod_c = mod_all[bsz].reshape(6, d)
        vec = lambda j: mod[:, j][:, None, :]
        j = i // 2
        if i % 2 == 0:
            x = _mixer_ab_layer(x, ctx, mod, mod_c, i, norm1_w[i], w_in[j], conv_w[j], a_log[j], dt_bias[j],
                                gdn_norm_w[j], (lam_q1[j], lam_k1[j], lam_q2[j], lam_k2[j]), subln_w[j],
                                w_out_ab[j])
        else:
            x = _fourier_layer(x, norm1_w[i], vec(1), vec(0), vec(2), w_out_f[j])
        x = _peer_layer(x, norm2_w[i], vec(4), vec(3), vec(5), peer_wq[i], peer_keys[i], peer_u[i],
                        peer_v[i])
    return _final_norm(x, final_norm_w, tm=512)
```

```python
import functools
import math

import numpy as np
import jax
import jax.numpy as jnp
from jax import lax
from jax.experimental import pallas as pl
from jax.experimental.pallas import tpu as pltpu

F32 = jnp.float32
BF16 = jnp.bfloat16
I32 = jnp.int32
U32 = jnp.uint32
HIGHEST = lax.Precision.HIGHEST

EPS = 1e-6
ROPE_BASE = 10000.0
GRID_W = 64
A_HEADS, A_DK, A_DV, A_CHUNK = 8, 64, 64, 64
B_HEADS, B_DH = 4, 64
C_GROUPS = 4
PEER_HEADS, PEER_NKEYS, PEER_DKEY, PEER_TOPK = 8, 128, 256, 16
A_W = A_HEADS * A_DK
B_W = B_HEADS * 2 * B_DH
GDN_BLOCK = 256
GDN_HEAD_GROUP = 4
ATTN_SLAB = 64
DFT_N1 = 128
SUB = 8
PAIR_COLS = 2 * SUB
GATHER_UNROLL = 8

VMEM_LIMIT_BYTES = 48 * 1024 * 1024


def _cparams(*sem):
    return pltpu.CompilerParams(dimension_semantics=sem, vmem_limit_bytes=VMEM_LIMIT_BYTES)


def _dot(a, b):
    return lax.dot_general(a, b, (((1,), (0,)), ((), ())), preferred_element_type=F32)


def _dot_nt(a, b):
    return lax.dot_general(a, b, (((1,), (1,)), ((), ())), preferred_element_type=F32)


def _dot_tn(a, b):
    return lax.dot_general(a, b, (((0,), (0,)), ((), ())), preferred_element_type=F32)


def _dot_f32(a, b):
    return lax.dot_general(a, b, (((1,), (0,)), ((), ())), precision=HIGHEST,
                           preferred_element_type=F32)


def _split(a):
    hi = a.astype(BF16)
    lo = (a - hi.astype(F32)).astype(BF16)
    return hi, lo


def _mm(a, b, passes=1, dot=_dot):
    if passes == 1:
        return dot(a.astype(BF16), b.astype(BF16))
    ah, al = _split(a)
    bh, bl = _split(b)
    return dot(ah, bh) + (dot(ah, bl) + dot(al, bh))


def _mm_lhs2(a, b_bf16):
    ah, al = _split(a)
    return _dot(ah, b_bf16) + _dot(al, b_bf16)


def _silu(x):
    return x * jax.nn.sigmoid(x)


def _softplus(x):
    return jnp.maximum(x, 0.0) + jnp.log(1.0 + jnp.exp(-jnp.abs(x)))


def _normmod(x, nw, sc, sh):
    ms = jnp.mean(x * x, axis=-1, keepdims=True)
    return (x * lax.rsqrt(ms + EPS) * nw) * (1.0 + sc) + sh


def _split_outside(w):
    hi = lax.bitcast_convert_type(lax.bitcast_convert_type(w, U32) & jnp.uint32(0xFFFF0000), F32)
    return hi.astype(BF16), (w - hi).astype(BF16)


def _block_ones(n, blk):
    i = np.arange(n) // blk
    return jnp.asarray((i[:, None] == i[None, :]).astype(np.float32), dtype=BF16)


def _mod_kernel(c_ref, w_ref, b_ref, o_ref):
    o_ref[...] = _dot_f32(_silu(c_ref[...]), w_ref[...]) + b_ref[...]


def _modulation(cmat, w, b):
    rows, d = cmat.shape
    n = w.shape[1]
    return pl.pallas_call(
        _mod_kernel,
        grid=(n // d,),
        in_specs=[pl.BlockSpec((rows, d), lambda j: (0, 0)),
                  pl.BlockSpec((d, d), lambda j: (0, j)),
                  pl.BlockSpec((1, d), lambda j: (0, j))],
        out_specs=pl.BlockSpec((rows, d), lambda j: (0, j)),
        out_shape=jax.ShapeDtypeStruct((rows, n), F32),
        compiler_params=_cparams("arbitrary"),
    )(cmat, w, b.reshape(1, n))


def _nm_matmul_kernel(*refs, widths, passes):
    x_ref, nw_ref, sc_ref, sh_ref = refs[:4]
    n_w = len(widths)
    w_refs = refs[4:4 + n_w * (2 if passes == 3 else 1)]
    outs = refs[4 + len(w_refs):]
    h = _normmod(x_ref[0], nw_ref[...], sc_ref[0, 0], sh_ref[0, 0])
    if passes == 1:
        hb = h.astype(BF16)
        for j in range(n_w):
            outs[j][0] = _dot(hb, w_refs[j][...]).astype(outs[j].dtype)
    else:
        hh, hl = _split(h)
        for j in range(n_w):
            wh, wl = w_refs[2 * j][...], w_refs[2 * j + 1][...]
            outs[j][0] = (_dot(hh, wh) + (_dot(hh, wl) + _dot(hl, wh))).astype(outs[j].dtype)
    if len(outs) > n_w:
        outs[n_w][0] = h


def _nm_matmul(x, nw, sc, sh, weights, out_dtypes, *, tm, n_first=0, passes=1, emit_h=False):
    bsz, n, d = x.shape
    seg = sc.shape[1]
    widths = tuple(w.shape[1] for w in weights)
    w_in = []
    for w in weights:
        if passes == 3:
            w_in += list(_split_outside(w))
        else:
            w_in.append(w.astype(BF16))

    def mod_map(b, i):
        return (b, jnp.where(i < n_first, 0, seg - 1), 0, 0)

    in_specs = [pl.BlockSpec((1, tm, d), lambda b, i: (b, i, 0)),
                pl.BlockSpec((1, d), lambda b, i: (0, 0)),
                pl.BlockSpec((1, 1, 1, d), mod_map),
                pl.BlockSpec((1, 1, 1, d), mod_map)]
    in_specs += [pl.BlockSpec(w.shape, lambda b, i: (0, 0)) for w in w_in]
    out_specs = [pl.BlockSpec((1, tm, wd), lambda b, i: (b, i, 0)) for wd in widths]
    out_shape = [jax.ShapeDtypeStruct((bsz, n, wd), dt) for wd, dt in zip(widths, out_dtypes)]
    if emit_h:
        out_specs.append(pl.BlockSpec((1, tm, d), lambda b, i: (b, i, 0)))
        out_shape.append(jax.ShapeDtypeStruct((bsz, n, d), F32))
    return pl.pallas_call(
        functools.partial(_nm_matmul_kernel, widths=widths, passes=passes),
        grid=(bsz, n // tm),
        in_specs=in_specs, out_specs=out_specs, out_shape=out_shape,
        compiler_params=_cparams("parallel", "arbitrary"),
    )(x, nw.reshape(1, d), sc, sh, *w_in)


def _prep_kernel(qkv_ref, prev_ref, next_ref, cw_ref, bd_ref, bq_ref, bk_ref, cos_ref, sin_ref,
                 q_out, k_out, v_out, bqr_out, bkr_out):
    x = qkv_ref[0]
    tm = x.shape[0]
    row = lax.broadcasted_iota(I32, x.shape, 0)
    xm = jnp.where(row == 0, prev_ref[0, 0], pltpu.roll(x, 1, 0))
    xp = jnp.where(row == tm - 1, next_ref[0, 0], pltpu.roll(x, tm - 1, 0))
    cw = cw_ref[...]
    y = _silu(xm * cw[0:1] + x * cw[1:2] + xp * cw[2:3])
    bd = bd_ref[...]

    def l2(t):
        return t * lax.rsqrt(_mm_lhs2(t * t, bd) + EPS)

    q_out[0] = l2(y[:, :A_W]) * (A_DK ** -0.5)
    k_out[0] = l2(y[:, A_W:2 * A_W])
    v_out[0] = y[:, 2 * A_W:]

    cos = jnp.concatenate([cos_ref[...]] * (B_W // 128), axis=1)
    sin = jnp.concatenate([sin_ref[...]] * (B_W // 128), axis=1)
    lane = lax.broadcasted_iota(I32, (tm, B_W), 1)
    first = (lane & (B_DH - 1)) < (B_DH // 2)

    def rope(t):
        rot = jnp.where(first, pltpu.roll(t, B_W - B_DH // 2, 1), pltpu.roll(t, B_DH // 2, 1))
        return t * cos + rot * sin

    bqr_out[0] = (rope(bq_ref[0]) * (B_DH ** -0.5)).astype(BF16)
    bkr_out[0] = rope(bk_ref[0]).astype(BF16)


def _prep(qkv, prev_rows, next_rows, conv_w, bq, bk, cos128, sin128, *, tm):
    bsz, n, cw = qkv.shape
    tok = lambda b, i: (b, i, 0)
    halo = pl.BlockSpec((1, 1, 1, cw), lambda b, i: (b, i, 0, 0))
    return pl.pallas_call(
        _prep_kernel,
        grid=(bsz, n // tm),
        in_specs=[pl.BlockSpec((1, tm, cw), tok), halo, halo,
                  pl.BlockSpec(conv_w.shape, lambda b, i: (0, 0)),
                  pl.BlockSpec((A_W, A_W), lambda b, i: (0, 0)),
                  pl.BlockSpec((1, tm, B_W), tok), pl.BlockSpec((1, tm, B_W), tok),
                  pl.BlockSpec((tm, 128), lambda b, i: (i, 0)),
                  pl.BlockSpec((tm, 128), lambda b, i: (i, 0))],
        out_specs=[pl.BlockSpec((1, tm, A_W), tok)] * 3 + [pl.BlockSpec((1, tm, B_W), tok)] * 2,
        out_shape=[jax.ShapeDtypeStruct((bsz, n, A_W), F32)] * 3
        + [jax.ShapeDtypeStruct((bsz, n, B_W), BF16)] * 2,
        compiler_params=_cparams("parallel", "arbitrary"),
    )(qkv, prev_rows, next_rows, conv_w, _block_ones(A_W, A_DK), bq, bk, cos128, sin128)


def _gdn_kernel(q_ref, k_ref, v_ref, ba_ref, bat_ref, alr_ref, dtr_ref, alc_ref, dtc_ref,
                lm_ref, lmt_ref, ltot_ref, e_ref, o_ref, s_ref, *, d, inv_passes):
    rev = d == 1
    nh = A_HEADS

    @pl.when(pl.program_id(1) == 0)
    def _():
        s_ref[...] = jnp.zeros_like(s_ref)

    q, k, v = q_ref[0], k_ref[0], v_ref[0]
    ba, bat = ba_ref[0], bat_ref[0]
    n = q.shape[0]
    beta_c = jax.nn.sigmoid(ba[:, d * nh:(d + 1) * nh])
    g_c = -jnp.exp(alr_ref[...]) * _softplus(ba[:, 2 * nh + d * nh:2 * nh + (d + 1) * nh] + dtr_ref[...])
    g_t = -jnp.exp(alc_ref[...]) * _softplus(bat[2 * nh + d * nh:2 * nh + (d + 1) * nh, :] + dtc_ref[...])
    lm, ltot, e = lm_ref[...], ltot_ref[...], e_ref[...]
    gcum_c = _dot_f32(lm, g_c)
    gcum_t = _dot_f32(g_t, lmt_ref[...])
    gcum_e = _dot_f32(gcum_c, e)
    gtot_e = _dot_f32(_dot_f32(ltot, g_c), e)
    beta_e = _dot_f32(beta_c, e)
    eg = jnp.exp(gcum_e)
    kb = k * beta_e
    vb = v * beta_e
    kbe = kb * eg
    qd = q * eg
    ktail = k * jnp.exp(gtot_e - gcum_e)
    egt = jnp.exp(gtot_e)

    ri = lax.broadcasted_iota(I32, (n, n), 0)
    ci = lax.broadcasted_iota(I32, (n, n), 1)
    eye = (ri == ci).astype(F32)
    same = lambda s: (ri >> s) == (ci >> s)
    d16 = same(4).astype(F32)
    o32 = (same(5) & ~same(4)).astype(F32)
    o64 = (same(6) & ~same(5)).astype(F32)
    incl = lm > 0.0
    strict = lm * (1.0 - eye)

    order = range(n // A_CHUNK - 1, -1, -1) if rev else range(n // A_CHUNK)
    sls = [slice(h * A_DK, (h + 1) * A_DK) for h in range(nh)]
    heads = [None] * nh
    for g0 in range(0, nh, GDN_HEAD_GROUP):
        hs = range(g0, g0 + GDN_HEAD_GROUP)
        decay = {h: jnp.exp(jnp.where(incl, gcum_c[:, h:h + 1] - gcum_t[h:h + 1, :], -1e30)) for h in hs}
        a = {h: _mm(kb[:, sls[h]], k[:, sls[h]], 1, _dot_nt) * decay[h] * strict for h in hs}
        p = {h: -(a[h] * d16) for h in hs}
        x = {h: eye + p[h] for h in hs}
        for _ in range(3):
            p = {h: _mm(p[h], p[h], inv_passes) for h in hs}
            x = {h: x[h] + _mm(x[h], p[h], inv_passes) for h in hs}
        for om in (o32, o64):
            y = {h: _mm(x[h], a[h] * om, inv_passes) for h in hs}
            x = {h: x[h] - _mm(y[h], x[h], inv_passes) for h in hs}
        uw = {h: _mm(x[h], jnp.concatenate([vb[:, sls[h]], kbe[:, sls[h]]], axis=1)) for h in hs}
        attn = {h: _mm(q[:, sls[h]], k[:, sls[h]], 1, _dot_nt) * decay[h] for h in hs}
        s = {h: s_ref[h] for h in hs}
        parts = {h: {} for h in hs}
        for c in order:
            r = slice(c * A_CHUNK, (c + 1) * A_CHUNK)
            v_new = {h: uw[h][r, :A_DV] - _mm(uw[h][r, A_DV:], s[h]) for h in hs}
            for h in hs:
                parts[h][c] = _mm(qd[r, sls[h]], s[h]) + _mm(attn[h][r, r], v_new[h])
            s = {h: s[h] * egt[c * A_CHUNK:c * A_CHUNK + 1, sls[h]]
                 + _mm(ktail[r, sls[h]], v_new[h], 1, _dot_tn) for h in hs}
        for h in hs:
            s_ref[h] = s[h]
            heads[h] = jnp.concatenate([parts[h][c] for c in range(n // A_CHUNK)], axis=0)
    o_ref[0] = jnp.concatenate(heads, axis=1)


def _gdn_constants(d):
    n = GDN_BLOCK
    i = np.arange(n)
    same = (i[:, None] // A_CHUNK) == (i[None, :] // A_CHUNK)
    tri = (i[None, :] >= i[:, None]) if d == 1 else (i[None, :] <= i[:, None])
    lm = (same & tri).astype(np.float32)
    e = (np.arange(A_W)[None, :] // A_DK == np.arange(A_HEADS)[:, None]).astype(np.float32)
    return jnp.asarray(lm), jnp.asarray(lm.T.copy()), jnp.asarray(same.astype(np.float32)), jnp.asarray(e)


def _gdn(q, k, v, ba, bat, a_log, dt_bias, *, d, n_ctx, inv_passes):
    bsz, n, _ = q.shape
    nb, nc = n // GDN_BLOCK, n_ctx // GDN_BLOCK
    blk = (lambda s: jnp.where(s < nc, nc - 1 - s, nb - 1 + nc - s)) if d == 1 else (lambda s: s)
    tok = lambda b, s: (b, blk(s), 0)
    full = lambda shape: pl.BlockSpec(shape, lambda b, s: (0,) * len(shape))
    lm, lmt, ltot, e = _gdn_constants(d)
    return pl.pallas_call(
        functools.partial(_gdn_kernel, d=d, inv_passes=inv_passes),
        grid=(bsz, nb),
        in_specs=[pl.BlockSpec((1, GDN_BLOCK, A_W), tok)] * 3
        + [pl.BlockSpec((1, GDN_BLOCK, ba.shape[2]), tok),
           pl.BlockSpec((1, bat.shape[1], GDN_BLOCK), lambda b, s: (b, 0, blk(s))),
           full((1, A_HEADS)), full((1, A_HEADS)), full((A_HEADS, 1)), full((A_HEADS, 1)),
           full(lm.shape), full(lm.shape), full(lm.shape), full(e.shape)],
        out_specs=pl.BlockSpec((1, GDN_BLOCK, A_W), tok),
        out_shape=jax.ShapeDtypeStruct((bsz, n, A_W), F32),
        scratch_shapes=[pltpu.VMEM((A_HEADS, A_DK, A_DV), F32)],
        compiler_params=_cparams("parallel", "arbitrary"),
    )(q, k, v, ba, bat, a_log[d].reshape(1, -1), dt_bias[d].reshape(1, -1),
      a_log[d].reshape(-1, 1), dt_bias[d].reshape(-1, 1), lm, lmt, ltot, e)


def _attn_kernel(q_ref, k_ref, v_ref, lq1_ref, lk1_ref, lq2_ref, lk2_ref, o_ref, *, lam_init, tk):
    q = q_ref[0]
    tq = q.shape[0]
    nk = k_ref.shape[1]
    lam = (jnp.exp(jnp.sum(lq1_ref[...] * lk1_ref[...], keepdims=True))
           - jnp.exp(jnp.sum(lq2_ref[...] * lk2_ref[...], keepdims=True)) + lam_init)
    def over_keys(op, t):
        return op(op(t.reshape(tk // ATTN_SLAB, ATTN_SLAB, tq), axis=0), axis=0, keepdims=True)

    qs = [q[:, m * B_DH:(m + 1) * B_DH] for m in range(2)]
    mx = [jnp.full((1, tq), -jnp.inf, F32)] * 2
    l = [jnp.zeros((1, tq), F32)] * 2
    acc = [jnp.zeros((2 * B_DH, tq), F32)] * 2
    for c in range(nk // tk):
        ks = slice(c * tk, (c + 1) * tk)
        s = [_dot_nt(k_ref[0, ks, m * B_DH:(m + 1) * B_DH], qs[m]) for m in range(2)]
        m_new = [jnp.maximum(mx[m], over_keys(jnp.max, s[m])) for m in range(2)]
        alpha = [jnp.exp(mx[m] - m_new[m]) for m in range(2)]
        p = [jnp.exp(s[m] - m_new[m]) for m in range(2)]
        l = [alpha[m] * l[m] + over_keys(jnp.sum, p[m]) for m in range(2)]
        acc = [alpha[m] * acc[m] + _dot(v_ref[0, :, ks], p[m].astype(BF16)) for m in range(2)]
        mx = m_new
    o_ref[0] = (acc[0] / l[0] - lam * (acc[1] / l[1])).T


def _diff_attention(bqr, bkr, bv, lam_params, *, lam_init, n_ctx, tq, tk):
    bsz, n, _ = bqr.shape
    n_lat = n - n_ctx
    hw = 2 * B_DH
    assert n_lat % tq == 0 and n % tk == 0 and tk % ATTN_SLAB == 0
    lam_spec = pl.BlockSpec((1, B_DH), lambda b, h, i: (0, 0))
    return pl.pallas_call(
        functools.partial(_attn_kernel, lam_init=lam_init, tk=tk),
        grid=(bsz, B_HEADS, n_lat // tq),
        in_specs=[pl.BlockSpec((1, tq, hw), lambda b, h, i: (b, i, h)),
                  pl.BlockSpec((1, n, hw), lambda b, h, i: (b, 0, h)),
                  pl.BlockSpec((1, hw, n), lambda b, h, i: (b, h, 0))] + [lam_spec] * 4,
        out_specs=pl.BlockSpec((1, tq, hw), lambda b, h, i: (b, i, h)),
        out_shape=jax.ShapeDtypeStruct((bsz, n_lat, B_W), F32),
        compiler_params=_cparams("parallel", "parallel", "arbitrary"),
    )(bqr[:, n_ctx:], bkr, bv, *[p.reshape(1, B_DH) for p in lam_params])


def _outproj_kernel(of_ref, ob_ref, gate_ref, d_ref, x_ref, g1_ref, gw_ref, sw_ref, bd64_ref,
                    bd128_ref, w_ref, o_ref, *, lam_init):
    o = of_ref[0] + ob_ref[0]
    ms = _mm_lhs2(o * o, bd64_ref[...]) * (1.0 / A_DV)
    y1 = (o * lax.rsqrt(ms + EPS) * gw_ref[...]) * _silu(gate_ref[0])
    dd = d_ref[0]
    ms2 = _mm_lhs2(dd * dd, bd128_ref[...]) * (1.0 / (2 * B_DH))
    y2 = (dd * lax.rsqrt(ms2 + EPS) * sw_ref[...]) * (1.0 - lam_init)
    ycat = jnp.concatenate([y1, y2], axis=1).astype(BF16)
    o_ref[0] = x_ref[0] + g1_ref[0] * _dot(ycat, w_ref[...])


def _outproj(o_f, o_b, gate, d_lat, x, g1, gdn_norm_w, subln_w, w_out, *, lam_init, n_ctx, tm):
    bsz, n_lat, d = x.shape
    off = n_ctx // tm
    tok = lambda b, i: (b, i, 0)
    tok_off = lambda b, i: (b, i + off, 0)
    full = lambda shape: pl.BlockSpec(shape, lambda b, i: (0,) * len(shape))
    return pl.pallas_call(
        functools.partial(_outproj_kernel, lam_init=lam_init),
        grid=(bsz, n_lat // tm),
        in_specs=[pl.BlockSpec((1, tm, A_W), tok_off)] * 3
        + [pl.BlockSpec((1, tm, B_W), tok), pl.BlockSpec((1, tm, d), tok),
           pl.BlockSpec((1, 1, d), lambda b, i: (b, 0, 0)),
           full((1, A_W)), full((1, B_W)), full((A_W, A_W)), full((B_W, B_W)), full(w_out.shape)],
        out_specs=pl.BlockSpec((1, tm, d), tok),
        out_shape=jax.ShapeDtypeStruct((bsz, n_lat, d), F32),
        compiler_params=_cparams("parallel", "arbitrary"),
    )(o_f, o_b, gate, d_lat, x, g1, jnp.tile(gdn_norm_w, A_HEADS).reshape(1, A_W),
      jnp.tile(subln_w, B_HEADS).reshape(1, B_W), _block_ones(A_W, A_DV), _block_ones(B_W, 2 * B_DH),
      w_out.astype(BF16))


def _dft1_kernel(x_ref, nw_ref, sc_ref, sh_ref, f_ref, twr_ref, twi_ref, re_ref, im_ref, *, d, passes):
    n1 = f_ref.shape[1]
    f = f_ref[...]
    for t in range(x_ref.shape[2] // d):
        cs = slice(t * d, (t + 1) * d)
        h = _normmod(x_ref[0, :, cs], nw_ref[...], sc_ref[0], sh_ref[0])
        a = _mm(f, h, passes)
        a_re, a_im = a[:n1], a[n1:]
        twr = jnp.concatenate([twr_ref[:, t * 128:(t + 1) * 128]] * (d // 128), axis=1)
        twi = jnp.concatenate([twi_ref[:, t * 128:(t + 1) * 128]] * (d // 128), axis=1)
        re_ref[0, :, cs] = a_re * twr - a_im * twi
        im_ref[0, :, cs] = a_re * twi + a_im * twr


def _dft2_kernel(re_ref, im_ref, x_ref, g1_ref, f_ref, cc_ref, sc_ref, w_ref, o_ref, *, d, passes):
    n2 = f_ref.shape[1]
    f = f_ref[...]
    dg = d // C_GROUPS
    for t in range(re_ref.shape[1]):
        pr = _mm(f, re_ref[0, t], passes)
        pi = _mm(f, im_ref[0, t], passes)
        x_re = pr[:n2] + pi[n2:]
        x_im = pi[:n2] - pr[n2:]
        y = jnp.concatenate(
            [_mm(x_re[:, g * dg:(g + 1) * dg], cc_ref[...], passes)
             + _mm(x_im[:, g * dg:(g + 1) * dg], sc_ref[...], passes) for g in range(C_GROUPS)], axis=1)
        cs = slice(t * d, (t + 1) * d)
        o_ref[0, :, cs] = x_ref[0, :, cs] + g1_ref[0] * _dot(y.astype(BF16), w_ref[...])


def _fourier_layer(x, nw, sc, sh, g1, w_out, *, passes=3, nb=2, kb=8):
    bsz, n, d = x.shape
    n1, n2 = DFT_N1, n // DFT_N1
    dg = d // C_GROUPS
    k1 = np.arange(n1)
    ang1 = 2 * np.pi * np.outer(k1, k1) / n1
    f1 = jnp.asarray(np.concatenate([np.cos(ang1), -np.sin(ang1)], 0), F32)
    angt = 2 * np.pi * np.outer(k1, np.arange(n2)) / n
    twr = jnp.asarray(np.repeat(np.cos(angt), 128, axis=1), F32)
    twi = jnp.asarray(np.repeat(-np.sin(angt), 128, axis=1), F32)
    k2 = np.arange(n2)
    ang2 = 2 * np.pi * np.outer(k2, k2) / n2
    f2 = jnp.asarray(np.concatenate([np.cos(ang2), np.sin(ang2)], 0), F32)
    angc = 2 * np.pi * np.outer(np.arange(dg), np.arange(dg)) / dg
    scale = 1.0 / math.sqrt(n * dg)
    cc = jnp.asarray(np.cos(angc) * scale, F32)
    sn = jnp.asarray(np.sin(angc) * scale, F32)

    xv = x.reshape(bsz, n1, n2 * d)
    col = lambda b, j: (b, 0, j)
    vec = pl.BlockSpec((1, 1, d), lambda b, j: (b, 0, 0))
    full = lambda shape: pl.BlockSpec(shape, lambda b, j: (0,) * len(shape))
    a_re, a_im = pl.pallas_call(
        functools.partial(_dft1_kernel, d=d, passes=passes),
        grid=(bsz, n2 // nb),
        in_specs=[pl.BlockSpec((1, n1, nb * d), col), full((1, d)), vec, vec, full(f1.shape),
                  pl.BlockSpec((n1, nb * 128), lambda b, j: (0, j)),
                  pl.BlockSpec((n1, nb * 128), lambda b, j: (0, j))],
        out_specs=[pl.BlockSpec((1, n1, nb * d), col)] * 2,
        out_shape=[jax.ShapeDtypeStruct((bsz, n1, n2 * d), F32)] * 2,
        compiler_params=_cparams("parallel", "arbitrary"),
    )(xv, nw.reshape(1, d), sc, sh, f1, twr, twi)

    xo = x.reshape(bsz, n2, n1 * d)
    out = pl.pallas_call(
        functools.partial(_dft2_kernel, d=d, passes=passes),
        grid=(bsz, n1 // kb),
        in_specs=[pl.BlockSpec((1, kb, n2, d), lambda b, j: (b, j, 0, 0))] * 2
        + [pl.BlockSpec((1, n2, kb * d), col), vec, full(f2.shape), full(cc.shape), full(sn.shape),
           full(w_out.shape)],
        out_specs=pl.BlockSpec((1, n2, kb * d), col),
        out_shape=jax.ShapeDtypeStruct((bsz, n2, n1 * d), F32),
        compiler_params=_cparams("parallel", "arbitrary"),
    )(a_re.reshape(bsz, n1, n2, d), a_im.reshape(bsz, n1, n2, d), xo, g1, f2, cc, sn,
      w_out.astype(BF16))
    return out.reshape(bsz, n, d)


def _top16(s, payload=None):
    rows = lax.broadcasted_iota(I32, s.shape, 0)
    big = s.shape[0]
    vals, idxs = [], []
    for _ in range(PEER_TOPK):
        m = jnp.max(s, axis=0, keepdims=True)
        am = jnp.min(jnp.where(s == m, rows, big), axis=0, keepdims=True)
        sel = rows == am
        vals.append(m)
        if payload is None:
            idxs.append(am)
        else:
            idxs.append(jnp.max(jnp.where(sel, payload, -1), axis=0, keepdims=True))
        s = jnp.where(sel, -jnp.inf, s)
    return jnp.concatenate(vals, axis=0), jnp.concatenate(idxs, axis=0)


def _staircase_candidates(s1, i1, s2, i2):
    k = PEER_TOPK
    sub = lax.broadcasted_iota(I32, (SUB,) + s1.shape[1:], 0)
    cs = [s1[0:1] + s2, s1[1:2] + s2[:SUB]]
    ci = [i1[0:1] * PEER_NKEYS + i2, i1[1:2] * PEER_NKEYS + i2[:SUB]]
    for a in range(2, SUB):
        keep = sub < k // (a + 1)
        cs.append(jnp.where(keep, s1[a:a + 1] + s2[:SUB], -jnp.inf))
        ci.append(jnp.where(keep, i1[a:a + 1] * PEER_NKEYS + i2[:SUB], -1))
    cs.append(s1[SUB:] + s2[0:1])
    ci.append(i1[SUB:] * PEER_NKEYS + i2[0:1])
    return jnp.concatenate(cs, axis=0), jnp.concatenate(ci, axis=0)


def _peer_topk_kernel(x_ref, nw_ref, sc_ref, sh_ref, wh_ref, wl_ref, keys_ref,
                      h_out, row_out, par_out, gate_out):
    h = _normmod(x_ref[0], nw_ref[...], sc_ref[0], sh_ref[0])
    h_out[0] = h
    hh, hl = _split(h)
    q = _dot(hh, wh_ref[...]) + (_dot(hh, wl_ref[...]) + _dot(hl, wh_ref[...]))
    half = PEER_DKEY // 2
    idx_rows, gate_rows = [], []
    for hd in range(PEER_HEADS):
        tops = []
        for p in range(2):
            j = hd * 2 + p
            st = _mm(keys_ref[j], q[:, j * half:(j + 1) * half], 3, _dot_nt)
            tops.append(_top16(st))
        (s1, i1), (s2, i2) = tops
        top_s, top_i = _top16(*_staircase_candidates(s1, i1, s2, i2))
        ex = jnp.exp(top_s - top_s[0:1])
        gate_rows.append(ex / jnp.sum(ex, axis=0, keepdims=True))
        idx_rows.append(top_i)
    idx = jnp.concatenate(idx_rows, axis=0).T
    n_pairs = PEER_NKEYS * PEER_NKEYS // 2
    row_out[0] = (idx & (n_pairs - 1)) * SUB
    par_out[0] = (idx >= n_pairs).astype(I32)
    gate_out[0] = jnp.concatenate(gate_rows, axis=0).T


def _peer_topk(x, nw, sc, sh, w_q, keys, *, tb):
    bsz, n, d = x.shape
    nk = PEER_HEADS * PEER_TOPK
    wh, wl = _split_outside(w_q)
    keys2 = keys.reshape(PEER_HEADS * 2, PEER_NKEYS, PEER_DKEY // 2)
    tok = lambda b, i: (b, i, 0)
    vec = pl.BlockSpec((1, 1, d), lambda b, i: (b, 0, 0))
    full = lambda shape: pl.BlockSpec(shape, lambda b, i: (0,) * len(shape))
    return pl.pallas_call(
        _peer_topk_kernel,
        grid=(bsz, n // tb),
        in_specs=[pl.BlockSpec((1, tb, d), tok), full((1, d)), vec, vec, full(wh.shape), full(wl.shape),
                  full(keys2.shape)],
        out_specs=[pl.BlockSpec((1, tb, d), tok)] + [pl.BlockSpec((1, tb, nk), tok)] * 3,
        out_shape=[jax.ShapeDtypeStruct((bsz, n, d), F32), jax.ShapeDtypeStruct((bsz, n, nk), I32),
                   jax.ShapeDtypeStruct((bsz, n, nk), I32), jax.ShapeDtypeStruct((bsz, n, nk), F32)],
        compiler_params=_cparams("parallel", "arbitrary"),
    )(x, nw.reshape(1, d), sc, sh, wh, wl, keys2)


def _pack_kernel(hi_ref, lo_ref, o_ref):
    pairs, d = hi_ref.shape
    bits = lambda v: lax.bitcast_convert_type(v.astype(BF16).astype(F32), U32)
    packed = bits(hi_ref[...]) | (bits(lo_ref[...]) >> 16)
    for s in range(d // 128):
        o_ref[pl.ds(s, pairs, stride=SUB), :] = packed[:, s * 128:(s + 1) * 128]


def _pack_table(tab, *, pairs=256):
    e, d = tab.shape
    assert d // 128 == SUB
    nblk = e // 2 // pairs
    return pl.pallas_call(
        _pack_kernel,
        grid=(nblk,),
        in_specs=[pl.BlockSpec((pairs, d), lambda i: (i, 0)), pl.BlockSpec((pairs, d), lambda i: (i + nblk, 0))],
        out_specs=pl.BlockSpec((pairs * SUB, 128), lambda i: (i, 0)),
        out_shape=jax.ShapeDtypeStruct((e // 2 * SUB, 128), U32),
        compiler_params=_cparams("parallel"),
    )(tab, tab)


def _gather_tiles(tab_ref, row_ref, t, nk):
    tiles = [tab_ref[pl.ds(pl.multiple_of(row_ref[t, kk], SUB), SUB), :] for kk in range(nk)]
    return pltpu.bitcast(jnp.concatenate(tiles, axis=0), BF16)


def _gather_constants(nk):
    c = np.arange(nk * PAIR_COLS)
    pair = (c[None, :] // PAIR_COLS == np.arange(nk)[:, None])
    hi = (c % 2 == 1)[None, :]
    smask = ((c[None, :] % PAIR_COLS) // 2 == np.arange(SUB)[:, None]).astype(np.float32)
    as_bf16 = lambda m: jnp.asarray(m.astype(np.float32), dtype=BF16)
    return dict(e_all=as_bf16(pair), e_hi=as_bf16(pair & hi), e_lo=as_bf16(pair & ~hi),
                collapse=as_bf16(pair.T), smask=jnp.asarray(smask), hi_row=jnp.asarray(hi.astype(np.float32)))


def _gelu_tanh(x):
    return 0.5 * x * (1.0 + jnp.tanh(math.sqrt(2.0 / math.pi) * (x + 0.044715 * (x * x * x))))


def _peer_act_kernel(row_ref, par_ref, h_ref, gate_ref, tab_ref, eall_ref, coll_ref, smask_ref, hi_ref,
                     w_out, rsum_ref, h8_ref):
    tb, nk = gate_ref.shape
    smask = smask_ref[...]
    for s in range(SUB):
        h8_ref[pl.ds(s, tb, stride=SUB), :] = h_ref[:, s * 128:(s + 1) * 128]

    def token(t, carry):
        g = _gather_tiles(tab_ref, row_ref, t, nk)
        hh, hl = _split(h8_ref[pl.ds(pl.multiple_of(t * SUB, SUB), SUB), :])
        r = _dot_nt(jnp.concatenate([hh, hl], axis=0), g)
        rsum_ref[pl.ds(t, 1), :] = jnp.sum((r[:SUB] + r[SUB:]) * smask, axis=0, keepdims=True)
        return carry

    lax.fori_loop(0, tb, token, 0, unroll=GATHER_UNROLL)
    want_hi = 1.0 - _dot(par_ref[...].astype(F32).astype(BF16), eall_ref[...])
    picked = jnp.where(want_hi == hi_ref[...], rsum_ref[...], 0.0)
    w_out[...] = gate_ref[...] * _gelu_tanh(_mm_lhs2(picked, coll_ref[...]))


def _peer_mix_kernel(row_ref, par_ref, w_ref, x_ref, g2_ref, tab_ref, ehi_ref, elo_ref, smask_ref,
                     o_ref, wcols_ref, acc_ref):
    tb, nk = w_ref.shape
    smask = smask_ref[...]
    par = par_ref[...].astype(F32)
    w = w_ref[...]
    wcols_ref[...] = _mm_lhs2(w * (1.0 - par), ehi_ref[...]) + _mm_lhs2(w * par, elo_ref[...])

    def token(t, carry):
        g = _gather_tiles(tab_ref, row_ref, t, nk)
        wh, wl = _split(wcols_ref[pl.ds(t, 1), :] * smask)
        res = _dot(jnp.concatenate([wh, wl], axis=0), g)
        acc_ref[pl.ds(pl.multiple_of(t * SUB, SUB), SUB), :] = res[:SUB] + res[SUB:]
        return carry

    lax.fori_loop(0, tb, token, 0, unroll=GATHER_UNROLL)
    for s in range(SUB):
        cs = slice(s * 128, (s + 1) * 128)
        o_ref[:, cs] = x_ref[:, cs] + g2_ref[0][:, cs] * acc_ref[pl.ds(s, tb, stride=SUB), :]


def _peer_gather(x, h, row, par, gate, g2, u_tab, v_tab, *, tb):
    bsz, n, d = x.shape
    t_all = bsz * n
    nk = row.shape[-1]
    sl = d // 128
    assert sl == SUB
    row2, par2 = row.reshape(t_all, nk), par.reshape(t_all, nk)
    cst = _gather_constants(nk)
    ncol = nk * PAIR_COLS
    tab_spec = pl.BlockSpec((u_tab.shape[0] // 2 * sl, 128), lambda i: (0, 0), pipeline_mode=pl.Buffered(1))
    smem = pl.BlockSpec((tb, nk), lambda i: (i, 0), memory_space=pltpu.SMEM)
    tokk = pl.BlockSpec((tb, nk), lambda i: (i, 0))
    tokd = pl.BlockSpec((tb, d), lambda i: (i, 0))
    tiles = pltpu.VMEM((tb * sl, 128), F32)
    full = lambda a: pl.BlockSpec(a.shape, lambda i: (0,) * a.ndim)
    w = pl.pallas_call(
        _peer_act_kernel,
        grid=(t_all // tb,),
        in_specs=[smem, tokk, tokd, tokk, tab_spec, full(cst["e_all"]), full(cst["collapse"]),
                  full(cst["smask"]), full(cst["hi_row"])],
        out_specs=tokk,
        out_shape=jax.ShapeDtypeStruct((t_all, nk), F32),
        scratch_shapes=[pltpu.VMEM((tb, ncol), F32), tiles],
        compiler_params=_cparams("arbitrary"),
    )(row2, par2, h.reshape(t_all, d), gate.reshape(t_all, nk), _pack_table(u_tab),
      cst["e_all"], cst["collapse"], cst["smask"], cst["hi_row"])
    out = pl.pallas_call(
        _peer_mix_kernel,
        grid=(t_all // tb,),
        in_specs=[smem, tokk, tokk, tokd, pl.BlockSpec((1, 1, d), lambda i: ((i * tb) // n, 0, 0)),
                  tab_spec, full(cst["e_hi"]), full(cst["e_lo"]), full(cst["smask"])],
        out_specs=tokd,
        out_shape=jax.ShapeDtypeStruct((t_all, d), F32),
        scratch_shapes=[pltpu.VMEM((tb, ncol), F32), tiles],
        compiler_params=_cparams("arbitrary"),
    )(row2, par2, w, x.reshape(t_all, d), g2, _pack_table(v_tab),
      cst["e_hi"], cst["e_lo"], cst["smask"])
    return out.reshape(bsz, n, d)


def _peer_layer(x, nw, sc, sh, g2, w_q, keys, u_tab, v_tab):
    h, row, par, gate = _peer_topk(x, nw, sc, sh, w_q, keys, tb=128)
    return _peer_gather(x, h, row, par, gate, g2, u_tab, v_tab, tb=128)


def _final_norm_kernel(x_ref, w_ref, o_ref):
    x = x_ref[0]
    o_ref[0] = x * lax.rsqrt(jnp.mean(x * x, axis=-1, keepdims=True) + EPS) * w_ref[...]


def _final_norm(x, w, *, tm):
    bsz, n, d = x.shape
    return pl.pallas_call(
        _final_norm_kernel,
        grid=(bsz, n // tm),
        in_specs=[pl.BlockSpec((1, tm, d), lambda b, i: (b, i, 0)), pl.BlockSpec((1, d), lambda b, i: (0, 0))],
        out_specs=pl.BlockSpec((1, tm, d), lambda b, i: (b, i, 0)),
        out_shape=jax.ShapeDtypeStruct((bsz, n, d), F32),
        compiler_params=_cparams("parallel", "arbitrary"),
    )(x, w.reshape(1, d))


def _rope_tables(n_ctx, n_lat):
    pos = np.arange(n_lat)
    axis_dim = B_DH // 2
    inv = ROPE_BASE ** (-np.arange(0, axis_dim, 2, dtype=np.float32) / axis_dim)
    ang = np.concatenate([(pos // GRID_W)[:, None] * inv, (pos % GRID_W)[:, None] * inv], axis=-1)
    ang = np.concatenate([np.zeros((n_ctx, ang.shape[1])), ang], axis=0).astype(np.float32)
    cos = np.cos(ang)
    sin = np.sin(ang)
    cos64 = np.concatenate([cos, cos], axis=1)
    sin64 = np.concatenate([-sin, sin], axis=1)
    return (jnp.asarray(np.tile(cos64, (1, 2)), F32), jnp.asarray(np.tile(sin64, (1, 2)), F32))


def _mixer_ab_layer(x, ctx, mod, mod_c, layer, norm1_w, w_in, conv_w, a_log, dt_bias, gdn_norm_w,
                    lam_params, subln_w, w_out):
    bsz, n_lat, d = x.shape
    n_ctx = ctx.shape[1]
    n = n_ctx + n_lat
    tm = GDN_BLOCK
    lam_init = 0.8 - 0.6 * math.exp(-0.3 * layer)

    xc = jnp.concatenate([ctx, x], axis=1)
    seg = lambda j: jnp.stack([jnp.broadcast_to(mod_c[j], (bsz, d)), mod[:, j]], axis=1)[:, :, None, :]
    cuts = np.cumsum([0, 3 * A_W, A_W, 4 * A_HEADS, B_W, B_W, B_W])
    weights = [w_in[:, a:b] for a, b in zip(cuts[:-1], cuts[1:])]
    qkv, gate, ba, bq, bk, bv = _nm_matmul(
        xc, norm1_w, seg(1), seg(0), weights, [F32, F32, F32, F32, F32, BF16],
        tm=tm, n_first=n_ctx // tm)

    last = qkv[:, tm - 1::tm]
    first = qkv[:, 0::tm]
    zero = jnp.zeros_like(last[:, :1])
    prev_rows = jnp.concatenate([zero, last[:, :-1]], axis=1)
    next_rows = jnp.concatenate([first[:, 1:], zero], axis=1)
    blk = jnp.arange(n // tm)[None, :, None]
    prev_rows = jnp.where(blk == n_ctx // tm, 0.0, prev_rows)[:, :, None, :]
    next_rows = jnp.where(blk == n_ctx // tm - 1, 0.0, next_rows)[:, :, None, :]
    cos128, sin128 = _rope_tables(n_ctx, n_lat)
    q, k, v, bqr, bkr = _prep(qkv, prev_rows, next_rows, conv_w, bq, bk, cos128, sin128, tm=tm)

    bat = jnp.swapaxes(ba, 1, 2)
    o_f = _gdn(q, k, v, ba, bat, a_log, dt_bias, d=0, n_ctx=n_ctx, inv_passes=1)
    o_b = _gdn(q, k, v, ba, bat, a_log, dt_bias, d=1, n_ctx=n_ctx, inv_passes=1)
    d_lat = _diff_attention(bqr, bkr, jnp.swapaxes(bv, 1, 2), lam_params, lam_init=lam_init, n_ctx=n_ctx, tq=512,
                            tk=n // 6 if n % (6 * 128) == 0 else n)
    return _outproj(o_f, o_b, gate, d_lat, x, mod[:, 2][:, None, :], gdn_norm_w, subln_w, w_out,
                    lam_init=lam_init, n_ctx=n_ctx, tm=tm)


def kernel(x, c, ctx, c_ctx, ada_w, ada_b, norm1_w, norm2_w, w_in, conv_w, a_log, dt_bias, gdn_norm_w,
           lam_q1, lam_k1, lam_q2, lam_k2, subln_w, w_out_ab, w_out_f, peer_wq, peer_keys, peer_u,
           peer_v, final_norm_w):
    bsz, n_lat, d = x.shape
    depth = ada_w.shape[0]
    last_ctx_reader = 2 * ((depth - 1) // 2)
    cmat = jnp.concatenate([c, c_ctx[None, :], jnp.zeros((8 - bsz - 1, d), F32)], axis=0)
    for i in range(depth):
        if i < last_ctx_reader:
            raise NotImplementedError("context stream advance (depth > 2) is not implemented")
        mod_all = _modulation(cmat, ada_w[i], ada_b[i])
        mod = mod_all[:bsz].reshape(bsz, 6, d)
        mod_c = mod_all[bsz].reshape(6, d)
        vec = lambda j: mod[:, j][:, None, :]
        j = i // 2
        if i % 2 == 0:
            x = _mixer_ab_layer(x, ctx, mod, mod_c, i, norm1_w[i], w_in[j], conv_w[j], a_log[j], dt_bias[j],
                                gdn_norm_w[j], (lam_q1[j], lam_k1[j], lam_q2[j], lam_k2[j]), subln_w[j],
                                w_out_ab[j])
        else:
            x = _fourier_layer(x, norm1_w[i], vec(1), vec(0), vec(2), w_out_f[j])
        x = _peer_layer(x, norm2_w[i], vec(4), vec(3), vec(5), peer_wq[i], peer_keys[i], peer_u[i],
                        peer_v[i])
    return _final_norm(x, final_norm_w, tm=512)
```

```python
import functools
import math

import numpy as np
import jax
import jax.numpy as jnp
from jax import lax
from jax.experimental import pallas as pl
from jax.experimental.pallas import tpu as pltpu

F32 = jnp.float32
BF16 = jnp.bfloat16
I32 = jnp.int32
U32 = jnp.uint32
HIGHEST = lax.Precision.HIGHEST

EPS = 1e-6
ROPE_BASE = 10000.0
GRID_W = 64
A_HEADS, A_DK, A_DV, A_CHUNK = 8, 64, 64, 64
B_HEADS, B_DH = 4, 64
C_GROUPS = 4
PEER_HEADS, PEER_NKEYS, PEER_DKEY, PEER_TOPK = 8, 128, 256, 16
A_W = A_HEADS * A_DK
B_W = B_HEADS * 2 * B_DH
GDN_BLOCK = 256
GDN_HEAD_GROUP = 8
ATTN_SLAB = 64
DFT_N1 = 128
SUB = 8
PAIR_COLS = 2 * SUB
GATHER_UNROLL = 8

VMEM_LIMIT_BYTES = 48 * 1024 * 1024


def _cparams(*sem):
    return pltpu.CompilerParams(dimension_semantics=sem, vmem_limit_bytes=VMEM_LIMIT_BYTES)


def _dot(a, b):
    return lax.dot_general(a, b, (((1,), (0,)), ((), ())), preferred_element_type=F32)


def _dot_nt(a, b):
    return lax.dot_general(a, b, (((1,), (1,)), ((), ())), preferred_element_type=F32)


def _dot_tn(a, b):
    return lax.dot_general(a, b, (((0,), (0,)), ((), ())), preferred_element_type=F32)


def _dot_f32(a, b):
    return lax.dot_general(a, b, (((1,), (0,)), ((), ())), precision=HIGHEST,
                           preferred_element_type=F32)


def _split(a):
    hi = a.astype(BF16)
    lo = (a - hi.astype(F32)).astype(BF16)
    return hi, lo


def _mm(a, b, passes=1, dot=_dot):
    if passes == 1:
        return dot(a.astype(BF16), b.astype(BF16))
    ah, al = _split(a)
    bh, bl = _split(b)
    return dot(ah, bh) + (dot(ah, bl) + dot(al, bh))


def _mm_lhs2(a, b_bf16):
    ah, al = _split(a)
    return _dot(ah, b_bf16) + _dot(al, b_bf16)


def _silu(x):
    return x * jax.nn.sigmoid(x)


def _softplus(x):
    return jnp.maximum(x, 0.0) + jnp.log(1.0 + jnp.exp(-jnp.abs(x)))


def _normmod(x, nw, sc, sh):
    ms = jnp.mean(x * x, axis=-1, keepdims=True)
    return (x * lax.rsqrt(ms + EPS) * nw) * (1.0 + sc) + sh


def _split_outside(w):
    hi = lax.bitcast_convert_type(lax.bitcast_convert_type(w, U32) & jnp.uint32(0xFFFF0000), F32)
    return hi.astype(BF16), (w - hi).astype(BF16)


def _block_ones(n, blk):
    i = np.arange(n) // blk
    return jnp.asarray((i[:, None] == i[None, :]).astype(np.float32), dtype=BF16)


def _mod_kernel(c_ref, w_ref, b_ref, o_ref):
    o_ref[...] = _dot_f32(_silu(c_ref[...]), w_ref[...]) + b_ref[...]


def _modulation(cmat, w, b):
    rows, d = cmat.shape
    n = w.shape[1]
    return pl.pallas_call(
        _mod_kernel,
        grid=(n // d,),
        in_specs=[pl.BlockSpec((rows, d), lambda j: (0, 0)),
                  pl.BlockSpec((d, d), lambda j: (0, j)),
                  pl.BlockSpec((1, d), lambda j: (0, j))],
        out_specs=pl.BlockSpec((rows, d), lambda j: (0, j)),
        out_shape=jax.ShapeDtypeStruct((rows, n), F32),
        compiler_params=_cparams("arbitrary"),
    )(cmat, w, b.reshape(1, n))


def _nm_matmul_kernel(*refs, widths, passes):
    x_ref, nw_ref, sc_ref, sh_ref = refs[:4]
    n_w = len(widths)
    w_refs = refs[4:4 + n_w * (2 if passes == 3 else 1)]
    outs = refs[4 + len(w_refs):]
    h = _normmod(x_ref[0], nw_ref[...], sc_ref[0, 0], sh_ref[0, 0])
    if passes == 1:
        hb = h.astype(BF16)
        for j in range(n_w):
            outs[j][0] = _dot(hb, w_refs[j][...]).astype(outs[j].dtype)
    else:
        hh, hl = _split(h)
        for j in range(n_w):
            wh, wl = w_refs[2 * j][...], w_refs[2 * j + 1][...]
            outs[j][0] = (_dot(hh, wh) + (_dot(hh, wl) + _dot(hl, wh))).astype(outs[j].dtype)
    if len(outs) > n_w:
        outs[n_w][0] = h


def _nm_matmul(x, nw, sc, sh, weights, out_dtypes, *, tm, n_first=0, passes=1, emit_h=False):
    bsz, n, d = x.shape
    seg = sc.shape[1]
    widths = tuple(w.shape[1] for w in weights)
    w_in = []
    for w in weights:
        if passes == 3:
            w_in += list(_split_outside(w))
        else:
            w_in.append(w.astype(BF16))

    def mod_map(b, i):
        return (b, jnp.where(i < n_first, 0, seg - 1), 0, 0)

    in_specs = [pl.BlockSpec((1, tm, d), lambda b, i: (b, i, 0)),
                pl.BlockSpec((1, d), lambda b, i: (0, 0)),
                pl.BlockSpec((1, 1, 1, d), mod_map),
                pl.BlockSpec((1, 1, 1, d), mod_map)]
    in_specs += [pl.BlockSpec(w.shape, lambda b, i: (0, 0)) for w in w_in]
    out_specs = [pl.BlockSpec((1, tm, wd), lambda b, i: (b, i, 0)) for wd in widths]
    out_shape = [jax.ShapeDtypeStruct((bsz, n, wd), dt) for wd, dt in zip(widths, out_dtypes)]
    if emit_h:
        out_specs.append(pl.BlockSpec((1, tm, d), lambda b, i: (b, i, 0)))
        out_shape.append(jax.ShapeDtypeStruct((bsz, n, d), F32))
    return pl.pallas_call(
        functools.partial(_nm_matmul_kernel, widths=widths, passes=passes),
        grid=(bsz, n // tm),
        in_specs=in_specs, out_specs=out_specs, out_shape=out_shape,
        compiler_params=_cparams("parallel", "arbitrary"),
    )(x, nw.reshape(1, d), sc, sh, *w_in)


def _prep_kernel(qkv_ref, prev_ref, next_ref, cw_ref, bd_ref, bq_ref, bk_ref, cos_ref, sin_ref,
                 q_out, k_out, v_out, bqr_out, bkr_out):
    x = qkv_ref[0]
    tm = x.shape[0]
    row = lax.broadcasted_iota(I32, x.shape, 0)
    xm = jnp.where(row == 0, prev_ref[0, 0], pltpu.roll(x, 1, 0))
    xp = jnp.where(row == tm - 1, next_ref[0, 0], pltpu.roll(x, tm - 1, 0))
    cw = cw_ref[...]
    y = _silu(xm * cw[0:1] + x * cw[1:2] + xp * cw[2:3])
    bd = bd_ref[...]

    def l2(t):
        return t * lax.rsqrt(_mm_lhs2(t * t, bd) + EPS)

    q_out[0] = l2(y[:, :A_W]) * (A_DK ** -0.5)
    k_out[0] = l2(y[:, A_W:2 * A_W])
    v_out[0] = y[:, 2 * A_W:]

    cos = jnp.concatenate([cos_ref[...]] * (B_W // 128), axis=1)
    sin = jnp.concatenate([sin_ref[...]] * (B_W // 128), axis=1)
    lane = lax.broadcasted_iota(I32, (tm, B_W), 1)
    first = (lane & (B_DH - 1)) < (B_DH // 2)

    def rope(t):
        rot = jnp.where(first, pltpu.roll(t, B_W - B_DH // 2, 1), pltpu.roll(t, B_DH // 2, 1))
        return t * cos + rot * sin

    bqr_out[0] = (rope(bq_ref[0]) * (B_DH ** -0.5)).astype(BF16)
    bkr_out[0] = rope(bk_ref[0]).astype(BF16)


def _prep(qkv, prev_rows, next_rows, conv_w, bq, bk, cos128, sin128, *, tm):
    bsz, n, cw = qkv.shape
    tok = lambda b, i: (b, i, 0)
    halo = pl.BlockSpec((1, 1, 1, cw), lambda b, i: (b, i, 0, 0))
    return pl.pallas_call(
        _prep_kernel,
        grid=(bsz, n // tm),
        in_specs=[pl.BlockSpec((1, tm, cw), tok), halo, halo,
                  pl.BlockSpec(conv_w.shape, lambda b, i: (0, 0)),
                  pl.BlockSpec((A_W, A_W), lambda b, i: (0, 0)),
                  pl.BlockSpec((1, tm, B_W), tok), pl.BlockSpec((1, tm, B_W), tok),
                  pl.BlockSpec((tm, 128), lambda b, i: (i, 0)),
                  pl.BlockSpec((tm, 128), lambda b, i: (i, 0))],
        out_specs=[pl.BlockSpec((1, tm, A_W), tok)] * 3 + [pl.BlockSpec((1, tm, B_W), tok)] * 2,
        out_shape=[jax.ShapeDtypeStruct((bsz, n, A_W), F32)] * 3
        + [jax.ShapeDtypeStruct((bsz, n, B_W), BF16)] * 2,
        compiler_params=_cparams("parallel", "arbitrary"),
    )(qkv, prev_rows, next_rows, conv_w, _block_ones(A_W, A_DK), bq, bk, cos128, sin128)


def _gdn_kernel(q_ref, k_ref, v_ref, ba_ref, bat_ref, alr_ref, dtr_ref, alc_ref, dtc_ref,
                lm_ref, lmt_ref, ltot_ref, e_ref, o_ref, s_ref, *, d, inv_passes):
    rev = d == 1
    nh = A_HEADS

    @pl.when(pl.program_id(1) == 0)
    def _():
        s_ref[...] = jnp.zeros_like(s_ref)

    q, k, v = q_ref[0], k_ref[0], v_ref[0]
    ba, bat = ba_ref[0], bat_ref[0]
    n = q.shape[0]
    beta_c = jax.nn.sigmoid(ba[:, d * nh:(d + 1) * nh])
    g_c = -jnp.exp(alr_ref[...]) * _softplus(ba[:, 2 * nh + d * nh:2 * nh + (d + 1) * nh] + dtr_ref[...])
    g_t = -jnp.exp(alc_ref[...]) * _softplus(bat[2 * nh + d * nh:2 * nh + (d + 1) * nh, :] + dtc_ref[...])
    lm, ltot, e = lm_ref[...], ltot_ref[...], e_ref[...]
    gcum_c = _dot_f32(lm, g_c)
    gcum_t = _dot_f32(g_t, lmt_ref[...])
    gcum_e = _dot_f32(gcum_c, e)
    gtot_e = _dot_f32(_dot_f32(ltot, g_c), e)
    beta_e = _dot_f32(beta_c, e)
    eg = jnp.exp(gcum_e)
    kb = k * beta_e
    vb = v * beta_e
    kbe = kb * eg
    qd = q * eg
    ktail = k * jnp.exp(gtot_e - gcum_e)
    egt = jnp.exp(gtot_e)

    ri = lax.broadcasted_iota(I32, (n, n), 0)
    ci = lax.broadcasted_iota(I32, (n, n), 1)
    eye = (ri == ci).astype(F32)
    same = lambda s: (ri >> s) == (ci >> s)
    d16 = same(4).astype(F32)
    o32 = (same(5) & ~same(4)).astype(F32)
    o64 = (same(6) & ~same(5)).astype(F32)
    incl = lm > 0.0
    strict = lm * (1.0 - eye)

    order = range(n // A_CHUNK - 1, -1, -1) if rev else range(n // A_CHUNK)
    sls = [slice(h * A_DK, (h + 1) * A_DK) for h in range(nh)]
    heads = [None] * nh
    for g0 in range(0, nh, GDN_HEAD_GROUP):
        hs = range(g0, g0 + GDN_HEAD_GROUP)
        decay = {h: jnp.exp(jnp.where(incl, gcum_c[:, h:h + 1] - gcum_t[h:h + 1, :], -1e30)) for h in hs}
        a = {h: _mm(kb[:, sls[h]], k[:, sls[h]], 1, _dot_nt) * decay[h] * strict for h in hs}
        p = {h: -(a[h] * d16) for h in hs}
        x = {h: eye + p[h] for h in hs}
        for _ in range(3):
            p = {h: _mm(p[h], p[h], inv_passes) for h in hs}
            x = {h: x[h] + _mm(x[h], p[h], inv_passes) for h in hs}
        for om in (o32, o64):
            y = {h: _mm(x[h], a[h] * om, inv_passes) for h in hs}
            x = {h: x[h] - _mm(y[h], x[h], inv_passes) for h in hs}
        uw = {h: _mm(x[h], jnp.concatenate([vb[:, sls[h]], kbe[:, sls[h]]], axis=1)) for h in hs}
        attn = {h: _mm(q[:, sls[h]], k[:, sls[h]], 1, _dot_nt) * decay[h] for h in hs}
        s = {h: s_ref[h] for h in hs}
        parts = {h: {} for h in hs}
        for c in order:
            r = slice(c * A_CHUNK, (c + 1) * A_CHUNK)
            v_new = {h: uw[h][r, :A_DV] - _mm(uw[h][r, A_DV:], s[h]) for h in hs}
            for h in hs:
                parts[h][c] = _mm(qd[r, sls[h]], s[h]) + _mm(attn[h][r, r], v_new[h])
            s = {h: s[h] * egt[c * A_CHUNK:c * A_CHUNK + 1, sls[h]]
                 + _mm(ktail[r, sls[h]], v_new[h], 1, _dot_tn) for h in hs}
        for h in hs:
            s_ref[h] = s[h]
            heads[h] = jnp.concatenate([parts[h][c] for c in range(n // A_CHUNK)], axis=0)
    o_ref[0] = jnp.concatenate(heads, axis=1)


def _gdn_constants(d):
    n = GDN_BLOCK
    i = np.arange(n)
    same = (i[:, None] // A_CHUNK) == (i[None, :] // A_CHUNK)
    tri = (i[None, :] >= i[:, None]) if d == 1 else (i[None, :] <= i[:, None])
    lm = (same & tri).astype(np.float32)
    e = (np.arange(A_W)[None, :] // A_DK == np.arange(A_HEADS)[:, None]).astype(np.float32)
    return jnp.asarray(lm), jnp.asarray(lm.T.copy()), jnp.asarray(same.astype(np.float32)), jnp.asarray(e)


def _gdn(q, k, v, ba, bat, a_log, dt_bias, *, d, n_ctx, inv_passes):
    bsz, n, _ = q.shape
    nb, nc = n // GDN_BLOCK, n_ctx // GDN_BLOCK
    blk = (lambda s: jnp.where(s < nc, nc - 1 - s, nb - 1 + nc - s)) if d == 1 else (lambda s: s)
    tok = lambda b, s: (b, blk(s), 0)
    full = lambda shape: pl.BlockSpec(shape, lambda b, s: (0,) * len(shape))
    lm, lmt, ltot, e = _gdn_constants(d)
    return pl.pallas_call(
        functools.partial(_gdn_kernel, d=d, inv_passes=inv_passes),
        grid=(bsz, nb),
        in_specs=[pl.BlockSpec((1, GDN_BLOCK, A_W), tok)] * 3
        + [pl.BlockSpec((1, GDN_BLOCK, ba.shape[2]), tok),
           pl.BlockSpec((1, bat.shape[1], GDN_BLOCK), lambda b, s: (b, 0, blk(s))),
           full((1, A_HEADS)), full((1, A_HEADS)), full((A_HEADS, 1)), full((A_HEADS, 1)),
           full(lm.shape), full(lm.shape), full(lm.shape), full(e.shape)],
        out_specs=pl.BlockSpec((1, GDN_BLOCK, A_W), tok),
        out_shape=jax.ShapeDtypeStruct((bsz, n, A_W), F32),
        scratch_shapes=[pltpu.VMEM((A_HEADS, A_DK, A_DV), F32)],
        compiler_params=_cparams("parallel", "arbitrary"),
    )(q, k, v, ba, bat, a_log[d].reshape(1, -1), dt_bias[d].reshape(1, -1),
      a_log[d].reshape(-1, 1), dt_bias[d].reshape(-1, 1), lm, lmt, ltot, e)


def _attn_kernel(q_ref, k_ref, v_ref, lq1_ref, lk1_ref, lq2_ref, lk2_ref, o_ref, *, lam_init, tk):
    q = q_ref[0]
    tq = q.shape[0]
    nk = k_ref.shape[1]
    lam = (jnp.exp(jnp.sum(lq1_ref[...] * lk1_ref[...], keepdims=True))
           - jnp.exp(jnp.sum(lq2_ref[...] * lk2_ref[...], keepdims=True)) + lam_init)
    def over_keys(op, t):
        return op(op(t.reshape(tk // ATTN_SLAB, ATTN_SLAB, tq), axis=0), axis=0, keepdims=True)

    qs = [q[:, m * B_DH:(m + 1) * B_DH] for m in range(2)]
    mx = [jnp.full((1, tq), -jnp.inf, F32)] * 2
    l = [jnp.zeros((1, tq), F32)] * 2
    acc = [jnp.zeros((2 * B_DH, tq), F32)] * 2
    for c in range(nk // tk):
        ks = slice(c * tk, (c + 1) * tk)
        s = [_dot_nt(k_ref[0, ks, m * B_DH:(m + 1) * B_DH], qs[m]) for m in range(2)]
        m_new = [jnp.maximum(mx[m], over_keys(jnp.max, s[m])) for m in range(2)]
        alpha = [jnp.exp(mx[m] - m_new[m]) for m in range(2)]
        p = [jnp.exp(s[m] - m_new[m]) for m in range(2)]
        l = [alpha[m] * l[m] + over_keys(jnp.sum, p[m]) for m in range(2)]
        acc = [alpha[m] * acc[m] + _dot(v_ref[0, :, ks], p[m].astype(BF16)) for m in range(2)]
        mx = m_new
    o_ref[0] = (acc[0] / l[0] - lam * (acc[1] / l[1])).T


def _diff_attention(bqr, bkr, bv, lam_params, *, lam_init, n_ctx, tq, tk):
    bsz, n, _ = bqr.shape
    n_lat = n - n_ctx
    hw = 2 * B_DH
    assert n_lat % tq == 0 and n % tk == 0 and tk % ATTN_SLAB == 0
    lam_spec = pl.BlockSpec((1, B_DH), lambda b, h, i: (0, 0))
    return pl.pallas_call(
        functools.partial(_attn_kernel, lam_init=lam_init, tk=tk),
        grid=(bsz, B_HEADS, n_lat // tq),
        in_specs=[pl.BlockSpec((1, tq, hw), lambda b, h, i: (b, i, h)),
                  pl.BlockSpec((1, n, hw), lambda b, h, i: (b, 0, h)),
                  pl.BlockSpec((1, hw, n), lambda b, h, i: (b, h, 0))] + [lam_spec] * 4,
        out_specs=pl.BlockSpec((1, tq, hw), lambda b, h, i: (b, i, h)),
        out_shape=jax.ShapeDtypeStruct((bsz, n_lat, B_W), F32),
        compiler_params=_cparams("parallel", "parallel", "arbitrary"),
    )(bqr[:, n_ctx:], bkr, bv, *[p.reshape(1, B_DH) for p in lam_params])


def _outproj_kernel(of_ref, ob_ref, gate_ref, d_ref, x_ref, g1_ref, gw_ref, sw_ref, bd64_ref,
                    bd128_ref, w_ref, o_ref, *, lam_init):
    o = of_ref[0] + ob_ref[0]
    ms = _mm_lhs2(o * o, bd64_ref[...]) * (1.0 / A_DV)
    y1 = (o * lax.rsqrt(ms + EPS) * gw_ref[...]) * _silu(gate_ref[0])
    dd = d_ref[0]
    ms2 = _mm_lhs2(dd * dd, bd128_ref[...]) * (1.0 / (2 * B_DH))
    y2 = (dd * lax.rsqrt(ms2 + EPS) * sw_ref[...]) * (1.0 - lam_init)
    ycat = jnp.concatenate([y1, y2], axis=1).astype(BF16)
    o_ref[0] = x_ref[0] + g1_ref[0] * _dot(ycat, w_ref[...])


def _outproj(o_f, o_b, gate, d_lat, x, g1, gdn_norm_w, subln_w, w_out, *, lam_init, n_ctx, tm):
    bsz, n_lat, d = x.shape
    off = n_ctx // tm
    tok = lambda b, i: (b, i, 0)
    tok_off = lambda b, i: (b, i + off, 0)
    full = lambda shape: pl.BlockSpec(shape, lambda b, i: (0,) * len(shape))
    return pl.pallas_call(
        functools.partial(_outproj_kernel, lam_init=lam_init),
        grid=(bsz, n_lat // tm),
        in_specs=[pl.BlockSpec((1, tm, A_W), tok_off)] * 3
        + [pl.BlockSpec((1, tm, B_W), tok), pl.BlockSpec((1, tm, d), tok),
           pl.BlockSpec((1, 1, d), lambda b, i: (b, 0, 0)),
           full((1, A_W)), full((1, B_W)), full((A_W, A_W)), full((B_W, B_W)), full(w_out.shape)],
        out_specs=pl.BlockSpec((1, tm, d), tok),
        out_shape=jax.ShapeDtypeStruct((bsz, n_lat, d), F32),
        compiler_params=_cparams("parallel", "arbitrary"),
    )(o_f, o_b, gate, d_lat, x, g1, jnp.tile(gdn_norm_w, A_HEADS).reshape(1, A_W),
      jnp.tile(subln_w, B_HEADS).reshape(1, B_W), _block_ones(A_W, A_DV), _block_ones(B_W, 2 * B_DH),
      w_out.astype(BF16))


def _dft1_kernel(x_ref, nw_ref, sc_ref, sh_ref, f_ref, twr_ref, twi_ref, re_ref, im_ref, *, d, passes):
    n1 = f_ref.shape[1]
    f = f_ref[...]
    for t in range(x_ref.shape[2] // d):
        cs = slice(t * d, (t + 1) * d)
        h = _normmod(x_ref[0, :, cs], nw_ref[...], sc_ref[0], sh_ref[0])
        a = _mm(f, h, passes)
        a_re, a_im = a[:n1], a[n1:]
        twr = jnp.concatenate([twr_ref[:, t * 128:(t + 1) * 128]] * (d // 128), axis=1)
        twi = jnp.concatenate([twi_ref[:, t * 128:(t + 1) * 128]] * (d // 128), axis=1)
        re_ref[0, :, cs] = a_re * twr - a_im * twi
        im_ref[0, :, cs] = a_re * twi + a_im * twr


def _dft2_kernel(re_ref, im_ref, x_ref, g1_ref, f_ref, cc_ref, sc_ref, w_ref, o_ref, *, d, passes):
    n2 = f_ref.shape[1]
    f = f_ref[...]
    dg = d // C_GROUPS
    for t in range(re_ref.shape[1]):
        pr = _mm(f, re_ref[0, t], passes)
        pi = _mm(f, im_ref[0, t], passes)
        x_re = pr[:n2] + pi[n2:]
        x_im = pi[:n2] - pr[n2:]
        y = jnp.concatenate(
            [_mm(x_re[:, g * dg:(g + 1) * dg], cc_ref[...], passes)
             + _mm(x_im[:, g * dg:(g + 1) * dg], sc_ref[...], passes) for g in range(C_GROUPS)], axis=1)
        cs = slice(t * d, (t + 1) * d)
        o_ref[0, :, cs] = x_ref[0, :, cs] + g1_ref[0] * _dot(y.astype(BF16), w_ref[...])


def _fourier_layer(x, nw, sc, sh, g1, w_out, *, passes=3, nb=2, kb=8):
    bsz, n, d = x.shape
    n1, n2 = DFT_N1, n // DFT_N1
    dg = d // C_GROUPS
    k1 = np.arange(n1)
    ang1 = 2 * np.pi * np.outer(k1, k1) / n1
    f1 = jnp.asarray(np.concatenate([np.cos(ang1), -np.sin(ang1)], 0), F32)
    angt = 2 * np.pi * np.outer(k1, np.arange(n2)) / n
    twr = jnp.asarray(np.repeat(np.cos(angt), 128, axis=1), F32)
    twi = jnp.asarray(np.repeat(-np.sin(angt), 128, axis=1), F32)
    k2 = np.arange(n2)
    ang2 = 2 * np.pi * np.outer(k2, k2) / n2
    f2 = jnp.asarray(np.concatenate([np.cos(ang2), np.sin(ang2)], 0), F32)
    angc = 2 * np.pi * np.outer(np.arange(dg), np.arange(dg)) / dg
    scale = 1.0 / math.sqrt(n * dg)
    cc = jnp.asarray(np.cos(angc) * scale, F32)
    sn = jnp.asarray(np.sin(angc) * scale, F32)

    xv = x.reshape(bsz, n1, n2 * d)
    col = lambda b, j: (b, 0, j)
    vec = pl.BlockSpec((1, 1, d), lambda b, j: (b, 0, 0))
    full = lambda shape: pl.BlockSpec(shape, lambda b, j: (0,) * len(shape))
    a_re, a_im = pl.pallas_call(
        functools.partial(_dft1_kernel, d=d, passes=passes),
        grid=(bsz, n2 // nb),
        in_specs=[pl.BlockSpec((1, n1, nb * d), col), full((1, d)), vec, vec, full(f1.shape),
                  pl.BlockSpec((n1, nb * 128), lambda b, j: (0, j)),
                  pl.BlockSpec((n1, nb * 128), lambda b, j: (0, j))],
        out_specs=[pl.BlockSpec((1, n1, nb * d), col)] * 2,
        out_shape=[jax.ShapeDtypeStruct((bsz, n1, n2 * d), F32)] * 2,
        compiler_params=_cparams("parallel", "arbitrary"),
    )(xv, nw.reshape(1, d), sc, sh, f1, twr, twi)

    xo = x.reshape(bsz, n2, n1 * d)
    out = pl.pallas_call(
        functools.partial(_dft2_kernel, d=d, passes=passes),
        grid=(bsz, n1 // kb),
        in_specs=[pl.BlockSpec((1, kb, n2, d), lambda b, j: (b, j, 0, 0))] * 2
        + [pl.BlockSpec((1, n2, kb * d), col), vec, full(f2.shape), full(cc.shape), full(sn.shape),
           full(w_out.shape)],
        out_specs=pl.BlockSpec((1, n2, kb * d), col),
        out_shape=jax.ShapeDtypeStruct((bsz, n2, n1 * d), F32),
        compiler_params=_cparams("parallel", "arbitrary"),
    )(a_re.reshape(bsz, n1, n2, d), a_im.reshape(bsz, n1, n2, d), xo, g1, f2, cc, sn,
      w_out.astype(BF16))
    return out.reshape(bsz, n, d)


def _top16(s, payload=None):
    rows = lax.broadcasted_iota(I32, s.shape, 0).astype(F32)
    big = float(s.shape[0])
    vals, idxs = [], []
    for _ in range(PEER_TOPK):
        m = jnp.max(s, axis=0, keepdims=True)
        am = jnp.min(jnp.where(s == m, rows, big), axis=0, keepdims=True)
        sel = rows == am
        vals.append(m)
        if payload is None:
            idxs.append(am)
        else:
            idxs.append(jnp.max(jnp.where(sel, payload, -1.0), axis=0, keepdims=True))
        s = jnp.where(sel, -jnp.inf, s)
    return jnp.concatenate(vals, axis=0), jnp.concatenate(idxs, axis=0)


def _staircase_candidates(s1, i1, s2, i2):
    k = PEER_TOPK
    sub = lax.broadcasted_iota(I32, (SUB,) + s1.shape[1:], 0)
    cs = [s1[0:1] + s2, s1[1:2] + s2[:SUB]]
    ci = [i1[0:1] * PEER_NKEYS + i2, i1[1:2] * PEER_NKEYS + i2[:SUB]]
    for a in range(2, SUB):
        keep = sub < k // (a + 1)
        cs.append(jnp.where(keep, s1[a:a + 1] + s2[:SUB], -jnp.inf))
        ci.append(jnp.where(keep, i1[a:a + 1] * PEER_NKEYS + i2[:SUB], -1.0))
    cs.append(s1[SUB:] + s2[0:1])
    ci.append(i1[SUB:] * PEER_NKEYS + i2[0:1])
    return jnp.concatenate(cs, axis=0), jnp.concatenate(ci, axis=0)


def _peer_topk_kernel(x_ref, nw_ref, sc_ref, sh_ref, wh_ref, wl_ref, keys_ref,
                      h_out, row_out, par_out, gate_out):
    h = _normmod(x_ref[0], nw_ref[...], sc_ref[0], sh_ref[0])
    h_out[0] = h
    hh, hl = _split(h)
    q = _dot(hh, wh_ref[...]) + (_dot(hh, wl_ref[...]) + _dot(hl, wh_ref[...]))
    half = PEER_DKEY // 2
    idx_rows, gate_rows = [], []
    for hd in range(PEER_HEADS):
        tops = []
        for p in range(2):
            j = hd * 2 + p
            st = _mm(keys_ref[j], q[:, j * half:(j + 1) * half], 3, _dot_nt)
            tops.append(_top16(st))
        (s1, i1), (s2, i2) = tops
        top_s, top_i = _top16(*_staircase_candidates(s1, i1, s2, i2))
        ex = jnp.exp(top_s - top_s[0:1])
        gate_rows.append(ex / jnp.sum(ex, axis=0, keepdims=True))
        idx_rows.append(top_i)
    idx = jnp.concatenate(idx_rows, axis=0).T.astype(I32)
    n_pairs = PEER_NKEYS * PEER_NKEYS // 2
    row_out[0] = (idx & (n_pairs - 1)) * SUB
    par_out[0] = (idx >= n_pairs).astype(I32)
    gate_out[0] = jnp.concatenate(gate_rows, axis=0).T


def _peer_topk(x, nw, sc, sh, w_q, keys, *, tb):
    bsz, n, d = x.shape
    nk = PEER_HEADS * PEER_TOPK
    wh, wl = _split_outside(w_q)
    keys2 = keys.reshape(PEER_HEADS * 2, PEER_NKEYS, PEER_DKEY // 2)
    tok = lambda b, i: (b, i, 0)
    vec = pl.BlockSpec((1, 1, d), lambda b, i: (b, 0, 0))
    full = lambda shape: pl.BlockSpec(shape, lambda b, i: (0,) * len(shape))
    return pl.pallas_call(
        _peer_topk_kernel,
        grid=(bsz, n // tb),
        in_specs=[pl.BlockSpec((1, tb, d), tok), full((1, d)), vec, vec, full(wh.shape), full(wl.shape),
                  full(keys2.shape)],
        out_specs=[pl.BlockSpec((1, tb, d), tok)] + [pl.BlockSpec((1, tb, nk), tok)] * 3,
        out_shape=[jax.ShapeDtypeStruct((bsz, n, d), F32), jax.ShapeDtypeStruct((bsz, n, nk), I32),
                   jax.ShapeDtypeStruct((bsz, n, nk), I32), jax.ShapeDtypeStruct((bsz, n, nk), F32)],
        compiler_params=_cparams("parallel", "arbitrary"),
    )(x, nw.reshape(1, d), sc, sh, wh, wl, keys2)


def _pack_kernel(hi_ref, lo_ref, o_ref):
    pairs, d = hi_ref.shape
    bits = lambda v: lax.bitcast_convert_type(v.astype(BF16).astype(F32), U32)
    packed = bits(hi_ref[...]) | (bits(lo_ref[...]) >> 16)
    for s in range(d // 128):
        o_ref[pl.ds(s, pairs, stride=SUB), :] = packed[:, s * 128:(s + 1) * 128]


def _pack_table(tab, *, pairs=256):
    e, d = tab.shape
    assert d // 128 == SUB
    nblk = e // 2 // pairs
    return pl.pallas_call(
        _pack_kernel,
        grid=(nblk,),
        in_specs=[pl.BlockSpec((pairs, d), lambda i: (i, 0)), pl.BlockSpec((pairs, d), lambda i: (i + nblk, 0))],
        out_specs=pl.BlockSpec((pairs * SUB, 128), lambda i: (i, 0)),
        out_shape=jax.ShapeDtypeStruct((e // 2 * SUB, 128), U32),
        compiler_params=_cparams("parallel"),
    )(tab, tab)


def _gather_tiles(tab_ref, row_ref, t, nk):
    tiles = [tab_ref[pl.ds(pl.multiple_of(row_ref[t, kk], SUB), SUB), :] for kk in range(nk)]
    return pltpu.bitcast(jnp.concatenate(tiles, axis=0), BF16)


def _gather_constants(nk):
    c = np.arange(nk * PAIR_COLS)
    pair = (c[None, :] // PAIR_COLS == np.arange(nk)[:, None])
    hi = (c % 2 == 1)[None, :]
    smask = ((c[None, :] % PAIR_COLS) // 2 == np.arange(SUB)[:, None]).astype(np.float32)
    as_bf16 = lambda m: jnp.asarray(m.astype(np.float32), dtype=BF16)
    return dict(e_all=as_bf16(pair), e_hi=as_bf16(pair & hi), e_lo=as_bf16(pair & ~hi),
                collapse=as_bf16(pair.T), smask=jnp.asarray(smask), hi_row=jnp.asarray(hi.astype(np.float32)))


def _gelu_tanh(x):
    return 0.5 * x * (1.0 + jnp.tanh(math.sqrt(2.0 / math.pi) * (x + 0.044715 * (x * x * x))))


def _peer_act_kernel(row_ref, par_ref, h_ref, gate_ref, tab_ref, eall_ref, coll_ref, smask_ref, hi_ref,
                     w_out, rsum_ref, h8_ref):
    tb, nk = gate_ref.shape
    smask = smask_ref[...]
    for s in range(SUB):
        h8_ref[pl.ds(s, tb, stride=SUB), :] = h_ref[:, s * 128:(s + 1) * 128]

    def token(t, carry):
        g = _gather_tiles(tab_ref, row_ref, t, nk)
        hh, hl = _split(h8_ref[pl.ds(pl.multiple_of(t * SUB, SUB), SUB), :])
        r = _dot_nt(jnp.concatenate([hh, hl], axis=0), g)
        rsum_ref[pl.ds(t, 1), :] = jnp.sum((r[:SUB] + r[SUB:]) * smask, axis=0, keepdims=True)
        return carry

    lax.fori_loop(0, tb, token, 0, unroll=2 * GATHER_UNROLL)
    want_hi = 1.0 - _dot(par_ref[...].astype(F32).astype(BF16), eall_ref[...])
    picked = jnp.where(want_hi == hi_ref[...], rsum_ref[...], 0.0)
    w_out[...] = gate_ref[...] * _gelu_tanh(_mm_lhs2(picked, coll_ref[...]))


def _peer_mix_kernel(row_ref, par_ref, w_ref, x_ref, g2_ref, tab_ref, ehi_ref, elo_ref, smask_ref,
                     o_ref, wcols_ref, acc_ref):
    tb, nk = w_ref.shape
    smask = smask_ref[...]
    par = par_ref[...].astype(F32)
    w = w_ref[...]
    wcols_ref[...] = _mm_lhs2(w * (1.0 - par), ehi_ref[...]) + _mm_lhs2(w * par, elo_ref[...])

    def token(t, carry):
        g = _gather_tiles(tab_ref, row_ref, t, nk)
        wh, wl = _split(wcols_ref[pl.ds(t, 1), :] * smask)
        res = _dot(jnp.concatenate([wh, wl], axis=0), g)
        acc_ref[pl.ds(pl.multiple_of(t * SUB, SUB), SUB), :] = res[:SUB] + res[SUB:]
        return carry

    lax.fori_loop(0, tb, token, 0, unroll=GATHER_UNROLL)
    for s in range(SUB):
        cs = slice(s * 128, (s + 1) * 128)
        o_ref[:, cs] = x_ref[:, cs] + g2_ref[0][:, cs] * acc_ref[pl.ds(s, tb, stride=SUB), :]


def _peer_gather(x, h, row, par, gate, g2, u_tab, v_tab, *, tb):
    bsz, n, d = x.shape
    t_all = bsz * n
    nk = row.shape[-1]
    sl = d // 128
    assert sl == SUB
    row2, par2 = row.reshape(t_all, nk), par.reshape(t_all, nk)
    cst = _gather_constants(nk)
    ncol = nk * PAIR_COLS
    tab_spec = pl.BlockSpec((u_tab.shape[0] // 2 * sl, 128), lambda i: (0, 0), pipeline_mode=pl.Buffered(1))
    smem = pl.BlockSpec((tb, nk), lambda i: (i, 0), memory_space=pltpu.SMEM)
    tokk = pl.BlockSpec((tb, nk), lambda i: (i, 0))
    tokd = pl.BlockSpec((tb, d), lambda i: (i, 0))
    tiles = pltpu.VMEM((tb * sl, 128), F32)
    full = lambda a: pl.BlockSpec(a.shape, lambda i: (0,) * a.ndim)
    w = pl.pallas_call(
        _peer_act_kernel,
        grid=(t_all // tb,),
        in_specs=[smem, tokk, tokd, tokk, tab_spec, full(cst["e_all"]), full(cst["collapse"]),
                  full(cst["smask"]), full(cst["hi_row"])],
        out_specs=tokk,
        out_shape=jax.ShapeDtypeStruct((t_all, nk), F32),
        scratch_shapes=[pltpu.VMEM((tb, ncol), F32), tiles],
        compiler_params=_cparams("arbitrary"),
    )(row2, par2, h.reshape(t_all, d), gate.reshape(t_all, nk), _pack_table(u_tab),
      cst["e_all"], cst["collapse"], cst["smask"], cst["hi_row"])
    out = pl.pallas_call(
        _peer_mix_kernel,
        grid=(t_all // tb,),
        in_specs=[smem, tokk, tokk, tokd, pl.BlockSpec((1, 1, d), lambda i: ((i * tb) // n, 0, 0)),
                  tab_spec, full(cst["e_hi"]), full(cst["e_lo"]), full(cst["smask"])],
        out_specs=tokd,
        out_shape=jax.ShapeDtypeStruct((t_all, d), F32),
        scratch_shapes=[pltpu.VMEM((tb, ncol), F32), tiles],
        compiler_params=_cparams("arbitrary"),
    )(row2, par2, w, x.reshape(t_all, d), g2, _pack_table(v_tab),
      cst["e_hi"], cst["e_lo"], cst["smask"])
    return out.reshape(bsz, n, d)


def _peer_layer(x, nw, sc, sh, g2, w_q, keys, u_tab, v_tab):
    h, row, par, gate = _peer_topk(x, nw, sc, sh, w_q, keys, tb=128)
    return _peer_gather(x, h, row, par, gate, g2, u_tab, v_tab, tb=128)


def _final_norm_kernel(x_ref, w_ref, o_ref):
    x = x_ref[0]
    o_ref[0] = x * lax.rsqrt(jnp.mean(x * x, axis=-1, keepdims=True) + EPS) * w_ref[...]


def _final_norm(x, w, *, tm):
    bsz, n, d = x.shape
    return pl.pallas_call(
        _final_norm_kernel,
        grid=(bsz, n // tm),
        in_specs=[pl.BlockSpec((1, tm, d), lambda b, i: (b, i, 0)), pl.BlockSpec((1, d), lambda b, i: (0, 0))],
        out_specs=pl.BlockSpec((1, tm, d), lambda b, i: (b, i, 0)),
        out_shape=jax.ShapeDtypeStruct((bsz, n, d), F32),
        compiler_params=_cparams("parallel", "arbitrary"),
    )(x, w.reshape(1, d))


def _rope_tables(n_ctx, n_lat):
    pos = np.arange(n_lat)
    axis_dim = B_DH // 2
    inv = ROPE_BASE ** (-np.arange(0, axis_dim, 2, dtype=np.float32) / axis_dim)
    ang = np.concatenate([(pos // GRID_W)[:, None] * inv, (pos % GRID_W)[:, None] * inv], axis=-1)
    ang = np.concatenate([np.zeros((n_ctx, ang.shape[1])), ang], axis=0).astype(np.float32)
    cos = np.cos(ang)
    sin = np.sin(ang)
    cos64 = np.concatenate([cos, cos], axis=1)
    sin64 = np.concatenate([-sin, sin], axis=1)
    return (jnp.asarray(np.tile(cos64, (1, 2)), F32), jnp.asarray(np.tile(sin64, (1, 2)), F32))


def _mixer_ab_layer(x, ctx, mod, mod_c, layer, norm1_w, w_in, conv_w, a_log, dt_bias, gdn_norm_w,
                    lam_params, subln_w, w_out):
    bsz, n_lat, d = x.shape
    n_ctx = ctx.shape[1]
    n = n_ctx + n_lat
    tm = GDN_BLOCK
    lam_init = 0.8 - 0.6 * math.exp(-0.3 * layer)

    xc = jnp.concatenate([ctx, x], axis=1)
    seg = lambda j: jnp.stack([jnp.broadcast_to(mod_c[j], (bsz, d)), mod[:, j]], axis=1)[:, :, None, :]
    cuts = np.cumsum([0, 3 * A_W, A_W, 4 * A_HEADS, B_W, B_W, B_W])
    weights = [w_in[:, a:b] for a, b in zip(cuts[:-1], cuts[1:])]
    qkv, gate, ba, bq, bk, bv = _nm_matmul(
        xc, norm1_w, seg(1), seg(0), weights, [F32, F32, F32, F32, F32, BF16],
        tm=tm, n_first=n_ctx // tm)

    last = qkv[:, tm - 1::tm]
    first = qkv[:, 0::tm]
    zero = jnp.zeros_like(last[:, :1])
    prev_rows = jnp.concatenate([zero, last[:, :-1]], axis=1)
    next_rows = jnp.concatenate([first[:, 1:], zero], axis=1)
    blk = jnp.arange(n // tm)[None, :, None]
    prev_rows = jnp.where(blk == n_ctx // tm, 0.0, prev_rows)[:, :, None, :]
    next_rows = jnp.where(blk == n_ctx // tm - 1, 0.0, next_rows)[:, :, None, :]
    cos128, sin128 = _rope_tables(n_ctx, n_lat)
    q, k, v, bqr, bkr = _prep(qkv, prev_rows, next_rows, conv_w, bq, bk, cos128, sin128, tm=tm)

    bat = jnp.swapaxes(ba, 1, 2)
    o_f = _gdn(q, k, v, ba, bat, a_log, dt_bias, d=0, n_ctx=n_ctx, inv_passes=1)
    o_b = _gdn(q, k, v, ba, bat, a_log, dt_bias, d=1, n_ctx=n_ctx, inv_passes=1)
    d_lat = _diff_attention(bqr, bkr, jnp.swapaxes(bv, 1, 2), lam_params, lam_init=lam_init, n_ctx=n_ctx, tq=512,
                            tk=n // 6 if n % (6 * 128) == 0 else n)
    return _outproj(o_f, o_b, gate, d_lat, x, mod[:, 2][:, None, :], gdn_norm_w, subln_w, w_out,
                    lam_init=lam_init, n_ctx=n_ctx, tm=tm)


def kernel(x, c, ctx, c_ctx, ada_w, ada_b, norm1_w, norm2_w, w_in, conv_w, a_log, dt_bias, gdn_norm_w,
           lam_q1, lam_k1, lam_q2, lam_k2, subln_w, w_out_ab, w_out_f, peer_wq, peer_keys, peer_u,
           peer_v, final_norm_w):
    bsz, n_lat, d = x.shape
    depth = ada_w.shape[0]
    last_ctx_reader = 2 * ((depth - 1) // 2)
    cmat = jnp.concatenate([c, c_ctx[None, :], jnp.zeros((8 - bsz - 1, d), F32)], axis=0)
    for i in range(depth):
        if i < last_ctx_reader:
            raise NotImplementedError("context stream advance (depth > 2) is not implemented")
        mod_all = _modulation(cmat, ada_w[i], ada_b[i])
        mod = mod_all[:bsz].reshape(bsz, 6, d)
        mod_c = mod_all[bsz].reshape(6, d)
        vec = lambda j: mod[:, j][:, None, :]
        j = i // 2
        if i % 2 == 0:
            x = _mixer_ab_layer(x, ctx, mod, mod_c, i, norm1_w[i], w_in[j], conv_w[j], a_log[j], dt_bias[j],
                                gdn_norm_w[j], (lam_q1[j], lam_k1[j], lam_q2[j], lam_k2[j]), subln_w[j],
                                w_out_ab[j])
        else:
            x = _fourier_layer(x, norm1_w[i], vec(1), vec(0), vec(2), w_out_f[j])
        x = _peer_layer(x, norm2_w[i], vec(4), vec(3), vec(5), peer_wq[i], peer_keys[i], peer_u[i],
                        peer_v[i])
    return _final_norm(x, final_norm_w, tm=512)
```

```python
import functools
import math

import numpy as np
import jax
import jax.numpy as jnp
from jax import lax
from jax.experimental import pallas as pl
from jax.experimental.pallas import tpu as pltpu

F32 = jnp.float32
BF16 = jnp.bfloat16
I32 = jnp.int32
U32 = jnp.uint32
HIGHEST = lax.Precision.HIGHEST

EPS = 1e-6
ROPE_BASE = 10000.0
GRID_W = 64
A_HEADS, A_DK, A_DV, A_CHUNK = 8, 64, 64, 64
B_HEADS, B_DH = 4, 64
C_GROUPS = 4
PEER_HEADS, PEER_NKEYS, PEER_DKEY, PEER_TOPK = 8, 128, 256, 16
A_W = A_HEADS * A_DK
B_W = B_HEADS * 2 * B_DH
GDN_BLOCK = 256
GDN_HEAD_GROUP = 8
ATTN_SLAB = 64
DFT_N1 = 128
SUB = 8
PAIR_COLS = 2 * SUB
GATHER_UNROLL = 8

VMEM_LIMIT_BYTES = 48 * 1024 * 1024


def _cparams(*sem):
    return pltpu.CompilerParams(dimension_semantics=sem, vmem_limit_bytes=VMEM_LIMIT_BYTES)


def _dot(a, b):
    return lax.dot_general(a, b, (((1,), (0,)), ((), ())), preferred_element_type=F32)


def _dot_nt(a, b):
    return lax.dot_general(a, b, (((1,), (1,)), ((), ())), preferred_element_type=F32)


def _dot_tn(a, b):
    return lax.dot_general(a, b, (((0,), (0,)), ((), ())), preferred_element_type=F32)


def _dot_f32(a, b):
    return lax.dot_general(a, b, (((1,), (0,)), ((), ())), precision=HIGHEST,
                           preferred_element_type=F32)


def _split(a):
    hi = a.astype(BF16)
    lo = (a - hi.astype(F32)).astype(BF16)
    return hi, lo


def _mm(a, b, passes=1, dot=_dot):
    if passes == 1:
        return dot(a.astype(BF16), b.astype(BF16))
    ah, al = _split(a)
    bh, bl = _split(b)
    return dot(ah, bh) + (dot(ah, bl) + dot(al, bh))


def _mm_lhs2(a, b_bf16):
    ah, al = _split(a)
    return _dot(ah, b_bf16) + _dot(al, b_bf16)


def _silu(x):
    return x * jax.nn.sigmoid(x)


def _softplus(x):
    return jnp.maximum(x, 0.0) + jnp.log(1.0 + jnp.exp(-jnp.abs(x)))


def _normmod(x, nw, sc, sh):
    ms = jnp.mean(x * x, axis=-1, keepdims=True)
    return (x * lax.rsqrt(ms + EPS) * nw) * (1.0 + sc) + sh


def _split_outside(w):
    hi = lax.bitcast_convert_type(lax.bitcast_convert_type(w, U32) & jnp.uint32(0xFFFF0000), F32)
    return hi.astype(BF16), (w - hi).astype(BF16)


def _block_ones(n, blk):
    i = np.arange(n) // blk
    return jnp.asarray((i[:, None] == i[None, :]).astype(np.float32), dtype=BF16)


def _mod_kernel(c_ref, w_ref, b_ref, o_ref):
    o_ref[...] = _dot_f32(_silu(c_ref[...]), w_ref[...]) + b_ref[...]


def _modulation(cmat, w, b):
    rows, d = cmat.shape
    n = w.shape[1]
    return pl.pallas_call(
        _mod_kernel,
        grid=(n // d,),
        in_specs=[pl.BlockSpec((rows, d), lambda j: (0, 0)),
                  pl.BlockSpec((d, d), lambda j: (0, j)),
                  pl.BlockSpec((1, d), lambda j: (0, j))],
        out_specs=pl.BlockSpec((rows, d), lambda j: (0, j)),
        out_shape=jax.ShapeDtypeStruct((rows, n), F32),
        compiler_params=_cparams("arbitrary"),
    )(cmat, w, b.reshape(1, n))


def _in_proj_kernel(*refs, n_w, n_first):
    ctx_ref, x_ref, nw_ref, sc_ref, sh_ref = refs[:5]
    w_refs = refs[5:5 + n_w]
    outs = refs[5 + n_w:5 + 2 * n_w]
    edge_ref = refs[5 + 2 * n_w]
    xin = jnp.where(pl.program_id(1) < n_first, ctx_ref[0], x_ref[0])
    hb = _normmod(xin, nw_ref[...], sc_ref[0, 0], sh_ref[0, 0]).astype(BF16)
    for j in range(n_w):
        y = _dot(hb, w_refs[j][...])
        outs[j][0] = y.astype(outs[j].dtype)
        if j == 0:
            tm = y.shape[0]
            edge_ref[0, 0] = jnp.concatenate(
                [y[0:1], y[tm - 1:tm], jnp.zeros((SUB - 2, y.shape[1]), F32)], axis=0)


def _in_proj(ctx, x, nw, sc, sh, weights, out_dtypes, *, tm):
    bsz, n_lat, d = x.shape
    n_first = ctx.shape[1] // tm
    nblk = n_first + n_lat // tm
    widths = tuple(w.shape[1] for w in weights)
    w_in = [w.astype(BF16) for w in weights]
    mod_map = lambda b, i: (b, jnp.where(i < n_first, 0, 1), 0, 0)
    tok = lambda b, i: (b, i, 0)
    in_specs = [pl.BlockSpec((1, tm, d), lambda b, i: (b, jnp.minimum(i, n_first - 1), 0)),
                pl.BlockSpec((1, tm, d), lambda b, i: (b, jnp.maximum(i - n_first, 0), 0)),
                pl.BlockSpec((1, d), lambda b, i: (0, 0)),
                pl.BlockSpec((1, 1, 1, d), mod_map), pl.BlockSpec((1, 1, 1, d), mod_map)]
    in_specs += [pl.BlockSpec(w.shape, lambda b, i: (0, 0)) for w in w_in]
    out_specs = [pl.BlockSpec((1, tm, wd), tok) for wd in widths]
    out_specs.append(pl.BlockSpec((1, 1, SUB, widths[0]), lambda b, i: (b, i, 0, 0)))
    out_shape = [jax.ShapeDtypeStruct((bsz, nblk * tm, wd), dt) for wd, dt in zip(widths, out_dtypes)]
    out_shape.append(jax.ShapeDtypeStruct((bsz, nblk, SUB, widths[0]), F32))
    return pl.pallas_call(
        functools.partial(_in_proj_kernel, n_w=len(w_in), n_first=n_first),
        grid=(bsz, nblk),
        in_specs=in_specs, out_specs=out_specs, out_shape=out_shape,
        compiler_params=_cparams("parallel", "arbitrary"),
    )(ctx, x, nw.reshape(1, d), sc, sh, *w_in)


def _prep_kernel(qkv_ref, prev_ref, next_ref, cw_ref, bd_ref, bq_ref, bk_ref, cos_ref, sin_ref,
                 q_out, k_out, v_out, bqr_out, bkr_out):
    x = qkv_ref[0]
    tm = x.shape[0]
    row = lax.broadcasted_iota(I32, x.shape, 0)
    xm = jnp.where(row == 0, prev_ref[0, 0], pltpu.roll(x, 1, 0))
    xp = jnp.where(row == tm - 1, next_ref[0, 0], pltpu.roll(x, tm - 1, 0))
    cw = cw_ref[...]
    y = _silu(xm * cw[0:1] + x * cw[1:2] + xp * cw[2:3])
    bd = bd_ref[...]

    def l2(t):
        return t * lax.rsqrt(_mm_lhs2(t * t, bd) + EPS)

    q_out[0] = l2(y[:, :A_W]) * (A_DK ** -0.5)
    k_out[0] = l2(y[:, A_W:2 * A_W])
    v_out[0] = y[:, 2 * A_W:]

    cos = jnp.concatenate([cos_ref[...]] * (B_W // 128), axis=1)
    sin = jnp.concatenate([sin_ref[...]] * (B_W // 128), axis=1)
    lane = lax.broadcasted_iota(I32, (tm, B_W), 1)
    first = (lane & (B_DH - 1)) < (B_DH // 2)

    def rope(t):
        rot = jnp.where(first, pltpu.roll(t, B_W - B_DH // 2, 1), pltpu.roll(t, B_DH // 2, 1))
        return t * cos + rot * sin

    bqr_out[0] = (rope(bq_ref[0]) * (B_DH ** -0.5)).astype(BF16)
    bkr_out[0] = rope(bk_ref[0]).astype(BF16)


def _prep(qkv, prev_rows, next_rows, conv_w, bq, bk, cos128, sin128, *, tm, n_ctx):
    bsz, n, cw = qkv.shape
    tok = lambda b, i: (b, i, 0)
    lat = lambda b, i: (b, jnp.maximum(i - n_ctx // tm, 0), 0)
    halo = pl.BlockSpec((1, 1, 1, cw), lambda b, i: (b, i, 0, 0))
    return pl.pallas_call(
        _prep_kernel,
        grid=(bsz, n // tm),
        in_specs=[pl.BlockSpec((1, tm, cw), tok), halo, halo,
                  pl.BlockSpec(conv_w.shape, lambda b, i: (0, 0)),
                  pl.BlockSpec((A_W, A_W), lambda b, i: (0, 0)),
                  pl.BlockSpec((1, tm, B_W), tok), pl.BlockSpec((1, tm, B_W), tok),
                  pl.BlockSpec((tm, 128), lambda b, i: (i, 0)),
                  pl.BlockSpec((tm, 128), lambda b, i: (i, 0))],
        out_specs=[pl.BlockSpec((1, tm, A_W), tok)] * 3
        + [pl.BlockSpec((1, tm, B_W), lat), pl.BlockSpec((1, tm, B_W), tok)],
        out_shape=[jax.ShapeDtypeStruct((bsz, n, A_W), F32)] * 3
        + [jax.ShapeDtypeStruct((bsz, n - n_ctx, B_W), BF16), jax.ShapeDtypeStruct((bsz, n, B_W), BF16)],
        compiler_params=_cparams("parallel", "arbitrary"),
    )(qkv, prev_rows, next_rows, conv_w, _block_ones(A_W, A_DK), bq, bk, cos128, sin128)


def _gdn_kernel(q_ref, k_ref, v_ref, ba_ref, bat_ref, alr_ref, dtr_ref, alc_ref, dtc_ref,
                lm_ref, lmt_ref, ltot_ref, e_ref, o_ref, s_ref, *, d, inv_passes):
    rev = d == 1
    nh = A_HEADS

    @pl.when(pl.program_id(1) == 0)
    def _():
        s_ref[...] = jnp.zeros_like(s_ref)

    q, k, v = q_ref[0], k_ref[0], v_ref[0]
    ba, bat = ba_ref[0], bat_ref[0]
    n = q.shape[0]
    beta_c = jax.nn.sigmoid(ba[:, d * nh:(d + 1) * nh])
    g_c = -jnp.exp(alr_ref[...]) * _softplus(ba[:, 2 * nh + d * nh:2 * nh + (d + 1) * nh] + dtr_ref[...])
    g_t = -jnp.exp(alc_ref[...]) * _softplus(bat[2 * nh + d * nh:2 * nh + (d + 1) * nh, :] + dtc_ref[...])
    lm, ltot, e = lm_ref[...], ltot_ref[...], e_ref[...]
    gcum_c = _dot_f32(lm, g_c)
    gcum_t = _dot_f32(g_t, lmt_ref[...])
    gcum_e = _dot_f32(gcum_c, e)
    gtot_e = _dot_f32(_dot_f32(ltot, g_c), e)
    beta_e = _dot_f32(beta_c, e)
    eg = jnp.exp(gcum_e)
    kb = k * beta_e
    vb = v * beta_e
    kbe = kb * eg
    qd = q * eg
    ktail = k * jnp.exp(gtot_e - gcum_e)
    egt = jnp.exp(gtot_e)

    ri = lax.broadcasted_iota(I32, (n, n), 0)
    ci = lax.broadcasted_iota(I32, (n, n), 1)
    eye = (ri == ci).astype(F32)
    same = lambda s: (ri >> s) == (ci >> s)
    d16 = same(4).astype(F32)
    o32 = (same(5) & ~same(4)).astype(F32)
    o64 = (same(6) & ~same(5)).astype(F32)
    incl = lm > 0.0
    strict = lm * (1.0 - eye)

    order = range(n // A_CHUNK - 1, -1, -1) if rev else range(n // A_CHUNK)
    sls = [slice(h * A_DK, (h + 1) * A_DK) for h in range(nh)]
    heads = [None] * nh
    for g0 in range(0, nh, GDN_HEAD_GROUP):
        hs = range(g0, g0 + GDN_HEAD_GROUP)
        decay = {h: jnp.exp(jnp.where(incl, gcum_c[:, h:h + 1] - gcum_t[h:h + 1, :], -1e30)) for h in hs}
        a = {h: _mm(kb[:, sls[h]], k[:, sls[h]], 1, _dot_nt) * decay[h] * strict for h in hs}
        p = {h: -(a[h] * d16) for h in hs}
        x = {h: eye + p[h] for h in hs}
        for _ in range(3):
            p = {h: _mm(p[h], p[h], inv_passes) for h in hs}
            x = {h: x[h] + _mm(x[h], p[h], inv_passes) for h in hs}
        for om in (o32, o64):
            y = {h: _mm(x[h], a[h] * om, inv_passes) for h in hs}
            x = {h: x[h] - _mm(y[h], x[h], inv_passes) for h in hs}
        uw = {h: _mm(x[h], jnp.concatenate([vb[:, sls[h]], kbe[:, sls[h]]], axis=1)) for h in hs}
        attn = {h: _mm(q[:, sls[h]], k[:, sls[h]], 1, _dot_nt) * decay[h] for h in hs}
        s = {h: s_ref[h] for h in hs}
        parts = {h: {} for h in hs}
        for c in order:
            r = slice(c * A_CHUNK, (c + 1) * A_CHUNK)
            v_new = {h: uw[h][r, :A_DV] - _mm(uw[h][r, A_DV:], s[h]) for h in hs}
            for h in hs:
                parts[h][c] = _mm(qd[r, sls[h]], s[h]) + _mm(attn[h][r, r], v_new[h])
            s = {h: s[h] * egt[c * A_CHUNK:c * A_CHUNK + 1, sls[h]]
                 + _mm(ktail[r, sls[h]], v_new[h], 1, _dot_tn) for h in hs}
        for h in hs:
            s_ref[h] = s[h]
            heads[h] = jnp.concatenate([parts[h][c] for c in range(n // A_CHUNK)], axis=0)
    o_ref[0] = jnp.concatenate(heads, axis=1)


def _gdn_constants(d):
    n = GDN_BLOCK
    i = np.arange(n)
    same = (i[:, None] // A_CHUNK) == (i[None, :] // A_CHUNK)
    tri = (i[None, :] >= i[:, None]) if d == 1 else (i[None, :] <= i[:, None])
    lm = (same & tri).astype(np.float32)
    e = (np.arange(A_W)[None, :] // A_DK == np.arange(A_HEADS)[:, None]).astype(np.float32)
    return jnp.asarray(lm), jnp.asarray(lm.T.copy()), jnp.asarray(same.astype(np.float32)), jnp.asarray(e)


def _gdn(q, k, v, ba, bat, a_log, dt_bias, *, d, n_ctx, inv_passes):
    bsz, n, _ = q.shape
    nb, nc = n // GDN_BLOCK, n_ctx // GDN_BLOCK
    blk = (lambda s: jnp.where(s < nc, nc - 1 - s, nb - 1 + nc - s)) if d == 1 else (lambda s: s)
    tok = lambda b, s: (b, blk(s), 0)
    full = lambda shape: pl.BlockSpec(shape, lambda b, s: (0,) * len(shape))
    lm, lmt, ltot, e = _gdn_constants(d)
    return pl.pallas_call(
        functools.partial(_gdn_kernel, d=d, inv_passes=inv_passes),
        grid=(bsz, nb),
        in_specs=[pl.BlockSpec((1, GDN_BLOCK, A_W), tok)] * 3
        + [pl.BlockSpec((1, GDN_BLOCK, ba.shape[2]), tok),
           pl.BlockSpec((1, bat.shape[1], GDN_BLOCK), lambda b, s: (b, 0, blk(s))),
           full((1, A_HEADS)), full((1, A_HEADS)), full((A_HEADS, 1)), full((A_HEADS, 1)),
           full(lm.shape), full(lm.shape), full(lm.shape), full(e.shape)],
        out_specs=pl.BlockSpec((1, GDN_BLOCK, A_W), tok),
        out_shape=jax.ShapeDtypeStruct((bsz, n, A_W), F32),
        scratch_shapes=[pltpu.VMEM((A_HEADS, A_DK, A_DV), F32)],
        compiler_params=_cparams("parallel", "arbitrary"),
    )(q, k, v, ba, bat, a_log[d].reshape(1, -1), dt_bias[d].reshape(1, -1),
      a_log[d].reshape(-1, 1), dt_bias[d].reshape(-1, 1), lm, lmt, ltot, e)


def _attn_kernel(q_ref, k_ref, v_ref, lq1_ref, lk1_ref, lq2_ref, lk2_ref, o_ref, *, lam_init, tk):
    q = q_ref[0]
    tq = q.shape[0]
    nk = k_ref.shape[1]
    lam = (jnp.exp(jnp.sum(lq1_ref[...] * lk1_ref[...], keepdims=True))
           - jnp.exp(jnp.sum(lq2_ref[...] * lk2_ref[...], keepdims=True)) + lam_init)
    def over_keys(op, t):
        return op(op(t.reshape(tk // ATTN_SLAB, ATTN_SLAB, tq), axis=0), axis=0, keepdims=True)

    qs = [q[:, m * B_DH:(m + 1) * B_DH] for m in range(2)]
    mx = [jnp.full((1, tq), -jnp.inf, F32)] * 2
    l = [jnp.zeros((1, tq), F32)] * 2
    acc = [jnp.zeros((2 * B_DH, tq), F32)] * 2
    for c in range(nk // tk):
        ks = slice(c * tk, (c + 1) * tk)
        s = [_dot_nt(k_ref[0, ks, m * B_DH:(m + 1) * B_DH], qs[m]) for m in range(2)]
        m_new = [jnp.maximum(mx[m], over_keys(jnp.max, s[m])) for m in range(2)]
        alpha = [jnp.exp(mx[m] - m_new[m]) for m in range(2)]
        p = [jnp.exp(s[m] - m_new[m]) for m in range(2)]
        l = [alpha[m] * l[m] + over_keys(jnp.sum, p[m]) for m in range(2)]
        acc = [alpha[m] * acc[m] + _dot(v_ref[0, :, ks], p[m].astype(BF16)) for m in range(2)]
        mx = m_new
    o_ref[0] = (acc[0] / l[0] - lam * (acc[1] / l[1])).T


def _diff_attention(bqr, bkr, bv, lam_params, *, lam_init, tq, tk):
    bsz, n_lat, _ = bqr.shape
    n = bkr.shape[1]
    hw = 2 * B_DH
    assert n_lat % tq == 0 and n % tk == 0 and tk % ATTN_SLAB == 0
    lam_spec = pl.BlockSpec((1, B_DH), lambda b, h, i: (0, 0))
    return pl.pallas_call(
        functools.partial(_attn_kernel, lam_init=lam_init, tk=tk),
        grid=(bsz, B_HEADS, n_lat // tq),
        in_specs=[pl.BlockSpec((1, tq, hw), lambda b, h, i: (b, i, h)),
                  pl.BlockSpec((1, n, hw), lambda b, h, i: (b, 0, h)),
                  pl.BlockSpec((1, hw, n), lambda b, h, i: (b, h, 0))] + [lam_spec] * 4,
        out_specs=pl.BlockSpec((1, tq, hw), lambda b, h, i: (b, i, h)),
        out_shape=jax.ShapeDtypeStruct((bsz, n_lat, B_W), F32),
        compiler_params=_cparams("parallel", "parallel", "arbitrary"),
    )(bqr, bkr, bv, *[p.reshape(1, B_DH) for p in lam_params])


def _outproj_kernel(of_ref, ob_ref, gate_ref, d_ref, x_ref, g1_ref, gw_ref, sw_ref, bd64_ref,
                    bd128_ref, w_ref, o_ref, *, lam_init):
    o = of_ref[0] + ob_ref[0]
    ms = _mm_lhs2(o * o, bd64_ref[...]) * (1.0 / A_DV)
    y1 = (o * lax.rsqrt(ms + EPS) * gw_ref[...]) * _silu(gate_ref[0])
    dd = d_ref[0]
    ms2 = _mm_lhs2(dd * dd, bd128_ref[...]) * (1.0 / (2 * B_DH))
    y2 = (dd * lax.rsqrt(ms2 + EPS) * sw_ref[...]) * (1.0 - lam_init)
    ycat = jnp.concatenate([y1, y2], axis=1).astype(BF16)
    o_ref[0] = x_ref[0] + g1_ref[0] * _dot(ycat, w_ref[...])


def _outproj(o_f, o_b, gate, d_lat, x, g1, gdn_norm_w, subln_w, w_out, *, lam_init, n_ctx, tm):
    bsz, n_lat, d = x.shape
    off = n_ctx // tm
    tok = lambda b, i: (b, i, 0)
    tok_off = lambda b, i: (b, i + off, 0)
    full = lambda shape: pl.BlockSpec(shape, lambda b, i: (0,) * len(shape))
    return pl.pallas_call(
        functools.partial(_outproj_kernel, lam_init=lam_init),
        grid=(bsz, n_lat // tm),
        in_specs=[pl.BlockSpec((1, tm, A_W), tok_off)] * 3
        + [pl.BlockSpec((1, tm, B_W), tok), pl.BlockSpec((1, tm, d), tok),
           pl.BlockSpec((1, 1, d), lambda b, i: (b, 0, 0)),
           full((1, A_W)), full((1, B_W)), full((A_W, A_W)), full((B_W, B_W)), full(w_out.shape)],
        out_specs=pl.BlockSpec((1, tm, d), tok),
        out_shape=jax.ShapeDtypeStruct((bsz, n_lat, d), F32),
        compiler_params=_cparams("parallel", "arbitrary"),
    )(o_f, o_b, gate, d_lat, x, g1, jnp.tile(gdn_norm_w, A_HEADS).reshape(1, A_W),
      jnp.tile(subln_w, B_HEADS).reshape(1, B_W), _block_ones(A_W, A_DV), _block_ones(B_W, 2 * B_DH),
      w_out.astype(BF16))


def _dft1_kernel(x_ref, nw_ref, sc_ref, sh_ref, f_ref, twr_ref, twi_ref, re_ref, im_ref, *, d, passes):
    n1 = f_ref.shape[1]
    f = f_ref[...]
    for t in range(re_ref.shape[2]):
        h = _normmod(x_ref[0, :, t * d:(t + 1) * d], nw_ref[...], sc_ref[0], sh_ref[0])
        a = _mm(f, h, passes)
        a_re, a_im = a[:n1], a[n1:]
        twr = jnp.concatenate([twr_ref[:, t * 128:(t + 1) * 128]] * (d // 128), axis=1)
        twi = jnp.concatenate([twi_ref[:, t * 128:(t + 1) * 128]] * (d // 128), axis=1)
        re_ref[0, :, t, :] = a_re * twr - a_im * twi
        im_ref[0, :, t, :] = a_re * twi + a_im * twr


def _dft2_kernel(re_ref, im_ref, x_ref, g1_ref, f_ref, cc_ref, sc_ref, w_ref, o_ref, *, d, passes):
    n2 = f_ref.shape[1]
    f = f_ref[...]
    dg = d // C_GROUPS
    for t in range(re_ref.shape[1]):
        pr = _mm(f, re_ref[0, t], passes)
        pi = _mm(f, im_ref[0, t], passes)
        x_re = pr[:n2] + pi[n2:]
        x_im = pi[:n2] - pr[n2:]
        y = jnp.concatenate(
            [_mm(x_re[:, g * dg:(g + 1) * dg], cc_ref[...], passes)
             + _mm(x_im[:, g * dg:(g + 1) * dg], sc_ref[...], passes) for g in range(C_GROUPS)], axis=1)
        o_ref[0, :, t, :] = x_ref[0, :, t, :] + g1_ref[0] * _dot(y.astype(BF16), w_ref[...])


def _fourier_layer(x, nw, sc, sh, g1, w_out, *, passes=3, nb=8, kb=8):
    bsz, n, d = x.shape
    n1, n2 = DFT_N1, n // DFT_N1
    dg = d // C_GROUPS
    k1 = np.arange(n1)
    ang1 = 2 * np.pi * np.outer(k1, k1) / n1
    f1 = jnp.asarray(np.concatenate([np.cos(ang1), -np.sin(ang1)], 0), F32)
    angt = 2 * np.pi * np.outer(k1, np.arange(n2)) / n
    twr = jnp.asarray(np.repeat(np.cos(angt), 128, axis=1), F32)
    twi = jnp.asarray(np.repeat(-np.sin(angt), 128, axis=1), F32)
    k2 = np.arange(n2)
    ang2 = 2 * np.pi * np.outer(k2, k2) / n2
    f2 = jnp.asarray(np.concatenate([np.cos(ang2), np.sin(ang2)], 0), F32)
    angc = 2 * np.pi * np.outer(np.arange(dg), np.arange(dg)) / dg
    scale = 1.0 / math.sqrt(n * dg)
    cc = jnp.asarray(np.cos(angc) * scale, F32)
    sn = jnp.asarray(np.sin(angc) * scale, F32)

    assert n2 % SUB == 0 and nb % SUB == 0 and kb % SUB == 0
    win1 = pl.BlockSpec((1, n1, nb, d), lambda b, j: (b, 0, j, 0))
    vec = pl.BlockSpec((1, 1, d), lambda b, j: (b, 0, 0))
    full = lambda shape: pl.BlockSpec(shape, lambda b, j: (0,) * len(shape))
    a_re, a_im = pl.pallas_call(
        functools.partial(_dft1_kernel, d=d, passes=passes),
        grid=(bsz, n2 // nb),
        in_specs=[pl.BlockSpec((1, n1, nb * d), lambda b, j: (b, 0, j)), full((1, d)), vec, vec, full(f1.shape),
                  pl.BlockSpec((n1, nb * 128), lambda b, j: (0, j)),
                  pl.BlockSpec((n1, nb * 128), lambda b, j: (0, j))],
        out_specs=[win1] * 2,
        out_shape=[jax.ShapeDtypeStruct((bsz, n1, n2, d), F32)] * 2,
        compiler_params=_cparams("parallel", "arbitrary"),
    )(x.reshape(bsz, n1, n2 * d), nw.reshape(1, d), sc, sh, f1, twr, twi)

    win2 = pl.BlockSpec((1, n2, kb, d), lambda b, j: (b, 0, j, 0))
    out = pl.pallas_call(
        functools.partial(_dft2_kernel, d=d, passes=passes),
        grid=(bsz, n1 // kb),
        in_specs=[pl.BlockSpec((1, kb, n2, d), lambda b, j: (b, j, 0, 0))] * 2
        + [win2, vec, full(f2.shape), full(cc.shape), full(sn.shape), full(w_out.shape)],
        out_specs=win2,
        out_shape=jax.ShapeDtypeStruct((bsz, n2, n1, d), F32),
        compiler_params=_cparams("parallel", "arbitrary"),
    )(a_re, a_im, x.reshape(bsz, n2, n1, d), g1, f2, cc, sn, w_out.astype(BF16))
    return out.reshape(bsz, n, d)


def _top16(s, payload=None):
    rows = lax.broadcasted_iota(I32, s.shape, 0).astype(F32)
    big = float(s.shape[0])
    vals, idxs = [], []
    for _ in range(PEER_TOPK):
        m = jnp.max(s, axis=0, keepdims=True)
        am = jnp.min(jnp.where(s == m, rows, big), axis=0, keepdims=True)
        sel = rows == am
        vals.append(m)
        if payload is None:
            idxs.append(am)
        else:
            idxs.append(jnp.max(jnp.where(sel, payload, -1.0), axis=0, keepdims=True))
        s = jnp.where(sel, -jnp.inf, s)
    return jnp.concatenate(vals, axis=0), jnp.concatenate(idxs, axis=0)


def _staircase_candidates(s1, i1, s2, i2):
    k = PEER_TOPK
    sub = lax.broadcasted_iota(I32, (SUB,) + s1.shape[1:], 0)
    cs = [s1[0:1] + s2, s1[1:2] + s2[:SUB]]
    ci = [i1[0:1] * PEER_NKEYS + i2, i1[1:2] * PEER_NKEYS + i2[:SUB]]
    for a in range(2, SUB):
        keep = sub < k // (a + 1)
        cs.append(jnp.where(keep, s1[a:a + 1] + s2[:SUB], -jnp.inf))
        ci.append(jnp.where(keep, i1[a:a + 1] * PEER_NKEYS + i2[:SUB], -1.0))
    cs.append(s1[SUB:] + s2[0:1])
    ci.append(i1[SUB:] * PEER_NKEYS + i2[0:1])
    return jnp.concatenate(cs, axis=0), jnp.concatenate(ci, axis=0)


def _peer_topk_kernel(x_ref, nw_ref, sc_ref, sh_ref, wh_ref, wl_ref, keys_ref,
                      h_out, row_out, par_out, gate_out):
    h = _normmod(x_ref[0], nw_ref[...], sc_ref[0], sh_ref[0])
    h_out[0] = h
    hh, hl = _split(h)
    q = _dot(hh, wh_ref[...]) + (_dot(hh, wl_ref[...]) + _dot(hl, wh_ref[...]))
    half = PEER_DKEY // 2
    idx_rows, gate_rows = [], []
    for hd in range(PEER_HEADS):
        tops = []
        for p in range(2):
            j = hd * 2 + p
            st = _mm(keys_ref[j], q[:, j * half:(j + 1) * half], 3, _dot_nt)
            tops.append(_top16(st))
        (s1, i1), (s2, i2) = tops
        top_s, top_i = _top16(*_staircase_candidates(s1, i1, s2, i2))
        ex = jnp.exp(top_s - top_s[0:1])
        gate_rows.append(ex / jnp.sum(ex, axis=0, keepdims=True))
        idx_rows.append(top_i)
    idx = jnp.concatenate(idx_rows, axis=0).T.astype(I32)
    n_pairs = PEER_NKEYS * PEER_NKEYS // 2
    row_out[0] = (idx & (n_pairs - 1)) * SUB
    par_out[0] = (idx >= n_pairs).astype(I32)
    gate_out[0] = jnp.concatenate(gate_rows, axis=0).T


def _peer_topk(x, nw, sc, sh, w_q, keys, *, tb):
    bsz, n, d = x.shape
    nk = PEER_HEADS * PEER_TOPK
    wh, wl = _split_outside(w_q)
    keys2 = keys.reshape(PEER_HEADS * 2, PEER_NKEYS, PEER_DKEY // 2)
    tok = lambda b, i: (b, i, 0)
    vec = pl.BlockSpec((1, 1, d), lambda b, i: (b, 0, 0))
    full = lambda shape: pl.BlockSpec(shape, lambda b, i: (0,) * len(shape))
    return pl.pallas_call(
        _peer_topk_kernel,
        grid=(bsz, n // tb),
        in_specs=[pl.BlockSpec((1, tb, d), tok), full((1, d)), vec, vec, full(wh.shape), full(wl.shape),
                  full(keys2.shape)],
        out_specs=[pl.BlockSpec((1, tb, d), tok)] + [pl.BlockSpec((1, tb, nk), tok)] * 3,
        out_shape=[jax.ShapeDtypeStruct((bsz, n, d), F32), jax.ShapeDtypeStruct((bsz, n, nk), I32),
                   jax.ShapeDtypeStruct((bsz, n, nk), I32), jax.ShapeDtypeStruct((bsz, n, nk), F32)],
        compiler_params=_cparams("parallel", "arbitrary"),
    )(x, nw.reshape(1, d), sc, sh, wh, wl, keys2)


def _pack_kernel(hi_ref, lo_ref, o_ref):
    pairs, d = hi_ref.shape
    bits = lambda v: lax.bitcast_convert_type(v.astype(BF16).astype(F32), U32)
    packed = bits(hi_ref[...]) | (bits(lo_ref[...]) >> 16)
    for s in range(d // 128):
        o_ref[pl.ds(s, pairs, stride=SUB), :] = packed[:, s * 128:(s + 1) * 128]


def _pack_table(tab, *, pairs=256):
    e, d = tab.shape
    assert d // 128 == SUB
    nblk = e // 2 // pairs
    return pl.pallas_call(
        _pack_kernel,
        grid=(nblk,),
        in_specs=[pl.BlockSpec((pairs, d), lambda i: (i, 0)), pl.BlockSpec((pairs, d), lambda i: (i + nblk, 0))],
        out_specs=pl.BlockSpec((pairs * SUB, 128), lambda i: (i, 0)),
        out_shape=jax.ShapeDtypeStruct((e // 2 * SUB, 128), U32),
        compiler_params=_cparams("parallel"),
    )(tab, tab)


def _gather_tiles(tab_ref, row_ref, t, nk):
    tiles = [tab_ref[pl.ds(pl.multiple_of(row_ref[t, kk], SUB), SUB), :] for kk in range(nk)]
    return pltpu.bitcast(jnp.concatenate(tiles, axis=0), BF16)


def _gather_constants(nk):
    c = np.arange(nk * PAIR_COLS)
    pair = (c[None, :] // PAIR_COLS == np.arange(nk)[:, None])
    hi = (c % 2 == 1)[None, :]
    smask = ((c[None, :] % PAIR_COLS) // 2 == np.arange(SUB)[:, None]).astype(np.float32)
    as_bf16 = lambda m: jnp.asarray(m.astype(np.float32), dtype=BF16)
    return dict(e_all=as_bf16(pair), e_hi=as_bf16(pair & hi), e_lo=as_bf16(pair & ~hi),
                collapse=as_bf16(pair.T), smask=jnp.asarray(smask), hi_row=jnp.asarray(hi.astype(np.float32)))


def _gelu_tanh(x):
    return 0.5 * x * (1.0 + jnp.tanh(math.sqrt(2.0 / math.pi) * (x + 0.044715 * (x * x * x))))


def _peer_act_kernel(row_ref, par_ref, h_ref, gate_ref, tab_ref, eall_ref, coll_ref, smask_ref, hi_ref,
                     w_out, rsum_ref, h8_ref):
    tb, nk = gate_ref.shape
    smask = smask_ref[...]
    for s in range(SUB):
        h8_ref[pl.ds(s, tb, stride=SUB), :] = h_ref[:, s * 128:(s + 1) * 128]

    def token(t, carry):
        g = _gather_tiles(tab_ref, row_ref, t, nk)
        hh, hl = _split(h8_ref[pl.ds(pl.multiple_of(t * SUB, SUB), SUB), :])
        r = _dot_nt(jnp.concatenate([hh, hl], axis=0), g)
        rsum_ref[pl.ds(t, 1), :] = jnp.sum((r[:SUB] + r[SUB:]) * smask, axis=0, keepdims=True)
        return carry

    lax.fori_loop(0, tb, token, 0, unroll=2 * GATHER_UNROLL)
    want_hi = 1.0 - _dot(par_ref[...].astype(F32).astype(BF16), eall_ref[...])
    picked = jnp.where(want_hi == hi_ref[...], rsum_ref[...], 0.0)
    w_out[...] = gate_ref[...] * _gelu_tanh(_mm_lhs2(picked, coll_ref[...]))


def _peer_mix_kernel(row_ref, par_ref, w_ref, x_ref, g2_ref, tab_ref, ehi_ref, elo_ref, smask_ref,
                     o_ref, wcols_ref, acc_ref):
    tb, nk = w_ref.shape
    smask = smask_ref[...]
    par = par_ref[...].astype(F32)
    w = w_ref[...]
    wcols_ref[...] = _mm_lhs2(w * (1.0 - par), ehi_ref[...]) + _mm_lhs2(w * par, elo_ref[...])

    def token(t, carry):
        g = _gather_tiles(tab_ref, row_ref, t, nk)
        wh, wl = _split(wcols_ref[pl.ds(t, 1), :] * smask)
        res = _dot(jnp.concatenate([wh, wl], axis=0), g)
        acc_ref[pl.ds(pl.multiple_of(t * SUB, SUB), SUB), :] = res[:SUB] + res[SUB:]
        return carry

    lax.fori_loop(0, tb, token, 0, unroll=GATHER_UNROLL)
    for s in range(SUB):
        cs = slice(s * 128, (s + 1) * 128)
        o_ref[:, cs] = x_ref[:, cs] + g2_ref[0][:, cs] * acc_ref[pl.ds(s, tb, stride=SUB), :]


def _peer_gather(x, h, row, par, gate, g2, u_tab, v_tab, *, tb):
    bsz, n, d = x.shape
    t_all = bsz * n
    nk = row.shape[-1]
    sl = d // 128
    assert sl == SUB
    row2, par2 = row.reshape(t_all, nk), par.reshape(t_all, nk)
    cst = _gather_constants(nk)
    ncol = nk * PAIR_COLS
    tab_spec = pl.BlockSpec((u_tab.shape[0] // 2 * sl, 128), lambda i: (0, 0), pipeline_mode=pl.Buffered(1))
    smem = pl.BlockSpec((tb, nk), lambda i: (i, 0), memory_space=pltpu.SMEM)
    tokk = pl.BlockSpec((tb, nk), lambda i: (i, 0))
    tokd = pl.BlockSpec((tb, d), lambda i: (i, 0))
    tiles = pltpu.VMEM((tb * sl, 128), F32)
    full = lambda a: pl.BlockSpec(a.shape, lambda i: (0,) * a.ndim)
    w = pl.pallas_call(
        _peer_act_kernel,
        grid=(t_all // tb,),
        in_specs=[smem, tokk, tokd, tokk, tab_spec, full(cst["e_all"]), full(cst["collapse"]),
                  full(cst["smask"]), full(cst["hi_row"])],
        out_specs=tokk,
        out_shape=jax.ShapeDtypeStruct((t_all, nk), F32),
        scratch_shapes=[pltpu.VMEM((tb, ncol), F32), tiles],
        compiler_params=_cparams("arbitrary"),
    )(row2, par2, h.reshape(t_all, d), gate.reshape(t_all, nk), _pack_table(u_tab),
      cst["e_all"], cst["collapse"], cst["smask"], cst["hi_row"])
    out = pl.pallas_call(
        _peer_mix_kernel,
        grid=(t_all // tb,),
        in_specs=[smem, tokk, tokk, tokd, pl.BlockSpec((1, 1, d), lambda i: ((i * tb) // n, 0, 0)),
                  tab_spec, full(cst["e_hi"]), full(cst["e_lo"]), full(cst["smask"])],
        out_specs=tokd,
        out_shape=jax.ShapeDtypeStruct((t_all, d), F32),
        scratch_shapes=[pltpu.VMEM((tb, ncol), F32), tiles],
        compiler_params=_cparams("arbitrary"),
    )(row2, par2, w, x.reshape(t_all, d), g2, _pack_table(v_tab),
      cst["e_hi"], cst["e_lo"], cst["smask"])
    return out.reshape(bsz, n, d)


def _peer_layer(x, nw, sc, sh, g2, w_q, keys, u_tab, v_tab):
    h, row, par, gate = _peer_topk(x, nw, sc, sh, w_q, keys, tb=128)
    return _peer_gather(x, h, row, par, gate, g2, u_tab, v_tab, tb=128)


def _final_norm_kernel(x_ref, w_ref, o_ref):
    x = x_ref[0]
    o_ref[0] = x * lax.rsqrt(jnp.mean(x * x, axis=-1, keepdims=True) + EPS) * w_ref[...]


def _final_norm(x, w, *, tm):
    bsz, n, d = x.shape
    return pl.pallas_call(
        _final_norm_kernel,
        grid=(bsz, n // tm),
        in_specs=[pl.BlockSpec((1, tm, d), lambda b, i: (b, i, 0)), pl.BlockSpec((1, d), lambda b, i: (0, 0))],
        out_specs=pl.BlockSpec((1, tm, d), lambda b, i: (b, i, 0)),
        out_shape=jax.ShapeDtypeStruct((bsz, n, d), F32),
        compiler_params=_cparams("parallel", "arbitrary"),
    )(x, w.reshape(1, d))


def _rope_tables(n_ctx, n_lat):
    pos = np.arange(n_lat)
    axis_dim = B_DH // 2
    inv = ROPE_BASE ** (-np.arange(0, axis_dim, 2, dtype=np.float32) / axis_dim)
    ang = np.concatenate([(pos // GRID_W)[:, None] * inv, (pos % GRID_W)[:, None] * inv], axis=-1)
    ang = np.concatenate([np.zeros((n_ctx, ang.shape[1])), ang], axis=0).astype(np.float32)
    cos = np.cos(ang)
    sin = np.sin(ang)
    cos64 = np.concatenate([cos, cos], axis=1)
    sin64 = np.concatenate([-sin, sin], axis=1)
    return (jnp.asarray(np.tile(cos64, (1, 2)), F32), jnp.asarray(np.tile(sin64, (1, 2)), F32))


def _mixer_ab_layer(x, ctx, mod, mod_c, layer, norm1_w, w_in, conv_w, a_log, dt_bias, gdn_norm_w,
                    lam_params, subln_w, w_out):
    bsz, n_lat, d = x.shape
    n_ctx = ctx.shape[1]
    n = n_ctx + n_lat
    tm = GDN_BLOCK
    lam_init = 0.8 - 0.6 * math.exp(-0.3 * layer)

    assert n_ctx % tm == 0 and n_lat % tm == 0
    seg = lambda j: jnp.stack([jnp.broadcast_to(mod_c[j], (bsz, d)), mod[:, j]], axis=1)[:, :, None, :]
    cuts = np.cumsum([0, 3 * A_W, A_W, 4 * A_HEADS, B_W, B_W, B_W])
    weights = [w_in[:, a:b] for a, b in zip(cuts[:-1], cuts[1:])]
    qkv, gate, ba, bq, bk, bv, edges = _in_proj(
        ctx, x, norm1_w, seg(1), seg(0), weights, [F32, F32, F32, F32, F32, BF16], tm=tm)

    first, last = edges[:, :, 0], edges[:, :, 1]
    zero = jnp.zeros_like(last[:, :1])
    prev_rows = jnp.concatenate([zero, last[:, :-1]], axis=1)
    next_rows = jnp.concatenate([first[:, 1:], zero], axis=1)
    blk = jnp.arange(n // tm)[None, :, None]
    prev_rows = jnp.where(blk == n_ctx // tm, 0.0, prev_rows)[:, :, None, :]
    next_rows = jnp.where(blk == n_ctx // tm - 1, 0.0, next_rows)[:, :, None, :]
    cos128, sin128 = _rope_tables(n_ctx, n_lat)
    q, k, v, bqr, bkr = _prep(qkv, prev_rows, next_rows, conv_w, bq, bk, cos128, sin128, tm=tm, n_ctx=n_ctx)

    bat = jnp.swapaxes(ba, 1, 2)
    o_f = _gdn(q, k, v, ba, bat, a_log, dt_bias, d=0, n_ctx=n_ctx, inv_passes=1)
    o_b = _gdn(q, k, v, ba, bat, a_log, dt_bias, d=1, n_ctx=n_ctx, inv_passes=1)
    d_lat = _diff_attention(bqr, bkr, jnp.swapaxes(bv, 1, 2), lam_params, lam_init=lam_init, tq=512,
                            tk=n // 6 if n % (6 * 128) == 0 else n)
    return _outproj(o_f, o_b, gate, d_lat, x, mod[:, 2][:, None, :], gdn_norm_w, subln_w, w_out,
                    lam_init=lam_init, n_ctx=n_ctx, tm=tm)


def kernel(x, c, ctx, c_ctx, ada_w, ada_b, norm1_w, norm2_w, w_in, conv_w, a_log, dt_bias, gdn_norm_w,
           lam_q1, lam_k1, lam_q2, lam_k2, subln_w, w_out_ab, w_out_f, peer_wq, peer_keys, peer_u,
           peer_v, final_norm_w):
    bsz, n_lat, d = x.shape
    depth = ada_w.shape[0]
    last_ctx_reader = 2 * ((depth - 1) // 2)
    cmat = jnp.concatenate([c, c_ctx[None, :], jnp.zeros((8 - bsz - 1, d), F32)], axis=0)
    for i in range(depth):
        if i < last_ctx_reader:
            raise NotImplementedError("context stream advance (depth > 2) is not implemented")
        mod_all = _modulation(cmat, ada_w[i], ada_b[i])
        mod = mod_all[:bsz].reshape(bsz, 6, d)
        mod_c = mod_all[bsz].reshape(6, d)
        vec = lambda j: mod[:, j][:, None, :]
        j = i // 2
        if i % 2 == 0:
            x = _mixer_ab_layer(x, ctx, mod, mod_c, i, norm1_w[i], w_in[j], conv_w[j], a_log[j], dt_bias[j],
                                gdn_norm_w[j], (lam_q1[j], lam_k1[j], lam_q2[j], lam_k2[j]), subln_w[j],
                                w_out_ab[j])
        else:
            x = _fourier_layer(x, norm1_w[i], vec(1), vec(0), vec(2), w_out_f[j])
        x = _peer_layer(x, norm2_w[i], vec(4), vec(3), vec(5), peer_wq[i], peer_keys[i], peer_u[i],
                        peer_v[i])
    return _final_norm(x, final_norm_w, tm=512)
```

```python
import functools
import math

import numpy as np
import jax
import jax.numpy as jnp
from jax import lax
from jax.experimental import pallas as pl
from jax.experimental.pallas import tpu as pltpu

F32 = jnp.float32
BF16 = jnp.bfloat16
I32 = jnp.int32
U32 = jnp.uint32
HIGHEST = lax.Precision.HIGHEST

EPS = 1e-6
ROPE_BASE = 10000.0
GRID_W = 64
A_HEADS, A_DK, A_DV, A_CHUNK = 8, 64, 64, 64
B_HEADS, B_DH = 4, 64
C_GROUPS = 4
PEER_HEADS, PEER_NKEYS, PEER_DKEY, PEER_TOPK = 8, 128, 256, 16
A_W = A_HEADS * A_DK
B_W = B_HEADS * 2 * B_DH
GDN_BLOCK = 256
GDN_HEAD_GROUP = 8
ATTN_SLAB = 64
DFT_N1 = 128
SUB = 8
PAIR_COLS = 2 * SUB
GATHER_UNROLL = 8

VMEM_LIMIT_BYTES = 48 * 1024 * 1024


def _cparams(*sem):
    return pltpu.CompilerParams(dimension_semantics=sem, vmem_limit_bytes=VMEM_LIMIT_BYTES)


def _dot(a, b):
    return lax.dot_general(a, b, (((1,), (0,)), ((), ())), preferred_element_type=F32)


def _dot_nt(a, b):
    return lax.dot_general(a, b, (((1,), (1,)), ((), ())), preferred_element_type=F32)


def _dot_tn(a, b):
    return lax.dot_general(a, b, (((0,), (0,)), ((), ())), preferred_element_type=F32)


def _dot_f32(a, b):
    return lax.dot_general(a, b, (((1,), (0,)), ((), ())), precision=HIGHEST,
                           preferred_element_type=F32)


def _split(a):
    hi = a.astype(BF16)
    lo = (a - hi.astype(F32)).astype(BF16)
    return hi, lo


def _mm(a, b, passes=1, dot=_dot):
    if passes == 1:
        return dot(a.astype(BF16), b.astype(BF16))
    ah, al = _split(a)
    bh, bl = _split(b)
    return dot(ah, bh) + (dot(ah, bl) + dot(al, bh))


def _split3(a):
    hi = a.astype(BF16)
    r = a - hi.astype(F32)
    mid = r.astype(BF16)
    return hi, mid, (r - mid.astype(F32)).astype(BF16)


def _mm_lhs2(a, b_bf16):
    ah, al = _split(a)
    return _dot(ah, b_bf16) + _dot(al, b_bf16)


def _silu(x):
    return x * jax.nn.sigmoid(x)


def _softplus(x):
    return jnp.maximum(x, 0.0) + jnp.log(1.0 + jnp.exp(-jnp.abs(x)))


def _normmod(x, nw, sc, sh):
    ms = jnp.mean(x * x, axis=-1, keepdims=True)
    return (x * lax.rsqrt(ms + EPS) * nw) * (1.0 + sc) + sh


def _split_outside(w):
    hi = lax.bitcast_convert_type(lax.bitcast_convert_type(w, U32) & jnp.uint32(0xFFFF0000), F32)
    return hi.astype(BF16), (w - hi).astype(BF16)


def _block_ones(n, blk):
    i = np.arange(n) // blk
    return jnp.asarray((i[:, None] == i[None, :]).astype(np.float32), dtype=BF16)


def _mod_kernel(c_ref, w_ref, b_ref, o_ref):
    o_ref[...] = _dot_f32(_silu(c_ref[...]), w_ref[...]) + b_ref[...]


def _modulation(cmat, w, b):
    rows, d = cmat.shape
    n = w.shape[1]
    return pl.pallas_call(
        _mod_kernel,
        grid=(n // d,),
        in_specs=[pl.BlockSpec((rows, d), lambda j: (0, 0)),
                  pl.BlockSpec((d, d), lambda j: (0, j)),
                  pl.BlockSpec((1, d), lambda j: (0, j))],
        out_specs=pl.BlockSpec((rows, d), lambda j: (0, j)),
        out_shape=jax.ShapeDtypeStruct((rows, n), F32),
        compiler_params=_cparams("arbitrary"),
    )(cmat, w, b.reshape(1, n))


def _in_proj_kernel(*refs, n_w, n_first):
    ctx_ref, x_ref, nw_ref, sc_ref, sh_ref = refs[:5]
    w_refs = refs[5:5 + n_w]
    outs = refs[5 + n_w:5 + 2 * n_w]
    edge_ref = refs[5 + 2 * n_w]
    xin = jnp.where(pl.program_id(1) < n_first, ctx_ref[0], x_ref[0])
    hb = _normmod(xin, nw_ref[...], sc_ref[0, 0], sh_ref[0, 0]).astype(BF16)
    for j in range(n_w):
        y = _dot(hb, w_refs[j][...])
        outs[j][0] = y.astype(outs[j].dtype)
        if j == 0:
            tm = y.shape[0]
            edge_ref[0, 0] = jnp.concatenate(
                [y[0:1], y[tm - 1:tm], jnp.zeros((SUB - 2, y.shape[1]), F32)], axis=0)


def _in_proj(ctx, x, nw, sc, sh, weights, out_dtypes, *, tm):
    bsz, n_lat, d = x.shape
    n_first = ctx.shape[1] // tm
    nblk = n_first + n_lat // tm
    widths = tuple(w.shape[1] for w in weights)
    w_in = [w.astype(BF16) for w in weights]
    mod_map = lambda b, i: (b, jnp.where(i < n_first, 0, 1), 0, 0)
    tok = lambda b, i: (b, i, 0)
    in_specs = [pl.BlockSpec((1, tm, d), lambda b, i: (b, jnp.minimum(i, n_first - 1), 0)),
                pl.BlockSpec((1, tm, d), lambda b, i: (b, jnp.maximum(i - n_first, 0), 0)),
                pl.BlockSpec((1, d), lambda b, i: (0, 0)),
                pl.BlockSpec((1, 1, 1, d), mod_map), pl.BlockSpec((1, 1, 1, d), mod_map)]
    in_specs += [pl.BlockSpec(w.shape, lambda b, i: (0, 0)) for w in w_in]
    out_specs = [pl.BlockSpec((1, tm, wd), tok) for wd in widths]
    out_specs.append(pl.BlockSpec((1, 1, SUB, widths[0]), lambda b, i: (b, i, 0, 0)))
    out_shape = [jax.ShapeDtypeStruct((bsz, nblk * tm, wd), dt) for wd, dt in zip(widths, out_dtypes)]
    out_shape.append(jax.ShapeDtypeStruct((bsz, nblk, SUB, widths[0]), F32))
    return pl.pallas_call(
        functools.partial(_in_proj_kernel, n_w=len(w_in), n_first=n_first),
        grid=(bsz, nblk),
        in_specs=in_specs, out_specs=out_specs, out_shape=out_shape,
        compiler_params=_cparams("parallel", "arbitrary"),
    )(ctx, x, nw.reshape(1, d), sc, sh, *w_in)


def _prep_kernel(qkv_ref, prev_ref, next_ref, cw_ref, bd_ref, bq_ref, bk_ref, cos_ref, sin_ref,
                 q_out, k_out, v_out, bqr_out, bkr_out):
    x = qkv_ref[0]
    tm = x.shape[0]
    row = lax.broadcasted_iota(I32, x.shape, 0)
    xm = jnp.where(row == 0, prev_ref[0, 0], pltpu.roll(x, 1, 0))
    xp = jnp.where(row == tm - 1, next_ref[0, 0], pltpu.roll(x, tm - 1, 0))
    cw = cw_ref[...]
    y = _silu(xm * cw[0:1] + x * cw[1:2] + xp * cw[2:3])
    bd = bd_ref[...]

    def l2(t):
        return t * lax.rsqrt(_mm_lhs2(t * t, bd) + EPS)

    q_out[0] = l2(y[:, :A_W]) * (A_DK ** -0.5)
    k_out[0] = l2(y[:, A_W:2 * A_W])
    v_out[0] = y[:, 2 * A_W:]

    cos = jnp.concatenate([cos_ref[...]] * (B_W // 128), axis=1)
    sin = jnp.concatenate([sin_ref[...]] * (B_W // 128), axis=1)
    lane = lax.broadcasted_iota(I32, (tm, B_W), 1)
    first = (lane & (B_DH - 1)) < (B_DH // 2)

    def rope(t):
        rot = jnp.where(first, pltpu.roll(t, B_W - B_DH // 2, 1), pltpu.roll(t, B_DH // 2, 1))
        return t * cos + rot * sin

    bqr_out[0] = (rope(bq_ref[0]) * (B_DH ** -0.5)).astype(BF16)
    bkr_out[0] = rope(bk_ref[0]).astype(BF16)


def _prep(qkv, prev_rows, next_rows, conv_w, bq, bk, cos128, sin128, *, tm, n_ctx):
    bsz, n, cw = qkv.shape
    tok = lambda b, i: (b, i, 0)
    lat = lambda b, i: (b, jnp.maximum(i - n_ctx // tm, 0), 0)
    halo = pl.BlockSpec((1, 1, 1, cw), lambda b, i: (b, i, 0, 0))
    return pl.pallas_call(
        _prep_kernel,
        grid=(bsz, n // tm),
        in_specs=[pl.BlockSpec((1, tm, cw), tok), halo, halo,
                  pl.BlockSpec(conv_w.shape, lambda b, i: (0, 0)),
                  pl.BlockSpec((A_W, A_W), lambda b, i: (0, 0)),
                  pl.BlockSpec((1, tm, B_W), tok), pl.BlockSpec((1, tm, B_W), tok),
                  pl.BlockSpec((tm, 128), lambda b, i: (i, 0)),
                  pl.BlockSpec((tm, 128), lambda b, i: (i, 0))],
        out_specs=[pl.BlockSpec((1, tm, A_W), tok)] * 3
        + [pl.BlockSpec((1, tm, B_W), lat), pl.BlockSpec((1, tm, B_W), tok)],
        out_shape=[jax.ShapeDtypeStruct((bsz, n, A_W), F32)] * 3
        + [jax.ShapeDtypeStruct((bsz, n - n_ctx, B_W), BF16), jax.ShapeDtypeStruct((bsz, n, B_W), BF16)],
        compiler_params=_cparams("parallel", "arbitrary"),
    )(qkv, prev_rows, next_rows, conv_w, _block_ones(A_W, A_DK), bq, bk, cos128, sin128)


def _gdn_kernel(q_ref, k_ref, v_ref, ba_ref, bat_ref, alr_ref, dtr_ref, alc_ref, dtc_ref,
                lm_ref, lmt_ref, ltot_ref, e_ref, o_ref, s_ref, *, d, inv_passes):
    rev = d == 1
    nh = A_HEADS

    @pl.when(pl.program_id(1) == 0)
    def _():
        s_ref[...] = jnp.zeros_like(s_ref)

    q, k, v = q_ref[0], k_ref[0], v_ref[0]
    ba, bat = ba_ref[0], bat_ref[0]
    n = q.shape[0]
    beta_c = jax.nn.sigmoid(ba[:, d * nh:(d + 1) * nh])
    g_c = -jnp.exp(alr_ref[...]) * _softplus(ba[:, 2 * nh + d * nh:2 * nh + (d + 1) * nh] + dtr_ref[...])
    g_t = -jnp.exp(alc_ref[...]) * _softplus(bat[2 * nh + d * nh:2 * nh + (d + 1) * nh, :] + dtc_ref[...])
    lm = lm_ref[...]
    gs = _split3(g_c)
    gcum_c = sum(_dot(lm.astype(BF16), p) for p in gs)
    gtot_c = sum(_dot(ltot_ref[...].astype(BF16), p) for p in gs)
    gcum_t = sum(_dot(p, lmt_ref[...].astype(BF16)) for p in _split3(g_t))
    small = jnp.concatenate([jnp.exp(gcum_c), jnp.exp(gtot_c - gcum_c), jnp.exp(gtot_c), beta_c], axis=1)
    wide = _mm_lhs2(small, e_ref[...])
    eg, ekt, egt, beta_e = (wide[:, j * A_W:(j + 1) * A_W] for j in range(4))
    kb = k * beta_e
    vb = v * beta_e
    kbe = kb * eg
    qd = q * eg
    ktail = k * ekt

    ri = lax.broadcasted_iota(I32, (n, n), 0)
    ci = lax.broadcasted_iota(I32, (n, n), 1)
    eye = (ri == ci).astype(F32)
    same = lambda s: (ri >> s) == (ci >> s)
    d16 = same(4).astype(F32)
    o32 = (same(5) & ~same(4)).astype(F32)
    o64 = (same(6) & ~same(5)).astype(F32)
    incl = lm > 0.0
    strict = lm * (1.0 - eye)

    order = range(n // A_CHUNK - 1, -1, -1) if rev else range(n // A_CHUNK)
    sls = [slice(h * A_DK, (h + 1) * A_DK) for h in range(nh)]
    heads = [None] * nh
    for g0 in range(0, nh, GDN_HEAD_GROUP):
        hs = range(g0, g0 + GDN_HEAD_GROUP)
        decay = {h: jnp.exp(jnp.where(incl, gcum_c[:, h:h + 1] - gcum_t[h:h + 1, :], -1e30)) for h in hs}
        a = {h: _mm(kb[:, sls[h]], k[:, sls[h]], 1, _dot_nt) * decay[h] * strict for h in hs}
        p = {h: -(a[h] * d16) for h in hs}
        x = {h: eye + p[h] for h in hs}
        for _ in range(3):
            p = {h: _mm(p[h], p[h], inv_passes) for h in hs}
            x = {h: x[h] + _mm(x[h], p[h], inv_passes) for h in hs}
        for om in (o32, o64):
            y = {h: _mm(x[h], a[h] * om, inv_passes) for h in hs}
            x = {h: x[h] - _mm(y[h], x[h], inv_passes) for h in hs}
        uw = {h: _mm(x[h], jnp.concatenate([vb[:, sls[h]], kbe[:, sls[h]]], axis=1)) for h in hs}
        attn = {h: _mm(q[:, sls[h]], k[:, sls[h]], 1, _dot_nt) * decay[h] for h in hs}
        s = {h: s_ref[h] for h in hs}
        parts = {h: {} for h in hs}
        for c in order:
            r = slice(c * A_CHUNK, (c + 1) * A_CHUNK)
            v_new = {h: uw[h][r, :A_DV] - _mm(uw[h][r, A_DV:], s[h]) for h in hs}
            for h in hs:
                parts[h][c] = _mm(qd[r, sls[h]], s[h]) + _mm(attn[h][r, r], v_new[h])
            s = {h: s[h] * egt[c * A_CHUNK:c * A_CHUNK + 1, sls[h]]
                 + _mm(ktail[r, sls[h]], v_new[h], 1, _dot_tn) for h in hs}
        for h in hs:
            s_ref[h] = s[h]
            heads[h] = jnp.concatenate([parts[h][c] for c in range(n // A_CHUNK)], axis=0)
    o_ref[0] = jnp.concatenate(heads, axis=1)


def _gdn_constants(d):
    n = GDN_BLOCK
    i = np.arange(n)
    same = (i[:, None] // A_CHUNK) == (i[None, :] // A_CHUNK)
    tri = (i[None, :] >= i[:, None]) if d == 1 else (i[None, :] <= i[:, None])
    lm = (same & tri).astype(np.float32)
    e = (np.arange(A_W)[None, :] // A_DK == np.arange(A_HEADS)[:, None]).astype(np.float32)
    e4 = np.kron(np.eye(4, dtype=np.float32), e)
    return (jnp.asarray(lm), jnp.asarray(lm.T.copy()), jnp.asarray(same.astype(np.float32)),
            jnp.asarray(e4, dtype=BF16))


def _gdn(q, k, v, ba, bat, a_log, dt_bias, *, d, n_ctx, inv_passes):
    bsz, n, _ = q.shape
    nb, nc = n // GDN_BLOCK, n_ctx // GDN_BLOCK
    blk = (lambda s: jnp.where(s < nc, nc - 1 - s, nb - 1 + nc - s)) if d == 1 else (lambda s: s)
    tok = lambda b, s: (b, blk(s), 0)
    full = lambda shape: pl.BlockSpec(shape, lambda b, s: (0,) * len(shape))
    lm, lmt, ltot, e = _gdn_constants(d)
    return pl.pallas_call(
        functools.partial(_gdn_kernel, d=d, inv_passes=inv_passes),
        grid=(bsz, nb),
        in_specs=[pl.BlockSpec((1, GDN_BLOCK, A_W), tok)] * 3
        + [pl.BlockSpec((1, GDN_BLOCK, ba.shape[2]), tok),
           pl.BlockSpec((1, bat.shape[1], GDN_BLOCK), lambda b, s: (b, 0, blk(s))),
           full((1, A_HEADS)), full((1, A_HEADS)), full((A_HEADS, 1)), full((A_HEADS, 1)),
           full(lm.shape), full(lm.shape), full(lm.shape), full(e.shape)],
        out_specs=pl.BlockSpec((1, GDN_BLOCK, A_W), tok),
        out_shape=jax.ShapeDtypeStruct((bsz, n, A_W), F32),
        scratch_shapes=[pltpu.VMEM((A_HEADS, A_DK, A_DV), F32)],
        compiler_params=_cparams("parallel", "arbitrary"),
    )(q, k, v, ba, bat, a_log[d].reshape(1, -1), dt_bias[d].reshape(1, -1),
      a_log[d].reshape(-1, 1), dt_bias[d].reshape(-1, 1), lm, lmt, ltot, e)


def _attn_kernel(q_ref, k_ref, v_ref, lq1_ref, lk1_ref, lq2_ref, lk2_ref, o_ref, *, lam_init, tk):
    q = q_ref[0]
    tq = q.shape[0]
    nk = k_ref.shape[1]
    lam = (jnp.exp(jnp.sum(lq1_ref[...] * lk1_ref[...], keepdims=True))
           - jnp.exp(jnp.sum(lq2_ref[...] * lk2_ref[...], keepdims=True)) + lam_init)
    def over_keys(op, t):
        return op(op(t.reshape(tk // ATTN_SLAB, ATTN_SLAB, tq), axis=0), axis=0, keepdims=True)

    qs = [q[:, m * B_DH:(m + 1) * B_DH] for m in range(2)]
    mx = [jnp.full((1, tq), -jnp.inf, F32)] * 2
    l = [jnp.zeros((1, tq), F32)] * 2
    acc = [jnp.zeros((2 * B_DH, tq), F32)] * 2
    for c in range(nk // tk):
        ks = slice(c * tk, (c + 1) * tk)
        s = [_dot_nt(k_ref[0, ks, m * B_DH:(m + 1) * B_DH], qs[m]) for m in range(2)]
        m_new = [jnp.maximum(mx[m], over_keys(jnp.max, s[m])) for m in range(2)]
        alpha = [jnp.exp(mx[m] - m_new[m]) for m in range(2)]
        p = [jnp.exp(s[m] - m_new[m]) for m in range(2)]
        l = [alpha[m] * l[m] + over_keys(jnp.sum, p[m]) for m in range(2)]
        acc = [alpha[m] * acc[m] + _dot(v_ref[0, :, ks], p[m].astype(BF16)) for m in range(2)]
        mx = m_new
    o_ref[0] = (acc[0] / l[0] - lam * (acc[1] / l[1])).T


def _diff_attention(bqr, bkr, bv, lam_params, *, lam_init, tq, tk):
    bsz, n_lat, _ = bqr.shape
    n = bkr.shape[1]
    hw = 2 * B_DH
    assert n_lat % tq == 0 and n % tk == 0 and tk % ATTN_SLAB == 0
    lam_spec = pl.BlockSpec((1, B_DH), lambda b, h, i: (0, 0))
    return pl.pallas_call(
        functools.partial(_attn_kernel, lam_init=lam_init, tk=tk),
        grid=(bsz, B_HEADS, n_lat // tq),
        in_specs=[pl.BlockSpec((1, tq, hw), lambda b, h, i: (b, i, h)),
                  pl.BlockSpec((1, n, hw), lambda b, h, i: (b, 0, h)),
                  pl.BlockSpec((1, hw, n), lambda b, h, i: (b, h, 0))] + [lam_spec] * 4,
        out_specs=pl.BlockSpec((1, tq, hw), lambda b, h, i: (b, i, h)),
        out_shape=jax.ShapeDtypeStruct((bsz, n_lat, B_W), F32),
        compiler_params=_cparams("parallel", "parallel", "arbitrary"),
    )(bqr, bkr, bv, *[p.reshape(1, B_DH) for p in lam_params])


def _outproj_kernel(of_ref, ob_ref, gate_ref, d_ref, x_ref, g1_ref, gw_ref, sw_ref, bd64_ref,
                    bd128_ref, w_ref, o_ref, *, lam_init):
    o = of_ref[0] + ob_ref[0]
    ms = _mm_lhs2(o * o, bd64_ref[...]) * (1.0 / A_DV)
    y1 = (o * lax.rsqrt(ms + EPS) * gw_ref[...]) * _silu(gate_ref[0])
    dd = d_ref[0]
    ms2 = _mm_lhs2(dd * dd, bd128_ref[...]) * (1.0 / (2 * B_DH))
    y2 = (dd * lax.rsqrt(ms2 + EPS) * sw_ref[...]) * (1.0 - lam_init)
    ycat = jnp.concatenate([y1, y2], axis=1).astype(BF16)
    o_ref[0] = x_ref[0] + g1_ref[0] * _dot(ycat, w_ref[...])


def _outproj(o_f, o_b, gate, d_lat, x, g1, gdn_norm_w, subln_w, w_out, *, lam_init, n_ctx, tm):
    bsz, n_lat, d = x.shape
    off = n_ctx // tm
    tok = lambda b, i: (b, i, 0)
    tok_off = lambda b, i: (b, i + off, 0)
    full = lambda shape: pl.BlockSpec(shape, lambda b, i: (0,) * len(shape))
    return pl.pallas_call(
        functools.partial(_outproj_kernel, lam_init=lam_init),
        grid=(bsz, n_lat // tm),
        in_specs=[pl.BlockSpec((1, tm, A_W), tok_off)] * 3
        + [pl.BlockSpec((1, tm, B_W), tok), pl.BlockSpec((1, tm, d), tok),
           pl.BlockSpec((1, 1, d), lambda b, i: (b, 0, 0)),
           full((1, A_W)), full((1, B_W)), full((A_W, A_W)), full((B_W, B_W)), full(w_out.shape)],
        out_specs=pl.BlockSpec((1, tm, d), tok),
        out_shape=jax.ShapeDtypeStruct((bsz, n_lat, d), F32),
        compiler_params=_cparams("parallel", "arbitrary"),
    )(o_f, o_b, gate, d_lat, x, g1, jnp.tile(gdn_norm_w, A_HEADS).reshape(1, A_W),
      jnp.tile(subln_w, B_HEADS).reshape(1, B_W), _block_ones(A_W, A_DV), _block_ones(B_W, 2 * B_DH),
      w_out.astype(BF16))


def _dft1_kernel(x_ref, nw_ref, sc_ref, sh_ref, f_ref, twr_ref, twi_ref, re_ref, im_ref, *, d, passes):
    n1 = f_ref.shape[1]
    f = f_ref[...]
    for t in range(re_ref.shape[2]):
        h = _normmod(x_ref[0, :, t * d:(t + 1) * d], nw_ref[...], sc_ref[0], sh_ref[0])
        a = _mm(f, h, passes)
        a_re, a_im = a[:n1], a[n1:]
        twr = jnp.concatenate([twr_ref[:, t * 128:(t + 1) * 128]] * (d // 128), axis=1)
        twi = jnp.concatenate([twi_ref[:, t * 128:(t + 1) * 128]] * (d // 128), axis=1)
        re_ref[0, :, t, :] = a_re * twr - a_im * twi
        im_ref[0, :, t, :] = a_re * twi + a_im * twr


def _dft2_kernel(re_ref, im_ref, x_ref, g1_ref, f_ref, cc_ref, sc_ref, w_ref, o_ref, *, d, passes):
    n2 = f_ref.shape[1]
    f = f_ref[...]
    dg = d // C_GROUPS
    for t in range(re_ref.shape[1]):
        pr = _mm(f, re_ref[0, t], passes)
        pi = _mm(f, im_ref[0, t], passes)
        x_re = pr[:n2] + pi[n2:]
        x_im = pi[:n2] - pr[n2:]
        y = jnp.concatenate(
            [_mm(x_re[:, g * dg:(g + 1) * dg], cc_ref[...], passes)
             + _mm(x_im[:, g * dg:(g + 1) * dg], sc_ref[...], passes) for g in range(C_GROUPS)], axis=1)
        o_ref[0, :, t, :] = x_ref[0, :, t, :] + g1_ref[0] * _dot(y.astype(BF16), w_ref[...])


def _fourier_layer(x, nw, sc, sh, g1, w_out, *, passes=3, nb=8, kb=8):
    bsz, n, d = x.shape
    n1, n2 = DFT_N1, n // DFT_N1
    dg = d // C_GROUPS
    k1 = np.arange(n1)
    ang1 = 2 * np.pi * np.outer(k1, k1) / n1
    f1 = jnp.asarray(np.concatenate([np.cos(ang1), -np.sin(ang1)], 0), F32)
    angt = 2 * np.pi * np.outer(k1, np.arange(n2)) / n
    twr = jnp.asarray(np.repeat(np.cos(angt), 128, axis=1), F32)
    twi = jnp.asarray(np.repeat(-np.sin(angt), 128, axis=1), F32)
    k2 = np.arange(n2)
    ang2 = 2 * np.pi * np.outer(k2, k2) / n2
    f2 = jnp.asarray(np.concatenate([np.cos(ang2), np.sin(ang2)], 0), F32)
    angc = 2 * np.pi * np.outer(np.arange(dg), np.arange(dg)) / dg
    scale = 1.0 / math.sqrt(n * dg)
    cc = jnp.asarray(np.cos(angc) * scale, F32)
    sn = jnp.asarray(np.sin(angc) * scale, F32)

    assert n2 % SUB == 0 and nb % SUB == 0 and kb % SUB == 0
    win1 = pl.BlockSpec((1, n1, nb, d), lambda b, j: (b, 0, j, 0))
    vec = pl.BlockSpec((1, 1, d), lambda b, j: (b, 0, 0))
    full = lambda shape: pl.BlockSpec(shape, lambda b, j: (0,) * len(shape))
    a_re, a_im = pl.pallas_call(
        functools.partial(_dft1_kernel, d=d, passes=passes),
        grid=(bsz, n2 // nb),
        in_specs=[pl.BlockSpec((1, n1, nb * d), lambda b, j: (b, 0, j)), full((1, d)), vec, vec, full(f1.shape),
                  pl.BlockSpec((n1, nb * 128), lambda b, j: (0, j)),
                  pl.BlockSpec((n1, nb * 128), lambda b, j: (0, j))],
        out_specs=[win1] * 2,
        out_shape=[jax.ShapeDtypeStruct((bsz, n1, n2, d), F32)] * 2,
        compiler_params=_cparams("parallel", "arbitrary"),
    )(x.reshape(bsz, n1, n2 * d), nw.reshape(1, d), sc, sh, f1, twr, twi)

    win2 = pl.BlockSpec((1, n2, kb, d), lambda b, j: (b, 0, j, 0))
    out = pl.pallas_call(
        functools.partial(_dft2_kernel, d=d, passes=passes),
        grid=(bsz, n1 // kb),
        in_specs=[pl.BlockSpec((1, kb, n2, d), lambda b, j: (b, j, 0, 0))] * 2
        + [win2, vec, full(f2.shape), full(cc.shape), full(sn.shape), full(w_out.shape)],
        out_specs=win2,
        out_shape=jax.ShapeDtypeStruct((bsz, n2, n1, d), F32),
        compiler_params=_cparams("parallel", "arbitrary"),
    )(a_re, a_im, x.reshape(bsz, n2, n1, d), g1, f2, cc, sn, w_out.astype(BF16))
    return out.reshape(bsz, n, d)


def _top16(s, payload=None):
    rows = lax.broadcasted_iota(I32, s.shape, 0).astype(F32)
    big = float(s.shape[0])
    vals, idxs = [], []
    for _ in range(PEER_TOPK):
        m = jnp.max(s, axis=0, keepdims=True)
        am = jnp.min(jnp.where(s == m, rows, big), axis=0, keepdims=True)
        sel = rows == am
        vals.append(m)
        if payload is None:
            idxs.append(am)
        else:
            idxs.append(jnp.max(jnp.where(sel, payload, -1.0), axis=0, keepdims=True))
        s = jnp.where(sel, -jnp.inf, s)
    return jnp.concatenate(vals, axis=0), jnp.concatenate(idxs, axis=0)


def _staircase_candidates(s1, i1, s2, i2):
    k = PEER_TOPK
    sub = lax.broadcasted_iota(I32, (SUB,) + s1.shape[1:], 0)
    cs = [s1[0:1] + s2, s1[1:2] + s2[:SUB]]
    ci = [i1[0:1] * PEER_NKEYS + i2, i1[1:2] * PEER_NKEYS + i2[:SUB]]
    for a in range(2, SUB):
        keep = sub < k // (a + 1)
        cs.append(jnp.where(keep, s1[a:a + 1] + s2[:SUB], -jnp.inf))
        ci.append(jnp.where(keep, i1[a:a + 1] * PEER_NKEYS + i2[:SUB], -1.0))
    cs.append(s1[SUB:] + s2[0:1])
    ci.append(i1[SUB:] * PEER_NKEYS + i2[0:1])
    return jnp.concatenate(cs, axis=0), jnp.concatenate(ci, axis=0)


def _peer_topk_kernel(x_ref, nw_ref, sc_ref, sh_ref, wh_ref, wl_ref, keys_ref,
                      h_out, row_out, par_out, gate_out):
    h = _normmod(x_ref[0], nw_ref[...], sc_ref[0], sh_ref[0])
    h_out[0] = h
    hh, hl = _split(h)
    q = _dot(hh, wh_ref[...]) + (_dot(hh, wl_ref[...]) + _dot(hl, wh_ref[...]))
    half = PEER_DKEY // 2
    idx_rows, gate_rows = [], []
    for hd in range(PEER_HEADS):
        tops = []
        for p in range(2):
            j = hd * 2 + p
            st = _mm(keys_ref[j], q[:, j * half:(j + 1) * half], 3, _dot_nt)
            tops.append(_top16(st))
        (s1, i1), (s2, i2) = tops
        top_s, top_i = _top16(*_staircase_candidates(s1, i1, s2, i2))
        ex = jnp.exp(top_s - top_s[0:1])
        gate_rows.append(ex / jnp.sum(ex, axis=0, keepdims=True))
        idx_rows.append(top_i)
    idx = jnp.concatenate(idx_rows, axis=0).T.astype(I32)
    n_pairs = PEER_NKEYS * PEER_NKEYS // 2
    row_out[0] = (idx & (n_pairs - 1)) * SUB
    par_out[0] = (idx >= n_pairs).astype(I32)
    gate_out[0] = jnp.concatenate(gate_rows, axis=0).T


def _peer_topk(x, nw, sc, sh, w_q, keys, *, tb):
    bsz, n, d = x.shape
    nk = PEER_HEADS * PEER_TOPK
    wh, wl = _split_outside(w_q)
    keys2 = keys.reshape(PEER_HEADS * 2, PEER_NKEYS, PEER_DKEY // 2)
    tok = lambda b, i: (b, i, 0)
    vec = pl.BlockSpec((1, 1, d), lambda b, i: (b, 0, 0))
    full = lambda shape: pl.BlockSpec(shape, lambda b, i: (0,) * len(shape))
    return pl.pallas_call(
        _peer_topk_kernel,
        grid=(bsz, n // tb),
        in_specs=[pl.BlockSpec((1, tb, d), tok), full((1, d)), vec, vec, full(wh.shape), full(wl.shape),
                  full(keys2.shape)],
        out_specs=[pl.BlockSpec((1, tb, d), tok)] + [pl.BlockSpec((1, tb, nk), tok)] * 3,
        out_shape=[jax.ShapeDtypeStruct((bsz, n, d), F32), jax.ShapeDtypeStruct((bsz, n, nk), I32),
                   jax.ShapeDtypeStruct((bsz, n, nk), I32), jax.ShapeDtypeStruct((bsz, n, nk), F32)],
        compiler_params=_cparams("parallel", "arbitrary"),
    )(x, nw.reshape(1, d), sc, sh, wh, wl, keys2)


def _pack_kernel(hi_ref, lo_ref, o_ref):
    pairs, d = hi_ref.shape
    bits = lambda v: lax.bitcast_convert_type(v.astype(BF16).astype(F32), U32)
    packed = bits(hi_ref[...]) | (bits(lo_ref[...]) >> 16)
    for s in range(d // 128):
        o_ref[pl.ds(s, pairs, stride=SUB), :] = packed[:, s * 128:(s + 1) * 128]


def _pack_table(tabs, layer, *, pairs=256):
    _, e, d = tabs.shape
    assert d // 128 == SUB
    nblk = e // 2 // pairs
    return pl.pallas_call(
        _pack_kernel,
        grid=(nblk,),
        in_specs=[pl.BlockSpec((None, pairs, d), lambda i: (layer, i, 0)),
                  pl.BlockSpec((None, pairs, d), lambda i: (layer, i + nblk, 0))],
        out_specs=pl.BlockSpec((pairs * SUB, 128), lambda i: (i, 0)),
        out_shape=jax.ShapeDtypeStruct((e // 2 * SUB, 128), U32),
        compiler_params=_cparams("parallel"),
    )(tabs, tabs)


def _gather_tiles(tab_ref, row_ref, t, nk):
    tiles = [tab_ref[pl.ds(pl.multiple_of(row_ref[t, kk], SUB), SUB), :] for kk in range(nk)]
    return pltpu.bitcast(jnp.concatenate(tiles, axis=0), BF16)


def _gather_constants(nk):
    c = np.arange(nk * PAIR_COLS)
    pair = (c[None, :] // PAIR_COLS == np.arange(nk)[:, None])
    hi = (c % 2 == 1)[None, :]
    smask = ((c[None, :] % PAIR_COLS) // 2 == np.arange(SUB)[:, None]).astype(np.float32)
    as_bf16 = lambda m: jnp.asarray(m.astype(np.float32), dtype=BF16)
    return dict(e_all=as_bf16(pair), e_hi=as_bf16(pair & hi), e_lo=as_bf16(pair & ~hi),
                collapse=as_bf16(pair.T), smask=jnp.asarray(smask), hi_row=jnp.asarray(hi.astype(np.float32)))


def _gelu_tanh(x):
    return 0.5 * x * (1.0 + jnp.tanh(math.sqrt(2.0 / math.pi) * (x + 0.044715 * (x * x * x))))


def _peer_act_kernel(row_ref, par_ref, h_ref, gate_ref, tab_ref, eall_ref, coll_ref, smask_ref, hi_ref,
                     w_out, rsum_ref, h8_ref):
    tb, nk = gate_ref.shape
    smask = smask_ref[...]
    for s in range(SUB):
        h8_ref[pl.ds(s, tb, stride=SUB), :] = h_ref[:, s * 128:(s + 1) * 128]

    def token(t, carry):
        g = _gather_tiles(tab_ref, row_ref, t, nk)
        hh, hl = _split(h8_ref[pl.ds(pl.multiple_of(t * SUB, SUB), SUB), :])
        r = _dot_nt(jnp.concatenate([hh, hl], axis=0), g)
        rsum_ref[pl.ds(t, 1), :] = jnp.sum((r[:SUB] + r[SUB:]) * smask, axis=0, keepdims=True)
        return carry

    lax.fori_loop(0, tb, token, 0, unroll=2 * GATHER_UNROLL)
    want_hi = 1.0 - _dot(par_ref[...].astype(F32).astype(BF16), eall_ref[...])
    picked = jnp.where(want_hi == hi_ref[...], rsum_ref[...], 0.0)
    w_out[...] = gate_ref[...] * _gelu_tanh(_mm_lhs2(picked, coll_ref[...]))


def _peer_mix_kernel(row_ref, par_ref, w_ref, x_ref, g2_ref, tab_ref, ehi_ref, elo_ref, smask_ref,
                     o_ref, wcols_ref, acc_ref):
    tb, nk = w_ref.shape
    smask = smask_ref[...]
    par = par_ref[...].astype(F32)
    w = w_ref[...]
    wcols_ref[...] = _mm_lhs2(w * (1.0 - par), ehi_ref[...]) + _mm_lhs2(w * par, elo_ref[...])

    def token(t, carry):
        g = _gather_tiles(tab_ref, row_ref, t, nk)
        wh, wl = _split(wcols_ref[pl.ds(t, 1), :] * smask)
        res = _dot(jnp.concatenate([wh, wl], axis=0), g)
        acc_ref[pl.ds(pl.multiple_of(t * SUB, SUB), SUB), :] = res[:SUB] + res[SUB:]
        return carry

    lax.fori_loop(0, tb, token, 0, unroll=GATHER_UNROLL)
    for s in range(SUB):
        cs = slice(s * 128, (s + 1) * 128)
        o_ref[:, cs] = x_ref[:, cs] + g2_ref[0][:, cs] * acc_ref[pl.ds(s, tb, stride=SUB), :]


def _peer_gather(x, h, row, par, gate, g2, u_tabs, v_tabs, layer, *, tb):
    bsz, n, d = x.shape
    u_tab, v_tab = _pack_table(u_tabs, layer), _pack_table(v_tabs, layer)
    t_all = bsz * n
    nk = row.shape[-1]
    sl = d // 128
    assert sl == SUB
    row2, par2 = row.reshape(t_all, nk), par.reshape(t_all, nk)
    cst = _gather_constants(nk)
    ncol = nk * PAIR_COLS
    tab_spec = pl.BlockSpec(u_tab.shape, lambda i: (0, 0), pipeline_mode=pl.Buffered(1))
    smem = pl.BlockSpec((tb, nk), lambda i: (i, 0), memory_space=pltpu.SMEM)
    tokk = pl.BlockSpec((tb, nk), lambda i: (i, 0))
    tokd = pl.BlockSpec((tb, d), lambda i: (i, 0))
    tiles = pltpu.VMEM((tb * sl, 128), F32)
    full = lambda a: pl.BlockSpec(a.shape, lambda i: (0,) * a.ndim)
    w = pl.pallas_call(
        _peer_act_kernel,
        grid=(t_all // tb,),
        in_specs=[smem, tokk, tokd, tokk, tab_spec, full(cst["e_all"]), full(cst["collapse"]),
                  full(cst["smask"]), full(cst["hi_row"])],
        out_specs=tokk,
        out_shape=jax.ShapeDtypeStruct((t_all, nk), F32),
        scratch_shapes=[pltpu.VMEM((tb, ncol), F32), tiles],
        compiler_params=_cparams("arbitrary"),
    )(row2, par2, h.reshape(t_all, d), gate.reshape(t_all, nk), u_tab,
      cst["e_all"], cst["collapse"], cst["smask"], cst["hi_row"])
    out = pl.pallas_call(
        _peer_mix_kernel,
        grid=(t_all // tb,),
        in_specs=[smem, tokk, tokk, tokd, pl.BlockSpec((1, 1, d), lambda i: ((i * tb) // n, 0, 0)),
                  tab_spec, full(cst["e_hi"]), full(cst["e_lo"]), full(cst["smask"])],
        out_specs=tokd,
        out_shape=jax.ShapeDtypeStruct((t_all, d), F32),
        scratch_shapes=[pltpu.VMEM((tb, ncol), F32), tiles],
        compiler_params=_cparams("arbitrary"),
    )(row2, par2, w, x.reshape(t_all, d), g2, v_tab,
      cst["e_hi"], cst["e_lo"], cst["smask"])
    return out.reshape(bsz, n, d)


def _peer_layer(x, nw, sc, sh, g2, w_q, keys, u_tabs, v_tabs, layer):
    h, row, par, gate = _peer_topk(x, nw, sc, sh, w_q, keys, tb=128)
    return _peer_gather(x, h, row, par, gate, g2, u_tabs, v_tabs, layer, tb=128)


def _final_norm_kernel(x_ref, w_ref, o_ref):
    x = x_ref[0]
    o_ref[0] = x * lax.rsqrt(jnp.mean(x * x, axis=-1, keepdims=True) + EPS) * w_ref[...]


def _final_norm(x, w, *, tm):
    bsz, n, d = x.shape
    return pl.pallas_call(
        _final_norm_kernel,
        grid=(bsz, n // tm),
        in_specs=[pl.BlockSpec((1, tm, d), lambda b, i: (b, i, 0)), pl.BlockSpec((1, d), lambda b, i: (0, 0))],
        out_specs=pl.BlockSpec((1, tm, d), lambda b, i: (b, i, 0)),
        out_shape=jax.ShapeDtypeStruct((bsz, n, d), F32),
        compiler_params=_cparams("parallel", "arbitrary"),
    )(x, w.reshape(1, d))


def _rope_tables(n_ctx, n_lat):
    pos = np.arange(n_lat)
    axis_dim = B_DH // 2
    inv = ROPE_BASE ** (-np.arange(0, axis_dim, 2, dtype=np.float32) / axis_dim)
    ang = np.concatenate([(pos // GRID_W)[:, None] * inv, (pos % GRID_W)[:, None] * inv], axis=-1)
    ang = np.concatenate([np.zeros((n_ctx, ang.shape[1])), ang], axis=0).astype(np.float32)
    cos = np.cos(ang)
    sin = np.sin(ang)
    cos64 = np.concatenate([cos, cos], axis=1)
    sin64 = np.concatenate([-sin, sin], axis=1)
    return (jnp.asarray(np.tile(cos64, (1, 2)), F32), jnp.asarray(np.tile(sin64, (1, 2)), F32))


def _mixer_ab_layer(x, ctx, mod, mod_c, layer, norm1_w, w_in, conv_w, a_log, dt_bias, gdn_norm_w,
                    lam_params, subln_w, w_out):
    bsz, n_lat, d = x.shape
    n_ctx = ctx.shape[1]
    n = n_ctx + n_lat
    tm = GDN_BLOCK
    lam_init = 0.8 - 0.6 * math.exp(-0.3 * layer)

    assert n_ctx % tm == 0 and n_lat % tm == 0
    seg = lambda j: jnp.stack([jnp.broadcast_to(mod_c[j], (bsz, d)), mod[:, j]], axis=1)[:, :, None, :]
    cuts = np.cumsum([0, 3 * A_W, A_W, 4 * A_HEADS, B_W, B_W, B_W])
    weights = [w_in[:, a:b] for a, b in zip(cuts[:-1], cuts[1:])]
    qkv, gate, ba, bq, bk, bv, edges = _in_proj(
        ctx, x, norm1_w, seg(1), seg(0), weights, [F32, F32, F32, F32, F32, BF16], tm=tm)

    first, last = edges[:, :, 0], edges[:, :, 1]
    zero = jnp.zeros_like(last[:, :1])
    prev_rows = jnp.concatenate([zero, last[:, :-1]], axis=1)
    next_rows = jnp.concatenate([first[:, 1:], zero], axis=1)
    blk = jnp.arange(n // tm)[None, :, None]
    prev_rows = jnp.where(blk == n_ctx // tm, 0.0, prev_rows)[:, :, None, :]
    next_rows = jnp.where(blk == n_ctx // tm - 1, 0.0, next_rows)[:, :, None, :]
    cos128, sin128 = _rope_tables(n_ctx, n_lat)
    q, k, v, bqr, bkr = _prep(qkv, prev_rows, next_rows, conv_w, bq, bk, cos128, sin128, tm=tm, n_ctx=n_ctx)

    bat = jnp.swapaxes(ba, 1, 2)
    o_f = _gdn(q, k, v, ba, bat, a_log, dt_bias, d=0, n_ctx=n_ctx, inv_passes=1)
    o_b = _gdn(q, k, v, ba, bat, a_log, dt_bias, d=1, n_ctx=n_ctx, inv_passes=1)
    d_lat = _diff_attention(bqr, bkr, jnp.swapaxes(bv, 1, 2), lam_params, lam_init=lam_init, tq=512,
                            tk=n // 6 if n % (6 * 128) == 0 else n)
    return _outproj(o_f, o_b, gate, d_lat, x, mod[:, 2][:, None, :], gdn_norm_w, subln_w, w_out,
                    lam_init=lam_init, n_ctx=n_ctx, tm=tm)


def kernel(x, c, ctx, c_ctx, ada_w, ada_b, norm1_w, norm2_w, w_in, conv_w, a_log, dt_bias, gdn_norm_w,
           lam_q1, lam_k1, lam_q2, lam_k2, subln_w, w_out_ab, w_out_f, peer_wq, peer_keys, peer_u,
           peer_v, final_norm_w):
    bsz, n_lat, d = x.shape
    depth = ada_w.shape[0]
    last_ctx_reader = 2 * ((depth - 1) // 2)
    cmat = jnp.concatenate([c, c_ctx[None, :], jnp.zeros((8 - bsz - 1, d), F32)], axis=0)
    for i in range(depth):
        if i < last_ctx_reader:
            raise NotImplementedError("context stream advance (depth > 2) is not implemented")
        mod_all = _modulation(cmat, ada_w[i], ada_b[i])
        mod = mod_all[:bsz].reshape(bsz, 6, d)
        mod_c = mod_all[bsz].reshape(6, d)
        vec = lambda j: mod[:, j][:, None, :]
        j = i // 2
        if i % 2 == 0:
            x = _mixer_ab_layer(x, ctx, mod, mod_c, i, norm1_w[i], w_in[j], conv_w[j], a_log[j], dt_bias[j],
                                gdn_norm_w[j], (lam_q1[j], lam_k1[j], lam_q2[j], lam_k2[j]), subln_w[j],
                                w_out_ab[j])
        else:
            x = _fourier_layer(x, norm1_w[i], vec(1), vec(0), vec(2), w_out_f[j])
        x = _peer_layer(x, norm2_w[i], vec(4), vec(3), vec(5), peer_wq[i], peer_keys[i], peer_u, peer_v, i)
    return _final_norm(x, final_norm_w, tm=512)
```

```python
import functools
import math

import numpy as np
import jax
import jax.numpy as jnp
from jax import lax
from jax.experimental import pallas as pl
from jax.experimental.pallas import tpu as pltpu

F32 = jnp.float32
BF16 = jnp.bfloat16
I32 = jnp.int32
U32 = jnp.uint32
HIGHEST = lax.Precision.HIGHEST

EPS = 1e-6
ROPE_BASE = 10000.0
GRID_W = 64
A_HEADS, A_DK, A_DV, A_CHUNK = 8, 64, 64, 64
B_HEADS, B_DH = 4, 64
C_GROUPS = 4
PEER_HEADS, PEER_NKEYS, PEER_DKEY, PEER_TOPK = 8, 128, 256, 16
A_W = A_HEADS * A_DK
B_W = B_HEADS * 2 * B_DH
GDN_BLOCK = 256
GDN_HEAD_GROUP = 8
ATTN_SLAB = 64
ATTN_TK = 1408
DFT_N1 = 128
SUB = 8
PAIR_COLS = 2 * SUB
GATHER_UNROLL = 8

VMEM_LIMIT_BYTES = 48 * 1024 * 1024


def _cparams(*sem):
    return pltpu.CompilerParams(dimension_semantics=sem, vmem_limit_bytes=VMEM_LIMIT_BYTES)


def _dot(a, b):
    return lax.dot_general(a, b, (((1,), (0,)), ((), ())), preferred_element_type=F32)


def _dot_nt(a, b):
    return lax.dot_general(a, b, (((1,), (1,)), ((), ())), preferred_element_type=F32)


def _dot_tn(a, b):
    return lax.dot_general(a, b, (((0,), (0,)), ((), ())), preferred_element_type=F32)


def _dot_f32(a, b):
    return lax.dot_general(a, b, (((1,), (0,)), ((), ())), precision=HIGHEST,
                           preferred_element_type=F32)


def _split(a):
    hi = a.astype(BF16)
    lo = (a - hi.astype(F32)).astype(BF16)
    return hi, lo


def _mm(a, b, passes=1, dot=_dot):
    if passes == 1:
        return dot(a.astype(BF16), b.astype(BF16))
    ah, al = _split(a)
    bh, bl = _split(b)
    return dot(ah, bh) + (dot(ah, bl) + dot(al, bh))


def _split3(a):
    hi = a.astype(BF16)
    r = a - hi.astype(F32)
    mid = r.astype(BF16)
    return hi, mid, (r - mid.astype(F32)).astype(BF16)


def _mm_lhs2(a, b_bf16):
    ah, al = _split(a)
    return _dot(ah, b_bf16) + _dot(al, b_bf16)


def _silu(x):
    return x * jax.nn.sigmoid(x)


def _softplus(x):
    return jnp.maximum(x, 0.0) + jnp.log(1.0 + jnp.exp(-jnp.abs(x)))


def _normmod(x, nw, sc, sh):
    ms = jnp.mean(x * x, axis=-1, keepdims=True)
    return (x * lax.rsqrt(ms + EPS) * nw) * (1.0 + sc) + sh


def _split_outside(w):
    hi = lax.bitcast_convert_type(lax.bitcast_convert_type(w, U32) & jnp.uint32(0xFFFF0000), F32)
    return hi.astype(BF16), (w - hi).astype(BF16)


def _block_ones(n, blk):
    i = np.arange(n) // blk
    return jnp.asarray((i[:, None] == i[None, :]).astype(np.float32), dtype=BF16)


def _mod_kernel(c_ref, w_ref, b_ref, o_ref):
    o_ref[...] = _dot_f32(_silu(c_ref[...]), w_ref[...]) + b_ref[...]


def _modulation(cmat, w, b):
    rows, d = cmat.shape
    n = w.shape[1]
    return pl.pallas_call(
        _mod_kernel,
        grid=(n // d,),
        in_specs=[pl.BlockSpec((rows, d), lambda j: (0, 0)),
                  pl.BlockSpec((d, d), lambda j: (0, j)),
                  pl.BlockSpec((1, d), lambda j: (0, j))],
        out_specs=pl.BlockSpec((rows, d), lambda j: (0, j)),
        out_shape=jax.ShapeDtypeStruct((rows, n), F32),
        compiler_params=_cparams("arbitrary"),
    )(cmat, w, b.reshape(1, n))


def _in_proj_kernel(*refs, n_w, n_first):
    ctx_ref, x_ref, nw_ref, sc_ref, sh_ref = refs[:5]
    w_refs = refs[5:5 + n_w]
    outs = refs[5 + n_w:5 + 2 * n_w]
    edge_ref = refs[5 + 2 * n_w]
    xin = jnp.where(pl.program_id(1) < n_first, ctx_ref[0], x_ref[0])
    hb = _normmod(xin, nw_ref[...], sc_ref[0, 0], sh_ref[0, 0]).astype(BF16)
    for j in range(n_w):
        y = _dot(hb, w_refs[j][...])
        outs[j][0] = y.astype(outs[j].dtype)
        if j == 0:
            tm = y.shape[0]
            edge_ref[0, 0] = jnp.concatenate(
                [y[0:1], y[tm - 1:tm], jnp.zeros((SUB - 2, y.shape[1]), F32)], axis=0)


def _in_proj(ctx, x, nw, sc, sh, weights, out_dtypes, *, tm):
    bsz, n_lat, d = x.shape
    n_first = ctx.shape[1] // tm
    nblk = n_first + n_lat // tm
    widths = tuple(w.shape[1] for w in weights)
    w_in = [w.astype(BF16) for w in weights]
    mod_map = lambda b, i: (b, jnp.where(i < n_first, 0, 1), 0, 0)
    tok = lambda b, i: (b, i, 0)
    in_specs = [pl.BlockSpec((1, tm, d), lambda b, i: (b, jnp.minimum(i, n_first - 1), 0)),
                pl.BlockSpec((1, tm, d), lambda b, i: (b, jnp.maximum(i - n_first, 0), 0)),
                pl.BlockSpec((1, d), lambda b, i: (0, 0)),
                pl.BlockSpec((1, 1, 1, d), mod_map), pl.BlockSpec((1, 1, 1, d), mod_map)]
    in_specs += [pl.BlockSpec(w.shape, lambda b, i: (0, 0)) for w in w_in]
    out_specs = [pl.BlockSpec((1, tm, wd), tok) for wd in widths]
    out_specs.append(pl.BlockSpec((1, 1, SUB, widths[0]), lambda b, i: (b, i, 0, 0)))
    out_shape = [jax.ShapeDtypeStruct((bsz, nblk * tm, wd), dt) for wd, dt in zip(widths, out_dtypes)]
    out_shape.append(jax.ShapeDtypeStruct((bsz, nblk, SUB, widths[0]), F32))
    return pl.pallas_call(
        functools.partial(_in_proj_kernel, n_w=len(w_in), n_first=n_first),
        grid=(bsz, nblk),
        in_specs=in_specs, out_specs=out_specs, out_shape=out_shape,
        compiler_params=_cparams("parallel", "arbitrary"),
    )(ctx, x, nw.reshape(1, d), sc, sh, *w_in)


def _prep_kernel(qkv_ref, prev_ref, next_ref, cw_ref, bd_ref, bq_ref, bk_ref, cos_ref, sin_ref,
                 q_out, k_out, v_out, bqr_out, bkr_out):
    x = qkv_ref[0]
    tm = x.shape[0]
    row = lax.broadcasted_iota(I32, x.shape, 0)
    xm = jnp.where(row == 0, prev_ref[0, 0], pltpu.roll(x, 1, 0))
    xp = jnp.where(row == tm - 1, next_ref[0, 0], pltpu.roll(x, tm - 1, 0))
    cw = cw_ref[...]
    y = _silu(xm * cw[0:1] + x * cw[1:2] + xp * cw[2:3])
    bd = bd_ref[...]

    def l2(t):
        return t * lax.rsqrt(_mm_lhs2(t * t, bd) + EPS)

    q_out[0] = l2(y[:, :A_W]) * (A_DK ** -0.5)
    k_out[0] = l2(y[:, A_W:2 * A_W])
    v_out[0] = y[:, 2 * A_W:]

    cos = jnp.concatenate([cos_ref[...]] * (B_W // 128), axis=1)
    sin = jnp.concatenate([sin_ref[...]] * (B_W // 128), axis=1)
    lane = lax.broadcasted_iota(I32, (tm, B_W), 1)
    first = (lane & (B_DH - 1)) < (B_DH // 2)

    def rope(t):
        rot = jnp.where(first, pltpu.roll(t, B_W - B_DH // 2, 1), pltpu.roll(t, B_DH // 2, 1))
        return t * cos + rot * sin

    bqr_out[0] = (rope(bq_ref[0]) * (B_DH ** -0.5)).astype(BF16)
    bkr_out[0] = rope(bk_ref[0]).astype(BF16)


def _prep(qkv, prev_rows, next_rows, conv_w, bq, bk, cos128, sin128, *, tm, n_ctx):
    bsz, n, cw = qkv.shape
    tok = lambda b, i: (b, i, 0)
    lat = lambda b, i: (b, jnp.maximum(i - n_ctx // tm, 0), 0)
    halo = pl.BlockSpec((1, 1, 1, cw), lambda b, i: (b, i, 0, 0))
    return pl.pallas_call(
        _prep_kernel,
        grid=(bsz, n // tm),
        in_specs=[pl.BlockSpec((1, tm, cw), tok), halo, halo,
                  pl.BlockSpec(conv_w.shape, lambda b, i: (0, 0)),
                  pl.BlockSpec((A_W, A_W), lambda b, i: (0, 0)),
                  pl.BlockSpec((1, tm, B_W), tok), pl.BlockSpec((1, tm, B_W), tok),
                  pl.BlockSpec((tm, 128), lambda b, i: (i, 0)),
                  pl.BlockSpec((tm, 128), lambda b, i: (i, 0))],
        out_specs=[pl.BlockSpec((1, tm, A_W), tok)] * 3
        + [pl.BlockSpec((1, tm, B_W), lat), pl.BlockSpec((1, tm, B_W), tok)],
        out_shape=[jax.ShapeDtypeStruct((bsz, n, A_W), F32)] * 3
        + [jax.ShapeDtypeStruct((bsz, n - n_ctx, B_W), BF16), jax.ShapeDtypeStruct((bsz, n, B_W), BF16)],
        compiler_params=_cparams("parallel", "arbitrary"),
    )(qkv, prev_rows, next_rows, conv_w, _block_ones(A_W, A_DK), bq, bk, cos128, sin128)


def _gdn_kernel(q_ref, k_ref, v_ref, ba_ref, bat_ref, alr_ref, dtr_ref, alc_ref, dtc_ref,
                lm_ref, lmt_ref, ltot_ref, e_ref, o_ref, s_ref, *, d, inv_passes):
    rev = d == 1
    nh = A_HEADS

    @pl.when(pl.program_id(1) == 0)
    def _():
        s_ref[...] = jnp.zeros_like(s_ref)

    q, k, v = q_ref[0], k_ref[0], v_ref[0]
    ba, bat = ba_ref[0], bat_ref[0]
    n = q.shape[0]
    beta_c = jax.nn.sigmoid(ba[:, d * nh:(d + 1) * nh])
    g_c = -jnp.exp(alr_ref[...]) * _softplus(ba[:, 2 * nh + d * nh:2 * nh + (d + 1) * nh] + dtr_ref[...])
    g_t = -jnp.exp(alc_ref[...]) * _softplus(bat[2 * nh + d * nh:2 * nh + (d + 1) * nh, :] + dtc_ref[...])
    lm = lm_ref[...]
    gs = _split3(g_c)
    gcum_c = sum(_dot(lm.astype(BF16), p) for p in gs)
    gtot_c = sum(_dot(ltot_ref[...].astype(BF16), p) for p in gs)
    gcum_t = sum(_dot(p, lmt_ref[...].astype(BF16)) for p in _split3(g_t))
    small = jnp.concatenate([jnp.exp(gcum_c), jnp.exp(gtot_c - gcum_c), jnp.exp(gtot_c), beta_c], axis=1)
    wide = _mm_lhs2(small, e_ref[...])
    eg, ekt, egt, beta_e = (wide[:, j * A_W:(j + 1) * A_W] for j in range(4))
    kb = k * beta_e
    vb = v * beta_e
    kbe = kb * eg
    qd = q * eg
    ktail = k * ekt

    ri = lax.broadcasted_iota(I32, (n, n), 0)
    ci = lax.broadcasted_iota(I32, (n, n), 1)
    eye = (ri == ci).astype(F32)
    same = lambda s: (ri >> s) == (ci >> s)
    d16 = same(4).astype(F32)
    o32 = (same(5) & ~same(4)).astype(F32)
    o64 = (same(6) & ~same(5)).astype(F32)
    incl = lm > 0.0
    strict = lm * (1.0 - eye)

    order = range(n // A_CHUNK - 1, -1, -1) if rev else range(n // A_CHUNK)
    sls = [slice(h * A_DK, (h + 1) * A_DK) for h in range(nh)]
    heads = [None] * nh
    for g0 in range(0, nh, GDN_HEAD_GROUP):
        hs = range(g0, g0 + GDN_HEAD_GROUP)
        decay = {h: jnp.exp(jnp.where(incl, gcum_c[:, h:h + 1] - gcum_t[h:h + 1, :], -1e30)) for h in hs}
        a = {h: _mm(kb[:, sls[h]], k[:, sls[h]], 1, _dot_nt) * decay[h] * strict for h in hs}
        p = {h: -(a[h] * d16) for h in hs}
        x = {h: eye + p[h] for h in hs}
        for _ in range(3):
            p = {h: _mm(p[h], p[h], inv_passes) for h in hs}
            x = {h: x[h] + _mm(x[h], p[h], inv_passes) for h in hs}
        for om in (o32, o64):
            y = {h: _mm(x[h], a[h] * om, inv_passes) for h in hs}
            x = {h: x[h] - _mm(y[h], x[h], inv_passes) for h in hs}
        uw = {h: _mm(x[h], jnp.concatenate([vb[:, sls[h]], kbe[:, sls[h]]], axis=1)) for h in hs}
        attn = {h: _mm(q[:, sls[h]], k[:, sls[h]], 1, _dot_nt) * decay[h] for h in hs}
        s = {h: s_ref[h] for h in hs}
        parts = {h: {} for h in hs}
        for c in order:
            r = slice(c * A_CHUNK, (c + 1) * A_CHUNK)
            v_new = {h: uw[h][r, :A_DV] - _mm(uw[h][r, A_DV:], s[h]) for h in hs}
            for h in hs:
                parts[h][c] = _mm(qd[r, sls[h]], s[h]) + _mm(attn[h][r, r], v_new[h])
            s = {h: s[h] * egt[c * A_CHUNK:c * A_CHUNK + 1, sls[h]]
                 + _mm(ktail[r, sls[h]], v_new[h], 1, _dot_tn) for h in hs}
        for h in hs:
            s_ref[h] = s[h]
            heads[h] = jnp.concatenate([parts[h][c] for c in range(n // A_CHUNK)], axis=0)
    o_ref[0] = jnp.concatenate(heads, axis=1)


def _gdn_constants(d):
    n = GDN_BLOCK
    i = np.arange(n)
    same = (i[:, None] // A_CHUNK) == (i[None, :] // A_CHUNK)
    tri = (i[None, :] >= i[:, None]) if d == 1 else (i[None, :] <= i[:, None])
    lm = (same & tri).astype(np.float32)
    e = (np.arange(A_W)[None, :] // A_DK == np.arange(A_HEADS)[:, None]).astype(np.float32)
    e4 = np.kron(np.eye(4, dtype=np.float32), e)
    return (jnp.asarray(lm), jnp.asarray(lm.T.copy()), jnp.asarray(same.astype(np.float32)),
            jnp.asarray(e4, dtype=BF16))


def _gdn(q, k, v, ba, bat, a_log, dt_bias, *, d, n_ctx, inv_passes):
    bsz, n, _ = q.shape
    nb, nc = n // GDN_BLOCK, n_ctx // GDN_BLOCK
    blk = (lambda s: jnp.where(s < nc, nc - 1 - s, nb - 1 + nc - s)) if d == 1 else (lambda s: s)
    tok = lambda b, s: (b, blk(s), 0)
    full = lambda shape: pl.BlockSpec(shape, lambda b, s: (0,) * len(shape))
    lm, lmt, ltot, e = _gdn_constants(d)
    return pl.pallas_call(
        functools.partial(_gdn_kernel, d=d, inv_passes=inv_passes),
        grid=(bsz, nb),
        in_specs=[pl.BlockSpec((1, GDN_BLOCK, A_W), tok)] * 3
        + [pl.BlockSpec((1, GDN_BLOCK, ba.shape[2]), tok),
           pl.BlockSpec((1, bat.shape[1], GDN_BLOCK), lambda b, s: (b, 0, blk(s))),
           full((1, A_HEADS)), full((1, A_HEADS)), full((A_HEADS, 1)), full((A_HEADS, 1)),
           full(lm.shape), full(lm.shape), full(lm.shape), full(e.shape)],
        out_specs=pl.BlockSpec((1, GDN_BLOCK, A_W), tok),
        out_shape=jax.ShapeDtypeStruct((bsz, n, A_W), F32),
        scratch_shapes=[pltpu.VMEM((A_HEADS, A_DK, A_DV), F32)],
        compiler_params=_cparams("parallel", "arbitrary"),
    )(q, k, v, ba, bat, a_log[d].reshape(1, -1), dt_bias[d].reshape(1, -1),
      a_log[d].reshape(-1, 1), dt_bias[d].reshape(-1, 1), lm, lmt, ltot, e)


def _attn_kernel(q_ref, k_ref, v_ref, lq1_ref, lk1_ref, lq2_ref, lk2_ref, o_ref, *, lam_init, tk):
    q = q_ref[0]
    tq = q.shape[0]
    nk = k_ref.shape[1]
    lam = (jnp.exp(jnp.sum(lq1_ref[...] * lk1_ref[...], keepdims=True))
           - jnp.exp(jnp.sum(lq2_ref[...] * lk2_ref[...], keepdims=True)) + lam_init)
    def over_keys(op, t):
        return op(op(t.reshape(tk // ATTN_SLAB, ATTN_SLAB, tq), axis=0), axis=0, keepdims=True)

    qs = [q[:, m * B_DH:(m + 1) * B_DH] for m in range(2)]
    mx = [jnp.full((1, tq), -jnp.inf, F32)] * 2
    l = [jnp.zeros((1, tq), F32)] * 2
    acc = [jnp.zeros((2 * B_DH, tq), F32)] * 2
    for c in range(nk // tk):
        ks = slice(c * tk, (c + 1) * tk)
        s = [_dot_nt(k_ref[0, ks, m * B_DH:(m + 1) * B_DH], qs[m]) for m in range(2)]
        m_new = [jnp.maximum(mx[m], over_keys(jnp.max, s[m])) for m in range(2)]
        alpha = [jnp.exp(mx[m] - m_new[m]) for m in range(2)]
        p = [jnp.exp(s[m] - m_new[m]) for m in range(2)]
        l = [alpha[m] * l[m] + over_keys(jnp.sum, p[m]) for m in range(2)]
        acc = [alpha[m] * acc[m] + _dot(v_ref[0, :, ks], p[m].astype(BF16)) for m in range(2)]
        mx = m_new
    o_ref[0] = (acc[0] / l[0] - lam * (acc[1] / l[1])).T


def _diff_attention(bqr, bkr, bv, lam_params, *, lam_init, tq, tk):
    bsz, n_lat, _ = bqr.shape
    n = bkr.shape[1]
    hw = 2 * B_DH
    assert n_lat % tq == 0 and n % tk == 0 and tk % ATTN_SLAB == 0
    lam_spec = pl.BlockSpec((1, B_DH), lambda b, h, i: (0, 0))
    return pl.pallas_call(
        functools.partial(_attn_kernel, lam_init=lam_init, tk=tk),
        grid=(bsz, B_HEADS, n_lat // tq),
        in_specs=[pl.BlockSpec((1, tq, hw), lambda b, h, i: (b, i, h)),
                  pl.BlockSpec((1, n, hw), lambda b, h, i: (b, 0, h)),
                  pl.BlockSpec((1, hw, n), lambda b, h, i: (b, h, 0))] + [lam_spec] * 4,
        out_specs=pl.BlockSpec((1, tq, hw), lambda b, h, i: (b, i, h)),
        out_shape=jax.ShapeDtypeStruct((bsz, n_lat, B_W), F32),
        compiler_params=_cparams("parallel", "parallel", "arbitrary"),
    )(bqr, bkr, bv, *[p.reshape(1, B_DH) for p in lam_params])


def _outproj_kernel(of_ref, ob_ref, gate_ref, d_ref, x_ref, g1_ref, gw_ref, sw_ref, bd64_ref,
                    bd128_ref, w_ref, o_ref, *, lam_init):
    o = of_ref[0] + ob_ref[0]
    ms = _mm_lhs2(o * o, bd64_ref[...]) * (1.0 / A_DV)
    y1 = (o * lax.rsqrt(ms + EPS) * gw_ref[...]) * _silu(gate_ref[0])
    dd = d_ref[0]
    ms2 = _mm_lhs2(dd * dd, bd128_ref[...]) * (1.0 / (2 * B_DH))
    y2 = (dd * lax.rsqrt(ms2 + EPS) * sw_ref[...]) * (1.0 - lam_init)
    ycat = jnp.concatenate([y1, y2], axis=1).astype(BF16)
    o_ref[0] = x_ref[0] + g1_ref[0] * _dot(ycat, w_ref[...])


def _outproj(o_f, o_b, gate, d_lat, x, g1, gdn_norm_w, subln_w, w_out, *, lam_init, n_ctx, tm):
    bsz, n_lat, d = x.shape
    off = n_ctx // tm
    tok = lambda b, i: (b, i, 0)
    tok_off = lambda b, i: (b, i + off, 0)
    full = lambda shape: pl.BlockSpec(shape, lambda b, i: (0,) * len(shape))
    return pl.pallas_call(
        functools.partial(_outproj_kernel, lam_init=lam_init),
        grid=(bsz, n_lat // tm),
        in_specs=[pl.BlockSpec((1, tm, A_W), tok_off)] * 3
        + [pl.BlockSpec((1, tm, B_W), tok), pl.BlockSpec((1, tm, d), tok),
           pl.BlockSpec((1, 1, d), lambda b, i: (b, 0, 0)),
           full((1, A_W)), full((1, B_W)), full((A_W, A_W)), full((B_W, B_W)), full(w_out.shape)],
        out_specs=pl.BlockSpec((1, tm, d), tok),
        out_shape=jax.ShapeDtypeStruct((bsz, n_lat, d), F32),
        compiler_params=_cparams("parallel", "arbitrary"),
    )(o_f, o_b, gate, d_lat, x, g1, jnp.tile(gdn_norm_w, A_HEADS).reshape(1, A_W),
      jnp.tile(subln_w, B_HEADS).reshape(1, B_W), _block_ones(A_W, A_DV), _block_ones(B_W, 2 * B_DH),
      w_out.astype(BF16))


def _dft1_kernel(x_ref, nw_ref, sc_ref, sh_ref, f_ref, twr_ref, twi_ref, re_ref, im_ref, *, d, passes):
    n1 = f_ref.shape[1]
    f = f_ref[...]
    for t in range(re_ref.shape[2]):
        h = _normmod(x_ref[0, :, t * d:(t + 1) * d], nw_ref[...], sc_ref[0], sh_ref[0])
        a = _mm(f, h, passes)
        a_re, a_im = a[:n1], a[n1:]
        twr = jnp.concatenate([twr_ref[:, t * 128:(t + 1) * 128]] * (d // 128), axis=1)
        twi = jnp.concatenate([twi_ref[:, t * 128:(t + 1) * 128]] * (d // 128), axis=1)
        re_ref[0, :, t, :] = a_re * twr - a_im * twi
        im_ref[0, :, t, :] = a_re * twi + a_im * twr


def _dft2_kernel(re_ref, im_ref, x_ref, g1_ref, f_ref, cc_ref, sc_ref, w_ref, o_ref, *, d, passes):
    n2 = f_ref.shape[1]
    f = f_ref[...]
    dg = d // C_GROUPS
    for t in range(re_ref.shape[1]):
        pr = _mm(f, re_ref[0, t], passes)
        pi = _mm(f, im_ref[0, t], passes)
        x_re = pr[:n2] + pi[n2:]
        x_im = pi[:n2] - pr[n2:]
        y = jnp.concatenate(
            [_mm(x_re[:, g * dg:(g + 1) * dg], cc_ref[...], passes)
             + _mm(x_im[:, g * dg:(g + 1) * dg], sc_ref[...], passes) for g in range(C_GROUPS)], axis=1)
        o_ref[0, :, t, :] = x_ref[0, :, t, :] + g1_ref[0] * _dot(y.astype(BF16), w_ref[...])


def _fourier_layer(x, nw, sc, sh, g1, w_out, *, passes=3, nb=8, kb=8):
    bsz, n, d = x.shape
    n1, n2 = DFT_N1, n // DFT_N1
    dg = d // C_GROUPS
    k1 = np.arange(n1)
    ang1 = 2 * np.pi * np.outer(k1, k1) / n1
    f1 = jnp.asarray(np.concatenate([np.cos(ang1), -np.sin(ang1)], 0), F32)
    angt = 2 * np.pi * np.outer(k1, np.arange(n2)) / n
    twr = jnp.asarray(np.repeat(np.cos(angt), 128, axis=1), F32)
    twi = jnp.asarray(np.repeat(-np.sin(angt), 128, axis=1), F32)
    k2 = np.arange(n2)
    ang2 = 2 * np.pi * np.outer(k2, k2) / n2
    f2 = jnp.asarray(np.concatenate([np.cos(ang2), np.sin(ang2)], 0), F32)
    angc = 2 * np.pi * np.outer(np.arange(dg), np.arange(dg)) / dg
    scale = 1.0 / math.sqrt(n * dg)
    cc = jnp.asarray(np.cos(angc) * scale, F32)
    sn = jnp.asarray(np.sin(angc) * scale, F32)

    assert n2 % SUB == 0 and nb % SUB == 0 and kb % SUB == 0
    win1 = pl.BlockSpec((1, n1, nb, d), lambda b, j: (b, 0, j, 0))
    vec = pl.BlockSpec((1, 1, d), lambda b, j: (b, 0, 0))
    full = lambda shape: pl.BlockSpec(shape, lambda b, j: (0,) * len(shape))
    a_re, a_im = pl.pallas_call(
        functools.partial(_dft1_kernel, d=d, passes=passes),
        grid=(bsz, n2 // nb),
        in_specs=[pl.BlockSpec((1, n1, nb * d), lambda b, j: (b, 0, j)), full((1, d)), vec, vec, full(f1.shape),
                  pl.BlockSpec((n1, nb * 128), lambda b, j: (0, j)),
                  pl.BlockSpec((n1, nb * 128), lambda b, j: (0, j))],
        out_specs=[win1] * 2,
        out_shape=[jax.ShapeDtypeStruct((bsz, n1, n2, d), F32)] * 2,
        compiler_params=_cparams("parallel", "arbitrary"),
    )(x.reshape(bsz, n1, n2 * d), nw.reshape(1, d), sc, sh, f1, twr, twi)

    win2 = pl.BlockSpec((1, n2, kb, d), lambda b, j: (b, 0, j, 0))
    out = pl.pallas_call(
        functools.partial(_dft2_kernel, d=d, passes=passes),
        grid=(bsz, n1 // kb),
        in_specs=[pl.BlockSpec((1, kb, n2, d), lambda b, j: (b, j, 0, 0))] * 2
        + [win2, vec, full(f2.shape), full(cc.shape), full(sn.shape), full(w_out.shape)],
        out_specs=win2,
        out_shape=jax.ShapeDtypeStruct((bsz, n2, n1, d), F32),
        compiler_params=_cparams("parallel", "arbitrary"),
    )(a_re, a_im, x.reshape(bsz, n2, n1, d), g1, f2, cc, sn, w_out.astype(BF16))
    return out.reshape(bsz, n, d)


def _top16(s, payload=None, order=None):
    rows = lax.broadcasted_iota(I32, s.shape, 0).astype(F32) if order is None else order
    big = 1e9
    vals, idxs = [], []
    for _ in range(PEER_TOPK):
        m = jnp.max(s, axis=0, keepdims=True)
        am = jnp.min(jnp.where(s == m, rows, big), axis=0, keepdims=True)
        sel = rows == am
        vals.append(m)
        if payload is None:
            idxs.append(am)
        else:
            idxs.append(jnp.max(jnp.where(sel, payload, -1.0), axis=0, keepdims=True))
        s = jnp.where(sel, -jnp.inf, s)
    return jnp.concatenate(vals, axis=0), jnp.concatenate(idxs, axis=0)


def _staircase_candidates(s1, i1, s2, i2):
    assert PEER_TOPK == 16 and SUB == 8
    sub_i = lax.broadcasted_iota(I32, (SUB,) + s1.shape[1:], 0)
    sub = sub_i.astype(F32)
    lo8 = lambda t: t[:SUB]
    rot = lambda t, k: pltpu.roll(t, k, 0)

    def pack(v1, v2, comb, pad):
        a_hi = rot(v1[SUB:], 2)
        return [
            comb(v1[0:1], v2),
            comb(v1[1:2], lo8(v2)),
            jnp.where(sub_i < 5, comb(v1[2:3], lo8(v2)), comb(v1[4:5], rot(lo8(v2), 5))),
            jnp.where(sub_i < 4, comb(v1[3:4], lo8(v2)),
                      jnp.where(sub_i < 6, comb(v1[5:6], rot(lo8(v2), 4)), comb(v1[6:7], rot(lo8(v2), 6)))),
            jnp.where(sub_i < 2, comb(v1[7:8], lo8(v2)), comb(a_hi, v2[0:1])),
            jnp.where(sub_i < 2, comb(a_hi, v2[0:1]), pad),
        ]

    cand = jnp.concatenate(pack(s1, s2, lambda x, y: x + y, -jnp.inf), axis=0)
    cand_i = jnp.concatenate(pack(i1, i2, lambda x, y: x * PEER_NKEYS + y, -1.0), axis=0)
    order = jnp.concatenate([
        sub, sub + SUB, 16.0 + sub,
        jnp.where(sub_i < 5, 32.0 + sub, 64.0 + (sub - 5.0)),
        jnp.where(sub_i < 4, 48.0 + sub, jnp.where(sub_i < 6, 80.0 + (sub - 4.0), 96.0 + (sub - 6.0))),
        jnp.where(sub_i < 2, 112.0 + sub, 16.0 * (sub + 6.0)),
        jnp.where(sub_i < 2, 16.0 * (sub + 14.0), 1e6 + sub)], axis=0)
    return cand, cand_i, order


def _peer_topk_kernel(x_ref, nw_ref, sc_ref, sh_ref, wh_ref, wl_ref, keys_ref,
                      h_out, row_out, par_out, gate_out):
    h = _normmod(x_ref[0], nw_ref[...], sc_ref[0], sh_ref[0])
    h_out[0] = h
    hh, hl = _split(h)
    q = _dot(hh, wh_ref[...]) + (_dot(hh, wl_ref[...]) + _dot(hl, wh_ref[...]))
    half = PEER_DKEY // 2
    idx_rows, gate_rows = [], []
    for hd in range(PEER_HEADS):
        tops = []
        for p in range(2):
            j = hd * 2 + p
            st = _mm(keys_ref[j], q[:, j * half:(j + 1) * half], 3, _dot_nt)
            tops.append(_top16(st))
        (s1, i1), (s2, i2) = tops
        top_s, top_i = _top16(*_staircase_candidates(s1, i1, s2, i2))
        ex = jnp.exp(top_s - top_s[0:1])
        gate_rows.append(ex / jnp.sum(ex, axis=0, keepdims=True))
        idx_rows.append(top_i)
    idx = jnp.concatenate(idx_rows, axis=0).T.astype(I32)
    n_pairs = PEER_NKEYS * PEER_NKEYS // 2
    row_out[0] = (idx & (n_pairs - 1)) * SUB
    par_out[0] = (idx >= n_pairs).astype(I32)
    gate_out[0] = jnp.concatenate(gate_rows, axis=0).T


def _peer_topk(x, nw, sc, sh, w_q, keys, *, tb):
    bsz, n, d = x.shape
    nk = PEER_HEADS * PEER_TOPK
    wh, wl = _split_outside(w_q)
    keys2 = keys.reshape(PEER_HEADS * 2, PEER_NKEYS, PEER_DKEY // 2)
    tok = lambda b, i: (b, i, 0)
    vec = pl.BlockSpec((1, 1, d), lambda b, i: (b, 0, 0))
    full = lambda shape: pl.BlockSpec(shape, lambda b, i: (0,) * len(shape))
    return pl.pallas_call(
        _peer_topk_kernel,
        grid=(bsz, n // tb),
        in_specs=[pl.BlockSpec((1, tb, d), tok), full((1, d)), vec, vec, full(wh.shape), full(wl.shape),
                  full(keys2.shape)],
        out_specs=[pl.BlockSpec((1, tb, d), tok)] + [pl.BlockSpec((1, tb, nk), tok)] * 3,
        out_shape=[jax.ShapeDtypeStruct((bsz, n, d), F32), jax.ShapeDtypeStruct((bsz, n, nk), I32),
                   jax.ShapeDtypeStruct((bsz, n, nk), I32), jax.ShapeDtypeStruct((bsz, n, nk), F32)],
        compiler_params=_cparams("parallel", "arbitrary"),
    )(x, nw.reshape(1, d), sc, sh, wh, wl, keys2)


def _pack_kernel(hi_ref, lo_ref, o_ref):
    pairs, d = hi_ref.shape
    bits = lambda v: lax.bitcast_convert_type(v.astype(BF16).astype(F32), U32)
    packed = bits(hi_ref[...]) | (bits(lo_ref[...]) >> 16)
    for s in range(d // 128):
        o_ref[pl.ds(s, pairs, stride=SUB), :] = packed[:, s * 128:(s + 1) * 128]


def _pack_table(tabs, layer, *, pairs=256):
    _, e, d = tabs.shape
    assert d // 128 == SUB
    nblk = e // 2 // pairs
    return pl.pallas_call(
        _pack_kernel,
        grid=(nblk,),
        in_specs=[pl.BlockSpec((None, pairs, d), lambda i: (layer, i, 0)),
                  pl.BlockSpec((None, pairs, d), lambda i: (layer, i + nblk, 0))],
        out_specs=pl.BlockSpec((pairs * SUB, 128), lambda i: (i, 0)),
        out_shape=jax.ShapeDtypeStruct((e // 2 * SUB, 128), U32),
        compiler_params=_cparams("parallel"),
    )(tabs, tabs)


def _gather_tiles(tab_ref, row_ref, t, nk):
    tiles = [tab_ref[pl.ds(pl.multiple_of(row_ref[t, kk], SUB), SUB), :] for kk in range(nk)]
    return pltpu.bitcast(jnp.concatenate(tiles, axis=0), BF16)


def _gather_constants(nk):
    c = np.arange(nk * PAIR_COLS)
    pair = (c[None, :] // PAIR_COLS == np.arange(nk)[:, None])
    hi = (c % 2 == 1)[None, :]
    smask = ((c[None, :] % PAIR_COLS) // 2 == np.arange(SUB)[:, None]).astype(np.float32)
    as_bf16 = lambda m: jnp.asarray(m.astype(np.float32), dtype=BF16)
    return dict(e_all=as_bf16(pair), e_hi=as_bf16(pair & hi), e_lo=as_bf16(pair & ~hi),
                collapse=as_bf16(pair.T), smask=jnp.asarray(smask), hi_row=jnp.asarray(hi.astype(np.float32)))


def _gelu_tanh(x):
    return 0.5 * x * (1.0 + jnp.tanh(math.sqrt(2.0 / math.pi) * (x + 0.044715 * (x * x * x))))


def _peer_act_kernel(row_ref, par_ref, h_ref, gate_ref, tab_ref, eall_ref, coll_ref, smask_ref, hi_ref,
                     w_out, rsum_ref, h8_ref):
    tb, nk = gate_ref.shape
    smask = smask_ref[...]
    for s in range(SUB):
        h8_ref[pl.ds(s, tb, stride=SUB), :] = h_ref[:, s * 128:(s + 1) * 128]

    def token(t, carry):
        g = _gather_tiles(tab_ref, row_ref, t, nk)
        hh, hl = _split(h8_ref[pl.ds(pl.multiple_of(t * SUB, SUB), SUB), :])
        r = _dot_nt(jnp.concatenate([hh, hl], axis=0), g)
        rsum_ref[pl.ds(t, 1), :] = jnp.sum((r[:SUB] + r[SUB:]) * smask, axis=0, keepdims=True)
        return carry

    lax.fori_loop(0, tb, token, 0, unroll=2 * GATHER_UNROLL)
    want_hi = 1.0 - _dot(par_ref[...].astype(F32).astype(BF16), eall_ref[...])
    picked = jnp.where(want_hi == hi_ref[...], rsum_ref[...], 0.0)
    w_out[...] = gate_ref[...] * _gelu_tanh(_mm_lhs2(picked, coll_ref[...]))


def _peer_mix_kernel(row_ref, par_ref, w_ref, x_ref, g2_ref, tab_ref, ehi_ref, elo_ref, smask_ref,
                     o_ref, wcols_ref, acc_ref):
    tb, nk = w_ref.shape
    smask = smask_ref[...]
    par = par_ref[...].astype(F32)
    w = w_ref[...]
    wcols_ref[...] = _mm_lhs2(w * (1.0 - par), ehi_ref[...]) + _mm_lhs2(w * par, elo_ref[...])

    def token(t, carry):
        g = _gather_tiles(tab_ref, row_ref, t, nk)
        wh, wl = _split(wcols_ref[pl.ds(t, 1), :] * smask)
        res = _dot(jnp.concatenate([wh, wl], axis=0), g)
        acc_ref[pl.ds(pl.multiple_of(t * SUB, SUB), SUB), :] = res[:SUB] + res[SUB:]
        return carry

    lax.fori_loop(0, tb, token, 0, unroll=GATHER_UNROLL)
    for s in range(SUB):
        cs = slice(s * 128, (s + 1) * 128)
        o_ref[:, cs] = x_ref[:, cs] + g2_ref[0][:, cs] * acc_ref[pl.ds(s, tb, stride=SUB), :]


def _peer_gather(x, h, row, par, gate, g2, u_tabs, v_tabs, layer, *, tb):
    bsz, n, d = x.shape
    u_tab, v_tab = _pack_table(u_tabs, layer), _pack_table(v_tabs, layer)
    t_all = bsz * n
    nk = row.shape[-1]
    sl = d // 128
    assert sl == SUB
    row2, par2 = row.reshape(t_all, nk), par.reshape(t_all, nk)
    cst = _gather_constants(nk)
    ncol = nk * PAIR_COLS
    tab_spec = pl.BlockSpec(u_tab.shape, lambda i: (0, 0), pipeline_mode=pl.Buffered(1))
    smem = pl.BlockSpec((tb, nk), lambda i: (i, 0), memory_space=pltpu.SMEM)
    tokk = pl.BlockSpec((tb, nk), lambda i: (i, 0))
    tokd = pl.BlockSpec((tb, d), lambda i: (i, 0))
    tiles = pltpu.VMEM((tb * sl, 128), F32)
    full = lambda a: pl.BlockSpec(a.shape, lambda i: (0,) * a.ndim)
    w = pl.pallas_call(
        _peer_act_kernel,
        grid=(t_all // tb,),
        in_specs=[smem, tokk, tokd, tokk, tab_spec, full(cst["e_all"]), full(cst["collapse"]),
                  full(cst["smask"]), full(cst["hi_row"])],
        out_specs=tokk,
        out_shape=jax.ShapeDtypeStruct((t_all, nk), F32),
        scratch_shapes=[pltpu.VMEM((tb, ncol), F32), tiles],
        compiler_params=_cparams("arbitrary"),
    )(row2, par2, h.reshape(t_all, d), gate.reshape(t_all, nk), u_tab,
      cst["e_all"], cst["collapse"], cst["smask"], cst["hi_row"])
    out = pl.pallas_call(
        _peer_mix_kernel,
        grid=(t_all // tb,),
        in_specs=[smem, tokk, tokk, tokd, pl.BlockSpec((1, 1, d), lambda i: ((i * tb) // n, 0, 0)),
                  tab_spec, full(cst["e_hi"]), full(cst["e_lo"]), full(cst["smask"])],
        out_specs=tokd,
        out_shape=jax.ShapeDtypeStruct((t_all, d), F32),
        scratch_shapes=[pltpu.VMEM((tb, ncol), F32), tiles],
        compiler_params=_cparams("arbitrary"),
    )(row2, par2, w, x.reshape(t_all, d), g2, v_tab,
      cst["e_hi"], cst["e_lo"], cst["smask"])
    return out.reshape(bsz, n, d)


def _peer_layer(x, nw, sc, sh, g2, w_q, keys, u_tabs, v_tabs, layer):
    h, row, par, gate = _peer_topk(x, nw, sc, sh, w_q, keys, tb=128)
    return _peer_gather(x, h, row, par, gate, g2, u_tabs, v_tabs, layer, tb=128)


def _final_norm_kernel(x_ref, w_ref, o_ref):
    x = x_ref[0]
    o_ref[0] = x * lax.rsqrt(jnp.mean(x * x, axis=-1, keepdims=True) + EPS) * w_ref[...]


def _final_norm(x, w, *, tm):
    bsz, n, d = x.shape
    return pl.pallas_call(
        _final_norm_kernel,
        grid=(bsz, n // tm),
        in_specs=[pl.BlockSpec((1, tm, d), lambda b, i: (b, i, 0)), pl.BlockSpec((1, d), lambda b, i: (0, 0))],
        out_specs=pl.BlockSpec((1, tm, d), lambda b, i: (b, i, 0)),
        out_shape=jax.ShapeDtypeStruct((bsz, n, d), F32),
        compiler_params=_cparams("parallel", "arbitrary"),
    )(x, w.reshape(1, d))


def _rope_tables(n_ctx, n_lat):
    pos = np.arange(n_lat)
    axis_dim = B_DH // 2
    inv = ROPE_BASE ** (-np.arange(0, axis_dim, 2, dtype=np.float32) / axis_dim)
    ang = np.concatenate([(pos // GRID_W)[:, None] * inv, (pos % GRID_W)[:, None] * inv], axis=-1)
    ang = np.concatenate([np.zeros((n_ctx, ang.shape[1])), ang], axis=0).astype(np.float32)
    cos = np.cos(ang)
    sin = np.sin(ang)
    cos64 = np.concatenate([cos, cos], axis=1)
    sin64 = np.concatenate([-sin, sin], axis=1)
    return (jnp.asarray(np.tile(cos64, (1, 2)), F32), jnp.asarray(np.tile(sin64, (1, 2)), F32))


def _mixer_ab_layer(x, ctx, mod, mod_c, layer, norm1_w, w_in, conv_w, a_log, dt_bias, gdn_norm_w,
                    lam_params, subln_w, w_out):
    bsz, n_lat, d = x.shape
    n_ctx = ctx.shape[1]
    n = n_ctx + n_lat
    tm = GDN_BLOCK
    lam_init = 0.8 - 0.6 * math.exp(-0.3 * layer)

    assert n_ctx % tm == 0 and n_lat % tm == 0
    seg = lambda j: jnp.stack([jnp.broadcast_to(mod_c[j], (bsz, d)), mod[:, j]], axis=1)[:, :, None, :]
    cuts = np.cumsum([0, 3 * A_W, A_W, 4 * A_HEADS, B_W, B_W, B_W])
    weights = [w_in[:, a:b] for a, b in zip(cuts[:-1], cuts[1:])]
    qkv, gate, ba, bq, bk, bv, edges = _in_proj(
        ctx, x, norm1_w, seg(1), seg(0), weights, [F32, F32, F32, F32, F32, BF16], tm=tm)

    first, last = edges[:, :, 0], edges[:, :, 1]
    zero = jnp.zeros_like(last[:, :1])
    prev_rows = jnp.concatenate([zero, last[:, :-1]], axis=1)
    next_rows = jnp.concatenate([first[:, 1:], zero], axis=1)
    blk = jnp.arange(n // tm)[None, :, None]
    prev_rows = jnp.where(blk == n_ctx // tm, 0.0, prev_rows)[:, :, None, :]
    next_rows = jnp.where(blk == n_ctx // tm - 1, 0.0, next_rows)[:, :, None, :]
    cos128, sin128 = _rope_tables(n_ctx, n_lat)
    q, k, v, bqr, bkr = _prep(qkv, prev_rows, next_rows, conv_w, bq, bk, cos128, sin128, tm=tm, n_ctx=n_ctx)

    bat = jnp.swapaxes(ba, 1, 2)
    o_f = _gdn(q, k, v, ba, bat, a_log, dt_bias, d=0, n_ctx=n_ctx, inv_passes=1)
    o_b = _gdn(q, k, v, ba, bat, a_log, dt_bias, d=1, n_ctx=n_ctx, inv_passes=1)
    d_lat = _diff_attention(bqr, bkr, jnp.swapaxes(bv, 1, 2), lam_params, lam_init=lam_init, tq=512,
                            tk=ATTN_TK if n % ATTN_TK == 0 else n)
    return _outproj(o_f, o_b, gate, d_lat, x, mod[:, 2][:, None, :], gdn_norm_w, subln_w, w_out,
                    lam_init=lam_init, n_ctx=n_ctx, tm=tm)


def kernel(x, c, ctx, c_ctx, ada_w, ada_b, norm1_w, norm2_w, w_in, conv_w, a_log, dt_bias, gdn_norm_w,
           lam_q1, lam_k1, lam_q2, lam_k2, subln_w, w_out_ab, w_out_f, peer_wq, peer_keys, peer_u,
           peer_v, final_norm_w):
    bsz, n_lat, d = x.shape
    depth = ada_w.shape[0]
    last_ctx_reader = 2 * ((depth - 1) // 2)
    cmat = jnp.concatenate([c, c_ctx[None, :], jnp.zeros((8 - bsz - 1, d), F32)], axis=0)
    for i in range(depth):
        if i < last_ctx_reader:
            raise NotImplementedError("context stream advance (depth > 2) is not implemented")
        mod_all = _modulation(cmat, ada_w[i], ada_b[i])
        mod = mod_all[:bsz].reshape(bsz, 6, d)
        mod_c = mod_all[bsz].reshape(6, d)
        vec = lambda j: mod[:, j][:, None, :]
        j = i // 2
        if i % 2 == 0:
            x = _mixer_ab_layer(x, ctx, mod, mod_c, i, norm1_w[i], w_in[j], conv_w[j], a_log[j], dt_bias[j],
                                gdn_norm_w[j], (lam_q1[j], lam_k1[j], lam_q2[j], lam_k2[j]), subln_w[j],
                                w_out_ab[j])
        else:
            x = _fourier_layer(x, norm1_w[i], vec(1), vec(0), vec(2), w_out_f[j])
        x = _peer_layer(x, norm2_w[i], vec(4), vec(3), vec(5), peer_wq[i], peer_keys[i], peer_u, peer_v, i)
    return _final_norm(x, final_norm_w, tm=512)
```

```python
import functools
import math

import numpy as np
import jax
import jax.numpy as jnp
from jax import lax
from jax.experimental import pallas as pl
from jax.experimental.pallas import tpu as pltpu

F32 = jnp.float32
BF16 = jnp.bfloat16
I32 = jnp.int32
U32 = jnp.uint32
HIGHEST = lax.Precision.HIGHEST

EPS = 1e-6
ROPE_BASE = 10000.0
GRID_W = 64
A_HEADS, A_DK, A_DV, A_CHUNK = 8, 64, 64, 64
B_HEADS, B_DH = 4, 64
C_GROUPS = 4
PEER_HEADS, PEER_NKEYS, PEER_DKEY, PEER_TOPK = 8, 128, 256, 16
A_W = A_HEADS * A_DK
B_W = B_HEADS * 2 * B_DH
GDN_BLOCK = 256
GDN_HEAD_GROUP = 8
ATTN_SLAB = 64
ATTN_TK = 1408
DFT_N1 = 128
SUB = 8
PAIR_COLS = 2 * SUB
GATHER_UNROLL = 8

VMEM_LIMIT_BYTES = 48 * 1024 * 1024


def _cparams(*sem):
    return pltpu.CompilerParams(dimension_semantics=sem, vmem_limit_bytes=VMEM_LIMIT_BYTES)


def _dot(a, b):
    return lax.dot_general(a, b, (((1,), (0,)), ((), ())), preferred_element_type=F32)


def _dot_nt(a, b):
    return lax.dot_general(a, b, (((1,), (1,)), ((), ())), preferred_element_type=F32)


def _dot_tn(a, b):
    return lax.dot_general(a, b, (((0,), (0,)), ((), ())), preferred_element_type=F32)


def _dot_f32(a, b):
    return lax.dot_general(a, b, (((1,), (0,)), ((), ())), precision=HIGHEST,
                           preferred_element_type=F32)


def _split(a):
    hi = a.astype(BF16)
    lo = (a - hi.astype(F32)).astype(BF16)
    return hi, lo


def _mm(a, b, passes=1, dot=_dot):
    if passes == 1:
        return dot(a.astype(BF16), b.astype(BF16))
    ah, al = _split(a)
    bh, bl = _split(b)
    return dot(ah, bh) + (dot(ah, bl) + dot(al, bh))


def _split3(a):
    hi = a.astype(BF16)
    r = a - hi.astype(F32)
    mid = r.astype(BF16)
    return hi, mid, (r - mid.astype(F32)).astype(BF16)


def _mm_lhs2(a, b_bf16):
    ah, al = _split(a)
    return _dot(ah, b_bf16) + _dot(al, b_bf16)


def _silu(x):
    return x * jax.nn.sigmoid(x)


def _softplus(x):
    return jnp.maximum(x, 0.0) + jnp.log(1.0 + jnp.exp(-jnp.abs(x)))


def _normmod(x, nw, sc, sh):
    ms = jnp.mean(x * x, axis=-1, keepdims=True)
    return (x * lax.rsqrt(ms + EPS) * nw) * (1.0 + sc) + sh


def _split_outside(w):
    hi = lax.bitcast_convert_type(lax.bitcast_convert_type(w, U32) & jnp.uint32(0xFFFF0000), F32)
    return hi.astype(BF16), (w - hi).astype(BF16)


def _block_ones(n, blk):
    i = np.arange(n) // blk
    return jnp.asarray((i[:, None] == i[None, :]).astype(np.float32), dtype=BF16)


def _mod_kernel(c_ref, w_ref, b_ref, o_ref):
    o_ref[...] = _dot_f32(_silu(c_ref[...]), w_ref[...]) + b_ref[...]


def _modulation(cmat, w, b):
    rows, d = cmat.shape
    n = w.shape[1]
    return pl.pallas_call(
        _mod_kernel,
        grid=(n // d,),
        in_specs=[pl.BlockSpec((rows, d), lambda j: (0, 0)),
                  pl.BlockSpec((d, d), lambda j: (0, j)),
                  pl.BlockSpec((1, d), lambda j: (0, j))],
        out_specs=pl.BlockSpec((rows, d), lambda j: (0, j)),
        out_shape=jax.ShapeDtypeStruct((rows, n), F32),
        compiler_params=_cparams("arbitrary"),
    )(cmat, w, b.reshape(1, n))


def _in_proj_kernel(*refs, n_w, n_first):
    ctx_ref, x_ref, nw_ref, sc_ref, sh_ref = refs[:5]
    w_refs = refs[5:5 + n_w]
    outs = refs[5 + n_w:5 + 2 * n_w]
    edge_ref = refs[5 + 2 * n_w]
    xin = jnp.where(pl.program_id(1) < n_first, ctx_ref[0], x_ref[0])
    hb = _normmod(xin, nw_ref[...], sc_ref[0, 0], sh_ref[0, 0]).astype(BF16)
    for j in range(n_w):
        y = _dot(hb, w_refs[j][...])
        outs[j][0] = y.astype(outs[j].dtype)
        if j == 0:
            tm = y.shape[0]
            edge_ref[0, 0] = jnp.concatenate(
                [y[0:1], y[tm - 1:tm], jnp.zeros((SUB - 2, y.shape[1]), F32)], axis=0)


def _in_proj(ctx, x, nw, sc, sh, weights, out_dtypes, *, tm):
    bsz, n_lat, d = x.shape
    n_first = ctx.shape[1] // tm
    nblk = n_first + n_lat // tm
    widths = tuple(w.shape[1] for w in weights)
    w_in = [w.astype(BF16) for w in weights]
    mod_map = lambda b, i: (b, jnp.where(i < n_first, 0, 1), 0, 0)
    tok = lambda b, i: (b, i, 0)
    in_specs = [pl.BlockSpec((1, tm, d), lambda b, i: (b, jnp.minimum(i, n_first - 1), 0)),
                pl.BlockSpec((1, tm, d), lambda b, i: (b, jnp.maximum(i - n_first, 0), 0)),
                pl.BlockSpec((1, d), lambda b, i: (0, 0)),
                pl.BlockSpec((1, 1, 1, d), mod_map), pl.BlockSpec((1, 1, 1, d), mod_map)]
    in_specs += [pl.BlockSpec(w.shape, lambda b, i: (0, 0)) for w in w_in]
    out_specs = [pl.BlockSpec((1, tm, wd), tok) for wd in widths]
    out_specs.append(pl.BlockSpec((1, 1, SUB, widths[0]), lambda b, i: (b, i, 0, 0)))
    out_shape = [jax.ShapeDtypeStruct((bsz, nblk * tm, wd), dt) for wd, dt in zip(widths, out_dtypes)]
    out_shape.append(jax.ShapeDtypeStruct((bsz, nblk, SUB, widths[0]), F32))
    return pl.pallas_call(
        functools.partial(_in_proj_kernel, n_w=len(w_in), n_first=n_first),
        grid=(bsz, nblk),
        in_specs=in_specs, out_specs=out_specs, out_shape=out_shape,
        compiler_params=_cparams("parallel", "arbitrary"),
    )(ctx, x, nw.reshape(1, d), sc, sh, *w_in)


def _prep_kernel(qkv_ref, prev_ref, next_ref, cw_ref, bd_ref, bq_ref, bk_ref, cos_ref, sin_ref,
                 q_out, k_out, v_out, bqr_out, bkr_out):
    x = qkv_ref[0]
    tm = x.shape[0]
    row = lax.broadcasted_iota(I32, x.shape, 0)
    xm = jnp.where(row == 0, prev_ref[0, 0], pltpu.roll(x, 1, 0))
    xp = jnp.where(row == tm - 1, next_ref[0, 0], pltpu.roll(x, tm - 1, 0))
    cw = cw_ref[...]
    y = _silu(xm * cw[0:1] + x * cw[1:2] + xp * cw[2:3])
    bd = bd_ref[...]

    def l2(t):
        return t * lax.rsqrt(_mm_lhs2(t * t, bd) + EPS)

    q_out[0] = l2(y[:, :A_W]) * (A_DK ** -0.5)
    k_out[0] = l2(y[:, A_W:2 * A_W])
    v_out[0] = y[:, 2 * A_W:]

    cos = jnp.concatenate([cos_ref[...]] * (B_W // 128), axis=1)
    sin = jnp.concatenate([sin_ref[...]] * (B_W // 128), axis=1)
    lane = lax.broadcasted_iota(I32, (tm, B_W), 1)
    first = (lane & (B_DH - 1)) < (B_DH // 2)

    def rope(t):
        rot = jnp.where(first, pltpu.roll(t, B_W - B_DH // 2, 1), pltpu.roll(t, B_DH // 2, 1))
        return t * cos + rot * sin

    bqr_out[0] = (rope(bq_ref[0]) * (B_DH ** -0.5)).astype(BF16)
    bkr_out[0] = rope(bk_ref[0]).astype(BF16)


def _prep(qkv, prev_rows, next_rows, conv_w, bq, bk, cos128, sin128, *, tm, n_ctx):
    bsz, n, cw = qkv.shape
    tok = lambda b, i: (b, i, 0)
    lat = lambda b, i: (b, jnp.maximum(i - n_ctx // tm, 0), 0)
    halo = pl.BlockSpec((1, 1, 1, cw), lambda b, i: (b, i, 0, 0))
    return pl.pallas_call(
        _prep_kernel,
        grid=(bsz, n // tm),
        in_specs=[pl.BlockSpec((1, tm, cw), tok), halo, halo,
                  pl.BlockSpec(conv_w.shape, lambda b, i: (0, 0)),
                  pl.BlockSpec((A_W, A_W), lambda b, i: (0, 0)),
                  pl.BlockSpec((1, tm, B_W), tok), pl.BlockSpec((1, tm, B_W), tok),
                  pl.BlockSpec((tm, 128), lambda b, i: (i, 0)),
                  pl.BlockSpec((tm, 128), lambda b, i: (i, 0))],
        out_specs=[pl.BlockSpec((1, tm, A_W), tok)] * 3
        + [pl.BlockSpec((1, tm, B_W), lat), pl.BlockSpec((1, tm, B_W), tok)],
        out_shape=[jax.ShapeDtypeStruct((bsz, n, A_W), F32)] * 3
        + [jax.ShapeDtypeStruct((bsz, n - n_ctx, B_W), BF16), jax.ShapeDtypeStruct((bsz, n, B_W), BF16)],
        compiler_params=_cparams("parallel", "arbitrary"),
    )(qkv, prev_rows, next_rows, conv_w, _block_ones(A_W, A_DK), bq, bk, cos128, sin128)


def _gdn_kernel(q_ref, k_ref, v_ref, ba_ref, bat_ref, alr_ref, dtr_ref, alc_ref, dtc_ref,
                lm_ref, lmt_ref, ltot_ref, e_ref, o_ref, s_ref, *, d, inv_passes):
    rev = d == 1
    nh = A_HEADS

    @pl.when(pl.program_id(1) == 0)
    def _():
        s_ref[...] = jnp.zeros_like(s_ref)

    q, k, v = q_ref[0], k_ref[0], v_ref[0]
    ba, bat = ba_ref[0], bat_ref[0]
    n = q.shape[0]
    beta_c = jax.nn.sigmoid(ba[:, d * nh:(d + 1) * nh])
    g_c = -jnp.exp(alr_ref[...]) * _softplus(ba[:, 2 * nh + d * nh:2 * nh + (d + 1) * nh] + dtr_ref[...])
    g_t = -jnp.exp(alc_ref[...]) * _softplus(bat[2 * nh + d * nh:2 * nh + (d + 1) * nh, :] + dtc_ref[...])
    lm = lm_ref[...]
    gs = _split3(g_c)
    gcum_c = sum(_dot(lm.astype(BF16), p) for p in gs)
    gtot_c = sum(_dot(ltot_ref[...].astype(BF16), p) for p in gs)
    gcum_t = sum(_dot(p, lmt_ref[...].astype(BF16)) for p in _split3(g_t))
    small = jnp.concatenate([jnp.exp(gcum_c), jnp.exp(gtot_c - gcum_c), jnp.exp(gtot_c), beta_c], axis=1)
    wide = _mm_lhs2(small, e_ref[...])
    eg, ekt, egt, beta_e = (wide[:, j * A_W:(j + 1) * A_W] for j in range(4))
    kb = k * beta_e
    vb = v * beta_e
    kbe = kb * eg
    qd = q * eg
    ktail = k * ekt

    ri = lax.broadcasted_iota(I32, (n, n), 0)
    ci = lax.broadcasted_iota(I32, (n, n), 1)
    eye = (ri == ci).astype(F32)
    same = lambda s: (ri >> s) == (ci >> s)
    d16 = same(4).astype(F32)
    o32 = (same(5) & ~same(4)).astype(F32)
    o64 = (same(6) & ~same(5)).astype(F32)
    incl = lm > 0.0
    strict = lm * (1.0 - eye)

    order = range(n // A_CHUNK - 1, -1, -1) if rev else range(n // A_CHUNK)
    sls = [slice(h * A_DK, (h + 1) * A_DK) for h in range(nh)]
    heads = [None] * nh
    for g0 in range(0, nh, GDN_HEAD_GROUP):
        hs = range(g0, g0 + GDN_HEAD_GROUP)
        decay = {h: jnp.exp(jnp.where(incl, gcum_c[:, h:h + 1] - gcum_t[h:h + 1, :], -1e30)) for h in hs}
        a = {h: _mm(kb[:, sls[h]], k[:, sls[h]], 1, _dot_nt) * decay[h] * strict for h in hs}
        p = {h: -(a[h] * d16) for h in hs}
        x = {h: eye + p[h] for h in hs}
        for _ in range(3):
            p = {h: _mm(p[h], p[h], inv_passes) for h in hs}
            x = {h: x[h] + _mm(x[h], p[h], inv_passes) for h in hs}
        for om in (o32, o64):
            y = {h: _mm(x[h], a[h] * om, inv_passes) for h in hs}
            x = {h: x[h] - _mm(y[h], x[h], inv_passes) for h in hs}
        uw = {h: _mm(x[h], jnp.concatenate([vb[:, sls[h]], kbe[:, sls[h]]], axis=1)) for h in hs}
        attn = {h: _mm(q[:, sls[h]], k[:, sls[h]], 1, _dot_nt) * decay[h] for h in hs}
        s = {h: s_ref[h] for h in hs}
        parts = {h: {} for h in hs}
        for c in order:
            r = slice(c * A_CHUNK, (c + 1) * A_CHUNK)
            v_new = {h: uw[h][r, :A_DV] - _mm(uw[h][r, A_DV:], s[h]) for h in hs}
            for h in hs:
                parts[h][c] = _mm(qd[r, sls[h]], s[h]) + _mm(attn[h][r, r], v_new[h])
            s = {h: s[h] * egt[c * A_CHUNK:c * A_CHUNK + 1, sls[h]]
                 + _mm(ktail[r, sls[h]], v_new[h], 1, _dot_tn) for h in hs}
        for h in hs:
            s_ref[h] = s[h]
            heads[h] = jnp.concatenate([parts[h][c] for c in range(n // A_CHUNK)], axis=0)
    o_ref[0] = jnp.concatenate(heads, axis=1)


def _gdn_constants(d):
    n = GDN_BLOCK
    i = np.arange(n)
    same = (i[:, None] // A_CHUNK) == (i[None, :] // A_CHUNK)
    tri = (i[None, :] >= i[:, None]) if d == 1 else (i[None, :] <= i[:, None])
    lm = (same & tri).astype(np.float32)
    e = (np.arange(A_W)[None, :] // A_DK == np.arange(A_HEADS)[:, None]).astype(np.float32)
    e4 = np.kron(np.eye(4, dtype=np.float32), e)
    return (jnp.asarray(lm), jnp.asarray(lm.T.copy()), jnp.asarray(same.astype(np.float32)),
            jnp.asarray(e4, dtype=BF16))


def _gdn(q, k, v, ba, bat, a_log, dt_bias, *, d, n_ctx, inv_passes):
    bsz, n, _ = q.shape
    nb, nc = n // GDN_BLOCK, n_ctx // GDN_BLOCK
    blk = (lambda s: jnp.where(s < nc, nc - 1 - s, nb - 1 + nc - s)) if d == 1 else (lambda s: s)
    tok = lambda b, s: (b, blk(s), 0)
    full = lambda shape: pl.BlockSpec(shape, lambda b, s: (0,) * len(shape))
    lm, lmt, ltot, e = _gdn_constants(d)
    return pl.pallas_call(
        functools.partial(_gdn_kernel, d=d, inv_passes=inv_passes),
        grid=(bsz, nb),
        in_specs=[pl.BlockSpec((1, GDN_BLOCK, A_W), tok)] * 3
        + [pl.BlockSpec((1, GDN_BLOCK, ba.shape[2]), tok),
           pl.BlockSpec((1, bat.shape[1], GDN_BLOCK), lambda b, s: (b, 0, blk(s))),
           full((1, A_HEADS)), full((1, A_HEADS)), full((A_HEADS, 1)), full((A_HEADS, 1)),
           full(lm.shape), full(lm.shape), full(lm.shape), full(e.shape)],
        out_specs=pl.BlockSpec((1, GDN_BLOCK, A_W), tok),
        out_shape=jax.ShapeDtypeStruct((bsz, n, A_W), F32),
        scratch_shapes=[pltpu.VMEM((A_HEADS, A_DK, A_DV), F32)],
        compiler_params=_cparams("parallel", "arbitrary"),
    )(q, k, v, ba, bat, a_log[d].reshape(1, -1), dt_bias[d].reshape(1, -1),
      a_log[d].reshape(-1, 1), dt_bias[d].reshape(-1, 1), lm, lmt, ltot, e)


def _attn_kernel(q_ref, k_ref, v_ref, lq1_ref, lk1_ref, lq2_ref, lk2_ref, o_ref, *, lam_init, tk):
    q = q_ref[0]
    tq = q.shape[0]
    nk = k_ref.shape[1]
    lam = (jnp.exp(jnp.sum(lq1_ref[...] * lk1_ref[...], keepdims=True))
           - jnp.exp(jnp.sum(lq2_ref[...] * lk2_ref[...], keepdims=True)) + lam_init)
    def over_keys(op, t):
        return op(op(t.reshape(tk // ATTN_SLAB, ATTN_SLAB, tq), axis=0), axis=0, keepdims=True)

    qs = [q[:, m * B_DH:(m + 1) * B_DH] for m in range(2)]
    mx = [jnp.full((1, tq), -jnp.inf, F32)] * 2
    l = [jnp.zeros((1, tq), F32)] * 2
    acc = [jnp.zeros((2 * B_DH, tq), F32)] * 2
    for c in range(nk // tk):
        ks = slice(c * tk, (c + 1) * tk)
        s = [_dot_nt(k_ref[0, ks, m * B_DH:(m + 1) * B_DH], qs[m]) for m in range(2)]
        m_new = [jnp.maximum(mx[m], over_keys(jnp.max, s[m])) for m in range(2)]
        alpha = [jnp.exp(mx[m] - m_new[m]) for m in range(2)]
        p = [jnp.exp(s[m] - m_new[m]) for m in range(2)]
        l = [alpha[m] * l[m] + over_keys(jnp.sum, p[m]) for m in range(2)]
        acc = [alpha[m] * acc[m] + _dot(v_ref[0, :, ks], p[m].astype(BF16)) for m in range(2)]
        mx = m_new
    o_ref[0] = (acc[0] / l[0] - lam * (acc[1] / l[1])).T


def _diff_attention(bqr, bkr, bv, lam_params, *, lam_init, tq, tk):
    bsz, n_lat, _ = bqr.shape
    n = bkr.shape[1]
    hw = 2 * B_DH
    assert n_lat % tq == 0 and n % tk == 0 and tk % ATTN_SLAB == 0
    lam_spec = pl.BlockSpec((1, B_DH), lambda b, h, i: (0, 0))
    return pl.pallas_call(
        functools.partial(_attn_kernel, lam_init=lam_init, tk=tk),
        grid=(bsz, B_HEADS, n_lat // tq),
        in_specs=[pl.BlockSpec((1, tq, hw), lambda b, h, i: (b, i, h)),
                  pl.BlockSpec((1, n, hw), lambda b, h, i: (b, 0, h)),
                  pl.BlockSpec((1, hw, n), lambda b, h, i: (b, h, 0))] + [lam_spec] * 4,
        out_specs=pl.BlockSpec((1, tq, hw), lambda b, h, i: (b, i, h)),
        out_shape=jax.ShapeDtypeStruct((bsz, n_lat, B_W), F32),
        compiler_params=_cparams("parallel", "parallel", "arbitrary"),
    )(bqr, bkr, bv, *[p.reshape(1, B_DH) for p in lam_params])


def _outproj_kernel(of_ref, ob_ref, gate_ref, d_ref, x_ref, g1_ref, gw_ref, sw_ref, bd64_ref,
                    bd128_ref, w_ref, o_ref, *, lam_init):
    o = of_ref[0] + ob_ref[0]
    ms = _mm_lhs2(o * o, bd64_ref[...]) * (1.0 / A_DV)
    y1 = (o * lax.rsqrt(ms + EPS) * gw_ref[...]) * _silu(gate_ref[0])
    dd = d_ref[0]
    ms2 = _mm_lhs2(dd * dd, bd128_ref[...]) * (1.0 / (2 * B_DH))
    y2 = (dd * lax.rsqrt(ms2 + EPS) * sw_ref[...]) * (1.0 - lam_init)
    ycat = jnp.concatenate([y1, y2], axis=1).astype(BF16)
    o_ref[0] = x_ref[0] + g1_ref[0] * _dot(ycat, w_ref[...])


def _outproj(o_f, o_b, gate, d_lat, x, g1, gdn_norm_w, subln_w, w_out, *, lam_init, n_ctx, tm):
    bsz, n_lat, d = x.shape
    off = n_ctx // tm
    tok = lambda b, i: (b, i, 0)
    tok_off = lambda b, i: (b, i + off, 0)
    full = lambda shape: pl.BlockSpec(shape, lambda b, i: (0,) * len(shape))
    return pl.pallas_call(
        functools.partial(_outproj_kernel, lam_init=lam_init),
        grid=(bsz, n_lat // tm),
        in_specs=[pl.BlockSpec((1, tm, A_W), tok_off)] * 3
        + [pl.BlockSpec((1, tm, B_W), tok), pl.BlockSpec((1, tm, d), tok),
           pl.BlockSpec((1, 1, d), lambda b, i: (b, 0, 0)),
           full((1, A_W)), full((1, B_W)), full((A_W, A_W)), full((B_W, B_W)), full(w_out.shape)],
        out_specs=pl.BlockSpec((1, tm, d), tok),
        out_shape=jax.ShapeDtypeStruct((bsz, n_lat, d), F32),
        compiler_params=_cparams("parallel", "arbitrary"),
    )(o_f, o_b, gate, d_lat, x, g1, jnp.tile(gdn_norm_w, A_HEADS).reshape(1, A_W),
      jnp.tile(subln_w, B_HEADS).reshape(1, B_W), _block_ones(A_W, A_DV), _block_ones(B_W, 2 * B_DH),
      w_out.astype(BF16))


def _dft1_kernel(x_ref, nw_ref, sc_ref, sh_ref, f_ref, twr_ref, twi_ref, re_ref, im_ref, *, d, passes):
    n1 = f_ref.shape[1]
    f = f_ref[...]
    for t in range(re_ref.shape[2]):
        h = _normmod(x_ref[0, :, t * d:(t + 1) * d], nw_ref[...], sc_ref[0], sh_ref[0])
        a = _mm(f, h, passes)
        a_re, a_im = a[:n1], a[n1:]
        twr = jnp.concatenate([twr_ref[:, t * 128:(t + 1) * 128]] * (d // 128), axis=1)
        twi = jnp.concatenate([twi_ref[:, t * 128:(t + 1) * 128]] * (d // 128), axis=1)
        re_ref[0, :, t, :] = a_re * twr - a_im * twi
        im_ref[0, :, t, :] = a_re * twi + a_im * twr


def _dft2_kernel(re_ref, im_ref, x_ref, g1_ref, f_ref, cc_ref, sc_ref, w_ref, o_ref, *, d, passes):
    n2 = f_ref.shape[1]
    f = f_ref[...]
    dg = d // C_GROUPS
    for t in range(re_ref.shape[1]):
        pr = _mm(f, re_ref[0, t], passes)
        pi = _mm(f, im_ref[0, t], passes)
        x_re = pr[:n2] + pi[n2:]
        x_im = pi[:n2] - pr[n2:]
        y = jnp.concatenate(
            [_mm(x_re[:, g * dg:(g + 1) * dg], cc_ref[...], passes)
             + _mm(x_im[:, g * dg:(g + 1) * dg], sc_ref[...], passes) for g in range(C_GROUPS)], axis=1)
        o_ref[0, :, t, :] = x_ref[0, :, t, :] + g1_ref[0] * _dot(y.astype(BF16), w_ref[...])


def _fourier_layer(x, nw, sc, sh, g1, w_out, *, passes=3, nb=8, kb=8):
    bsz, n, d = x.shape
    n1, n2 = DFT_N1, n // DFT_N1
    dg = d // C_GROUPS
    k1 = np.arange(n1)
    ang1 = 2 * np.pi * np.outer(k1, k1) / n1
    f1 = jnp.asarray(np.concatenate([np.cos(ang1), -np.sin(ang1)], 0), F32)
    angt = 2 * np.pi * np.outer(k1, np.arange(n2)) / n
    twr = jnp.asarray(np.repeat(np.cos(angt), 128, axis=1), F32)
    twi = jnp.asarray(np.repeat(-np.sin(angt), 128, axis=1), F32)
    k2 = np.arange(n2)
    ang2 = 2 * np.pi * np.outer(k2, k2) / n2
    f2 = jnp.asarray(np.concatenate([np.cos(ang2), np.sin(ang2)], 0), F32)
    angc = 2 * np.pi * np.outer(np.arange(dg), np.arange(dg)) / dg
    scale = 1.0 / math.sqrt(n * dg)
    cc = jnp.asarray(np.cos(angc) * scale, F32)
    sn = jnp.asarray(np.sin(angc) * scale, F32)

    assert n2 % SUB == 0 and nb % SUB == 0 and kb % SUB == 0
    win1 = pl.BlockSpec((1, n1, nb, d), lambda b, j: (b, 0, j, 0))
    vec = pl.BlockSpec((1, 1, d), lambda b, j: (b, 0, 0))
    full = lambda shape: pl.BlockSpec(shape, lambda b, j: (0,) * len(shape))
    a_re, a_im = pl.pallas_call(
        functools.partial(_dft1_kernel, d=d, passes=passes),
        grid=(bsz, n2 // nb),
        in_specs=[pl.BlockSpec((1, n1, nb * d), lambda b, j: (b, 0, j)), full((1, d)), vec, vec, full(f1.shape),
                  pl.BlockSpec((n1, nb * 128), lambda b, j: (0, j)),
                  pl.BlockSpec((n1, nb * 128), lambda b, j: (0, j))],
        out_specs=[win1] * 2,
        out_shape=[jax.ShapeDtypeStruct((bsz, n1, n2, d), F32)] * 2,
        compiler_params=_cparams("parallel", "arbitrary"),
    )(x.reshape(bsz, n1, n2 * d), nw.reshape(1, d), sc, sh, f1, twr, twi)

    win2 = pl.BlockSpec((1, n2, kb, d), lambda b, j: (b, 0, j, 0))
    out = pl.pallas_call(
        functools.partial(_dft2_kernel, d=d, passes=passes),
        grid=(bsz, n1 // kb),
        in_specs=[pl.BlockSpec((1, kb, n2, d), lambda b, j: (b, j, 0, 0))] * 2
        + [win2, vec, full(f2.shape), full(cc.shape), full(sn.shape), full(w_out.shape)],
        out_specs=win2,
        out_shape=jax.ShapeDtypeStruct((bsz, n2, n1, d), F32),
        compiler_params=_cparams("parallel", "arbitrary"),
    )(a_re, a_im, x.reshape(bsz, n2, n1, d), g1, f2, cc, sn, w_out.astype(BF16))
    return out.reshape(bsz, n, d)


def _top16(s, payload=None, order=None):
    rows = lax.broadcasted_iota(I32, s.shape, 0).astype(F32) if order is None else order
    big = 1e9
    vals, idxs = [], []
    for _ in range(PEER_TOPK):
        m = jnp.max(s, axis=0, keepdims=True)
        am = jnp.min(jnp.where(s == m, rows, big), axis=0, keepdims=True)
        sel = rows == am
        vals.append(m)
        if payload is None:
            idxs.append(am)
        else:
            idxs.append(jnp.max(jnp.where(sel, payload, -1.0), axis=0, keepdims=True))
        s = jnp.where(sel, -jnp.inf, s)
    return jnp.concatenate(vals, axis=0), jnp.concatenate(idxs, axis=0)


def _staircase_candidates(s1, i1, s2, i2):
    assert PEER_TOPK == 16 and SUB == 8
    sub_i = lax.broadcasted_iota(I32, (SUB,) + s1.shape[1:], 0)
    sub = sub_i.astype(F32)
    lo8 = lambda t: t[:SUB]
    rot = lambda t, k: pltpu.roll(t, k, 0)

    def pack(v1, v2, comb, pad):
        a_hi = rot(v1[SUB:], 2)
        return [
            comb(v1[0:1], v2),
            comb(v1[1:2], lo8(v2)),
            jnp.where(sub_i < 5, comb(v1[2:3], lo8(v2)), comb(v1[4:5], rot(lo8(v2), 5))),
            jnp.where(sub_i < 4, comb(v1[3:4], lo8(v2)),
                      jnp.where(sub_i < 6, comb(v1[5:6], rot(lo8(v2), 4)), comb(v1[6:7], rot(lo8(v2), 6)))),
            jnp.where(sub_i < 2, comb(v1[7:8], lo8(v2)), comb(a_hi, v2[0:1])),
            jnp.where(sub_i < 2, comb(a_hi, v2[0:1]), pad),
        ]

    cand = jnp.concatenate(pack(s1, s2, lambda x, y: x + y, -jnp.inf), axis=0)
    cand_i = jnp.concatenate(pack(i1, i2, lambda x, y: x * PEER_NKEYS + y, -1.0), axis=0)
    order = jnp.concatenate([
        sub, sub + SUB, 16.0 + sub,
        jnp.where(sub_i < 5, 32.0 + sub, 64.0 + (sub - 5.0)),
        jnp.where(sub_i < 4, 48.0 + sub, jnp.where(sub_i < 6, 80.0 + (sub - 4.0), 96.0 + (sub - 6.0))),
        jnp.where(sub_i < 2, 112.0 + sub, 16.0 * (sub + 6.0)),
        jnp.where(sub_i < 2, 16.0 * (sub + 14.0), 1e6 + sub)], axis=0)
    return cand, cand_i, order


def _peer_topk_kernel(x_ref, nw_ref, sc_ref, sh_ref, wh_ref, wl_ref, keys_ref,
                      h_out, row_out, par_out, gate_out):
    h = _normmod(x_ref[0], nw_ref[...], sc_ref[0], sh_ref[0])
    h_out[0] = h
    hh, hl = _split(h)
    q = _dot(hh, wh_ref[...]) + (_dot(hh, wl_ref[...]) + _dot(hl, wh_ref[...]))
    half = PEER_DKEY // 2
    idx_rows, gate_rows = [], []
    for hd in range(PEER_HEADS):
        tops = []
        for p in range(2):
            j = hd * 2 + p
            st = _mm(keys_ref[j], q[:, j * half:(j + 1) * half], 3, _dot_nt)
            tops.append(_top16(st))
        (s1, i1), (s2, i2) = tops
        top_s, top_i = _top16(*_staircase_candidates(s1, i1, s2, i2))
        ex = jnp.exp(top_s - top_s[0:1])
        gate_rows.append(ex / jnp.sum(ex, axis=0, keepdims=True))
        idx_rows.append(top_i)
    idx = jnp.concatenate(idx_rows, axis=0).T.astype(I32)
    n_pairs = PEER_NKEYS * PEER_NKEYS // 2
    row_out[0] = (idx & (n_pairs - 1)) * SUB
    par_out[0] = (idx >= n_pairs).astype(I32)
    gate_out[0] = jnp.concatenate(gate_rows, axis=0).T


def _peer_topk(x, nw, sc, sh, w_q, keys, *, tb):
    bsz, n, d = x.shape
    nk = PEER_HEADS * PEER_TOPK
    wh, wl = _split_outside(w_q)
    keys2 = keys.reshape(PEER_HEADS * 2, PEER_NKEYS, PEER_DKEY // 2)
    tok = lambda b, i: (b, i, 0)
    vec = pl.BlockSpec((1, 1, d), lambda b, i: (b, 0, 0))
    full = lambda shape: pl.BlockSpec(shape, lambda b, i: (0,) * len(shape))
    return pl.pallas_call(
        _peer_topk_kernel,
        grid=(bsz, n // tb),
        in_specs=[pl.BlockSpec((1, tb, d), tok), full((1, d)), vec, vec, full(wh.shape), full(wl.shape),
                  full(keys2.shape)],
        out_specs=[pl.BlockSpec((1, tb, d), tok)] + [pl.BlockSpec((1, tb, nk), tok)] * 3,
        out_shape=[jax.ShapeDtypeStruct((bsz, n, d), F32), jax.ShapeDtypeStruct((bsz, n, nk), I32),
                   jax.ShapeDtypeStruct((bsz, n, nk), I32), jax.ShapeDtypeStruct((bsz, n, nk), F32)],
        compiler_params=_cparams("parallel", "arbitrary"),
    )(x, nw.reshape(1, d), sc, sh, wh, wl, keys2)


def _pack_kernel(hi_ref, lo_ref, o_ref):
    pairs, d = hi_ref.shape
    bits = lambda v: lax.bitcast_convert_type(v.astype(BF16).astype(F32), U32)
    packed = bits(hi_ref[...]) | (bits(lo_ref[...]) >> 16)
    for s in range(d // 128):
        o_ref[pl.ds(s, pairs, stride=SUB), :] = packed[:, s * 128:(s + 1) * 128]


def _pack_table(tabs, layer, *, pairs=256):
    _, e, d = tabs.shape
    assert d // 128 == SUB
    nblk = e // 2 // pairs
    return pl.pallas_call(
        _pack_kernel,
        grid=(nblk,),
        in_specs=[pl.BlockSpec((None, pairs, d), lambda i: (layer, i, 0)),
                  pl.BlockSpec((None, pairs, d), lambda i: (layer, i + nblk, 0))],
        out_specs=pl.BlockSpec((pairs * SUB, 128), lambda i: (i, 0)),
        out_shape=jax.ShapeDtypeStruct((e // 2 * SUB, 128), U32),
        compiler_params=_cparams("parallel"),
    )(tabs, tabs)


def _gather_tiles(tab_ref, row_ref, t, nk):
    tiles = [tab_ref[pl.ds(pl.multiple_of(row_ref[t, kk], SUB), SUB), :] for kk in range(nk)]
    return pltpu.bitcast(jnp.concatenate(tiles, axis=0), BF16)


def _gather_constants(nk):
    c = np.arange(nk * PAIR_COLS)
    pair = (c[None, :] // PAIR_COLS == np.arange(nk)[:, None])
    hi = (c % 2 == 1)[None, :]
    smask = ((c[None, :] % PAIR_COLS) // 2 == np.arange(SUB)[:, None]).astype(np.float32)
    as_bf16 = lambda m: jnp.asarray(m.astype(np.float32), dtype=BF16)
    return dict(e_all=as_bf16(pair), e_hi=as_bf16(pair & hi), e_lo=as_bf16(pair & ~hi),
                collapse=as_bf16(pair.T), smask=jnp.asarray(smask), hi_row=jnp.asarray(hi.astype(np.float32)))


def _gelu_tanh(x):
    return 0.5 * x * (1.0 + jnp.tanh(math.sqrt(2.0 / math.pi) * (x + 0.044715 * (x * x * x))))


def _peer_act_kernel(row_ref, par_ref, h_ref, gate_ref, tab_ref, eall_ref, coll_ref, smask_ref, hi_ref,
                     w_out, rsum_ref, h8_ref):
    tb, nk = gate_ref.shape
    smask = smask_ref[...]
    for s in range(SUB):
        h8_ref[pl.ds(s, tb, stride=SUB), :] = h_ref[:, s * 128:(s + 1) * 128]

    def token(t, carry):
        g = _gather_tiles(tab_ref, row_ref, t, nk)
        hh, hl = _split(h8_ref[pl.ds(pl.multiple_of(t * SUB, SUB), SUB), :])
        r = _dot_nt(jnp.concatenate([hh, hl], axis=0), g)
        rsum_ref[pl.ds(t, 1), :] = jnp.sum((r[:SUB] + r[SUB:]) * smask, axis=0, keepdims=True)
        return carry

    lax.fori_loop(0, tb, token, 0, unroll=2 * GATHER_UNROLL)
    want_hi = 1.0 - _dot(par_ref[...].astype(F32).astype(BF16), eall_ref[...])
    picked = jnp.where(want_hi == hi_ref[...], rsum_ref[...], 0.0)
    w_out[...] = gate_ref[...] * _gelu_tanh(_mm_lhs2(picked, coll_ref[...]))


def _peer_mix_kernel(row_ref, par_ref, w_ref, x_ref, g2_ref, tab_ref, ehi_ref, elo_ref, smask_ref,
                     o_ref, wcols_ref, acc_ref):
    tb, nk = w_ref.shape
    smask = smask_ref[...]
    par = par_ref[...].astype(F32)
    w = w_ref[...]
    wcols_ref[...] = _mm_lhs2(w * (1.0 - par), ehi_ref[...]) + _mm_lhs2(w * par, elo_ref[...])

    def token(t, carry):
        g = _gather_tiles(tab_ref, row_ref, t, nk)
        wh, wl = _split(wcols_ref[pl.ds(t, 1), :] * smask)
        res = _dot(jnp.concatenate([wh, wl], axis=0), g)
        acc_ref[pl.ds(pl.multiple_of(t * SUB, SUB), SUB), :] = res[:SUB] + res[SUB:]
        return carry

    lax.fori_loop(0, tb, token, 0, unroll=2 * GATHER_UNROLL)
    for s in range(SUB):
        cs = slice(s * 128, (s + 1) * 128)
        o_ref[:, cs] = x_ref[:, cs] + g2_ref[0][:, cs] * acc_ref[pl.ds(s, tb, stride=SUB), :]


def _peer_gather(x, h, row, par, gate, g2, u_tabs, v_tabs, layer, *, tb):
    bsz, n, d = x.shape
    u_tab, v_tab = _pack_table(u_tabs, layer), _pack_table(v_tabs, layer)
    t_all = bsz * n
    nk = row.shape[-1]
    sl = d // 128
    assert sl == SUB
    row2, par2 = row.reshape(t_all, nk), par.reshape(t_all, nk)
    cst = _gather_constants(nk)
    ncol = nk * PAIR_COLS
    tab_spec = pl.BlockSpec(u_tab.shape, lambda i: (0, 0), pipeline_mode=pl.Buffered(1))
    smem = pl.BlockSpec((tb, nk), lambda i: (i, 0), memory_space=pltpu.SMEM)
    tokk = pl.BlockSpec((tb, nk), lambda i: (i, 0))
    tokd = pl.BlockSpec((tb, d), lambda i: (i, 0))
    tiles = pltpu.VMEM((tb * sl, 128), F32)
    full = lambda a: pl.BlockSpec(a.shape, lambda i: (0,) * a.ndim)
    w = pl.pallas_call(
        _peer_act_kernel,
        grid=(t_all // tb,),
        in_specs=[smem, tokk, tokd, tokk, tab_spec, full(cst["e_all"]), full(cst["collapse"]),
                  full(cst["smask"]), full(cst["hi_row"])],
        out_specs=tokk,
        out_shape=jax.ShapeDtypeStruct((t_all, nk), F32),
        scratch_shapes=[pltpu.VMEM((tb, ncol), F32), tiles],
        compiler_params=_cparams("arbitrary"),
    )(row2, par2, h.reshape(t_all, d), gate.reshape(t_all, nk), u_tab,
      cst["e_all"], cst["collapse"], cst["smask"], cst["hi_row"])
    out = pl.pallas_call(
        _peer_mix_kernel,
        grid=(t_all // tb,),
        in_specs=[smem, tokk, tokk, tokd, pl.BlockSpec((1, 1, d), lambda i: ((i * tb) // n, 0, 0)),
                  tab_spec, full(cst["e_hi"]), full(cst["e_lo"]), full(cst["smask"])],
        out_specs=tokd,
        out_shape=jax.ShapeDtypeStruct((t_all, d), F32),
        scratch_shapes=[pltpu.VMEM((tb, ncol), F32), tiles],
        compiler_params=_cparams("arbitrary"),
    )(row2, par2, w, x.reshape(t_all, d), g2, v_tab,
      cst["e_hi"], cst["e_lo"], cst["smask"])
    return out.reshape(bsz, n, d)


def _peer_layer(x, nw, sc, sh, g2, w_q, keys, u_tabs, v_tabs, layer):
    h, row, par, gate = _peer_topk(x, nw, sc, sh, w_q, keys, tb=128)
    return _peer_gather(x, h, row, par, gate, g2, u_tabs, v_tabs, layer, tb=128)


def _final_norm_kernel(x_ref, w_ref, o_ref):
    x = x_ref[0]
    o_ref[0] = x * lax.rsqrt(jnp.mean(x * x, axis=-1, keepdims=True) + EPS) * w_ref[...]


def _final_norm(x, w, *, tm):
    bsz, n, d = x.shape
    return pl.pallas_call(
        _final_norm_kernel,
        grid=(bsz, n // tm),
        in_specs=[pl.BlockSpec((1, tm, d), lambda b, i: (b, i, 0)), pl.BlockSpec((1, d), lambda b, i: (0, 0))],
        out_specs=pl.BlockSpec((1, tm, d), lambda b, i: (b, i, 0)),
        out_shape=jax.ShapeDtypeStruct((bsz, n, d), F32),
        compiler_params=_cparams("parallel", "arbitrary"),
    )(x, w.reshape(1, d))


def _rope_tables(n_ctx, n_lat):
    pos = np.arange(n_lat)
    axis_dim = B_DH // 2
    inv = ROPE_BASE ** (-np.arange(0, axis_dim, 2, dtype=np.float32) / axis_dim)
    ang = np.concatenate([(pos // GRID_W)[:, None] * inv, (pos % GRID_W)[:, None] * inv], axis=-1)
    ang = np.concatenate([np.zeros((n_ctx, ang.shape[1])), ang], axis=0).astype(np.float32)
    cos = np.cos(ang)
    sin = np.sin(ang)
    cos64 = np.concatenate([cos, cos], axis=1)
    sin64 = np.concatenate([-sin, sin], axis=1)
    return (jnp.asarray(np.tile(cos64, (1, 2)), F32), jnp.asarray(np.tile(sin64, (1, 2)), F32))


def _mixer_ab_layer(x, ctx, mod, mod_c, layer, norm1_w, w_in, conv_w, a_log, dt_bias, gdn_norm_w,
                    lam_params, subln_w, w_out):
    bsz, n_lat, d = x.shape
    n_ctx = ctx.shape[1]
    n = n_ctx + n_lat
    tm = GDN_BLOCK
    lam_init = 0.8 - 0.6 * math.exp(-0.3 * layer)

    assert n_ctx % tm == 0 and n_lat % tm == 0
    seg = lambda j: jnp.stack([jnp.broadcast_to(mod_c[j], (bsz, d)), mod[:, j]], axis=1)[:, :, None, :]
    cuts = np.cumsum([0, 3 * A_W, A_W, 4 * A_HEADS, B_W, B_W, B_W])
    weights = [w_in[:, a:b] for a, b in zip(cuts[:-1], cuts[1:])]
    qkv, gate, ba, bq, bk, bv, edges = _in_proj(
        ctx, x, norm1_w, seg(1), seg(0), weights, [F32, F32, F32, F32, F32, BF16], tm=tm)

    first, last = edges[:, :, 0], edges[:, :, 1]
    zero = jnp.zeros_like(last[:, :1])
    prev_rows = jnp.concatenate([zero, last[:, :-1]], axis=1)
    next_rows = jnp.concatenate([first[:, 1:], zero], axis=1)
    blk = jnp.arange(n // tm)[None, :, None]
    prev_rows = jnp.where(blk == n_ctx // tm, 0.0, prev_rows)[:, :, None, :]
    next_rows = jnp.where(blk == n_ctx // tm - 1, 0.0, next_rows)[:, :, None, :]
    cos128, sin128 = _rope_tables(n_ctx, n_lat)
    q, k, v, bqr, bkr = _prep(qkv, prev_rows, next_rows, conv_w, bq, bk, cos128, sin128, tm=tm, n_ctx=n_ctx)

    bat = jnp.swapaxes(ba, 1, 2)
    o_f = _gdn(q, k, v, ba, bat, a_log, dt_bias, d=0, n_ctx=n_ctx, inv_passes=1)
    o_b = _gdn(q, k, v, ba, bat, a_log, dt_bias, d=1, n_ctx=n_ctx, inv_passes=1)
    d_lat = _diff_attention(bqr, bkr, jnp.swapaxes(bv, 1, 2), lam_params, lam_init=lam_init, tq=512,
                            tk=ATTN_TK if n % ATTN_TK == 0 else n)
    return _outproj(o_f, o_b, gate, d_lat, x, mod[:, 2][:, None, :], gdn_norm_w, subln_w, w_out,
                    lam_init=lam_init, n_ctx=n_ctx, tm=tm)


def kernel(x, c, ctx, c_ctx, ada_w, ada_b, norm1_w, norm2_w, w_in, conv_w, a_log, dt_bias, gdn_norm_w,
           lam_q1, lam_k1, lam_q2, lam_k2, subln_w, w_out_ab, w_out_f, peer_wq, peer_keys, peer_u,
           peer_v, final_norm_w):
    bsz, n_lat, d = x.shape
    depth = ada_w.shape[0]
    last_ctx_reader = 2 * ((depth - 1) // 2)
    cmat = jnp.concatenate([c, c_ctx[None, :], jnp.zeros((8 - bsz - 1, d), F32)], axis=0)
    for i in range(depth):
        if i < last_ctx_reader:
            raise NotImplementedError("context stream advance (depth > 2) is not implemented")
        mod_all = _modulation(cmat, ada_w[i], ada_b[i])
        mod = mod_all[:bsz].reshape(bsz, 6, d)
        mod_c = mod_all[bsz].reshape(6, d)
        vec = lambda j: mod[:, j][:, None, :]
        j = i // 2
        if i % 2 == 0:
            x = _mixer_ab_layer(x, ctx, mod, mod_c, i, norm1_w[i], w_in[j], conv_w[j], a_log[j], dt_bias[j],
                                gdn_norm_w[j], (lam_q1[j], lam_k1[j], lam_q2[j], lam_k2[j]), subln_w[j],
                                w_out_ab[j])
        else:
            x = _fourier_layer(x, norm1_w[i], vec(1), vec(0), vec(2), w_out_f[j])
        x = _peer_layer(x, norm2_w[i], vec(4), vec(3), vec(5), peer_wq[i], peer_keys[i], peer_u, peer_v, i)
    return _final_norm(x, final_norm_w, tm=512)
```

```python
import functools
import math

import numpy as np
import jax
import jax.numpy as jnp
from jax import lax
from jax.experimental import pallas as pl
from jax.experimental.pallas import tpu as pltpu

F32 = jnp.float32
BF16 = jnp.bfloat16
I32 = jnp.int32
U32 = jnp.uint32
HIGHEST = lax.Precision.HIGHEST

EPS = 1e-6
ROPE_BASE = 10000.0
GRID_W = 64
A_HEADS, A_DK, A_DV, A_CHUNK = 8, 64, 64, 64
B_HEADS, B_DH = 4, 64
C_GROUPS = 4
PEER_HEADS, PEER_NKEYS, PEER_DKEY, PEER_TOPK = 8, 128, 256, 16
A_W = A_HEADS * A_DK
B_W = B_HEADS * 2 * B_DH
GDN_BLOCK = 256
GDN_HEAD_GROUP = 8
ATTN_SLAB = 64
ATTN_TK = 1408
DFT_N1 = 128
SUB = 8
PAIR_COLS = 2 * SUB
GATHER_UNROLL = 8

VMEM_LIMIT_BYTES = 48 * 1024 * 1024


def _cparams(*sem):
    return pltpu.CompilerParams(dimension_semantics=sem, vmem_limit_bytes=VMEM_LIMIT_BYTES)


def _dot(a, b):
    return lax.dot_general(a, b, (((1,), (0,)), ((), ())), preferred_element_type=F32)


def _dot_nt(a, b):
    return lax.dot_general(a, b, (((1,), (1,)), ((), ())), preferred_element_type=F32)


def _dot_tn(a, b):
    return lax.dot_general(a, b, (((0,), (0,)), ((), ())), preferred_element_type=F32)


def _dot_f32(a, b):
    return lax.dot_general(a, b, (((1,), (0,)), ((), ())), precision=HIGHEST,
                           preferred_element_type=F32)


def _split(a):
    hi = a.astype(BF16)
    lo = (a - hi.astype(F32)).astype(BF16)
    return hi, lo


def _mm(a, b, passes=1, dot=_dot):
    if passes == 1:
        return dot(a.astype(BF16), b.astype(BF16))
    ah, al = _split(a)
    bh, bl = _split(b)
    return dot(ah, bh) + (dot(ah, bl) + dot(al, bh))


def _split3(a):
    hi = a.astype(BF16)
    r = a - hi.astype(F32)
    mid = r.astype(BF16)
    return hi, mid, (r - mid.astype(F32)).astype(BF16)


def _mm_lhs2(a, b_bf16):
    ah, al = _split(a)
    return _dot(ah, b_bf16) + _dot(al, b_bf16)


def _silu(x):
    return x * jax.nn.sigmoid(x)


def _softplus(x):
    return jnp.maximum(x, 0.0) + jnp.log(1.0 + jnp.exp(-jnp.abs(x)))


def _normmod(x, nw, sc, sh):
    ms = jnp.mean(x * x, axis=-1, keepdims=True)
    return (x * lax.rsqrt(ms + EPS) * nw) * (1.0 + sc) + sh


def _split_outside(w):
    hi = lax.bitcast_convert_type(lax.bitcast_convert_type(w, U32) & jnp.uint32(0xFFFF0000), F32)
    return hi.astype(BF16), (w - hi).astype(BF16)


def _block_ones(n, blk):
    i = np.arange(n) // blk
    return jnp.asarray((i[:, None] == i[None, :]).astype(np.float32), dtype=BF16)


def _mod_kernel(c_ref, w_ref, b_ref, o_ref):
    o_ref[...] = _dot_f32(_silu(c_ref[...]), w_ref[...]) + b_ref[...]


def _modulation(cmat, w, b):
    rows, d = cmat.shape
    n = w.shape[1]
    return pl.pallas_call(
        _mod_kernel,
        grid=(n // d,),
        in_specs=[pl.BlockSpec((rows, d), lambda j: (0, 0)),
                  pl.BlockSpec((d, d), lambda j: (0, j)),
                  pl.BlockSpec((1, d), lambda j: (0, j))],
        out_specs=pl.BlockSpec((rows, d), lambda j: (0, j)),
        out_shape=jax.ShapeDtypeStruct((rows, n), F32),
        compiler_params=_cparams("arbitrary"),
    )(cmat, w, b.reshape(1, n))


def _in_proj_kernel(*refs, n_w, n_first):
    ctx_ref, x_ref, nw_ref, sc_ref, sh_ref = refs[:5]
    w_refs = refs[5:5 + n_w]
    outs = refs[5 + n_w:5 + 2 * n_w]
    edge_ref = refs[5 + 2 * n_w]
    xin = jnp.where(pl.program_id(1) < n_first, ctx_ref[0], x_ref[0])
    hb = _normmod(xin, nw_ref[...], sc_ref[0, 0], sh_ref[0, 0]).astype(BF16)
    for j in range(n_w):
        y = _dot(hb, w_refs[j][...])
        outs[j][0] = y.astype(outs[j].dtype)
        if j == 0:
            tm = y.shape[0]
            edge_ref[0, 0] = jnp.concatenate(
                [y[0:1], y[tm - 1:tm], jnp.zeros((SUB - 2, y.shape[1]), F32)], axis=0)


def _in_proj(ctx, x, nw, sc, sh, weights, out_dtypes, *, tm):
    bsz, n_lat, d = x.shape
    n_first = ctx.shape[1] // tm
    nblk = n_first + n_lat // tm
    widths = tuple(w.shape[1] for w in weights)
    w_in = [w.astype(BF16) for w in weights]
    mod_map = lambda b, i: (b, jnp.where(i < n_first, 0, 1), 0, 0)
    tok = lambda b, i: (b, i, 0)
    in_specs = [pl.BlockSpec((1, tm, d), lambda b, i: (b, jnp.minimum(i, n_first - 1), 0)),
                pl.BlockSpec((1, tm, d), lambda b, i: (b, jnp.maximum(i - n_first, 0), 0)),
                pl.BlockSpec((1, d), lambda b, i: (0, 0)),
                pl.BlockSpec((1, 1, 1, d), mod_map), pl.BlockSpec((1, 1, 1, d), mod_map)]
    in_specs += [pl.BlockSpec(w.shape, lambda b, i: (0, 0)) for w in w_in]
    out_specs = [pl.BlockSpec((1, tm, wd), tok) for wd in widths]
    out_specs.append(pl.BlockSpec((1, 1, SUB, widths[0]), lambda b, i: (b, i, 0, 0)))
    out_shape = [jax.ShapeDtypeStruct((bsz, nblk * tm, wd), dt) for wd, dt in zip(widths, out_dtypes)]
    out_shape.append(jax.ShapeDtypeStruct((bsz, nblk, SUB, widths[0]), F32))
    return pl.pallas_call(
        functools.partial(_in_proj_kernel, n_w=len(w_in), n_first=n_first),
        grid=(bsz, nblk),
        in_specs=in_specs, out_specs=out_specs, out_shape=out_shape,
        compiler_params=_cparams("parallel", "arbitrary"),
    )(ctx, x, nw.reshape(1, d), sc, sh, *w_in)


def _prep_kernel(qkv_ref, prev_ref, next_ref, cw_ref, bd_ref, bq_ref, bk_ref, cos_ref, sin_ref,
                 q_out, k_out, v_out, bqr_out, bkr_out):
    x = qkv_ref[0]
    tm = x.shape[0]
    row = lax.broadcasted_iota(I32, x.shape, 0)
    xm = jnp.where(row == 0, prev_ref[0, 0], pltpu.roll(x, 1, 0))
    xp = jnp.where(row == tm - 1, next_ref[0, 0], pltpu.roll(x, tm - 1, 0))
    cw = cw_ref[...]
    y = _silu(xm * cw[0:1] + x * cw[1:2] + xp * cw[2:3])
    bd = bd_ref[...]

    def l2(t):
        return t * lax.rsqrt(_mm_lhs2(t * t, bd) + EPS)

    q_out[0] = l2(y[:, :A_W]) * (A_DK ** -0.5)
    k_out[0] = l2(y[:, A_W:2 * A_W])
    v_out[0] = y[:, 2 * A_W:]

    cos = jnp.concatenate([cos_ref[...]] * (B_W // 128), axis=1)
    sin = jnp.concatenate([sin_ref[...]] * (B_W // 128), axis=1)
    lane = lax.broadcasted_iota(I32, (tm, B_W), 1)
    first = (lane & (B_DH - 1)) < (B_DH // 2)

    def rope(t):
        rot = jnp.where(first, pltpu.roll(t, B_W - B_DH // 2, 1), pltpu.roll(t, B_DH // 2, 1))
        return t * cos + rot * sin

    bqr_out[0] = (rope(bq_ref[0]) * (B_DH ** -0.5)).astype(BF16)
    bkr_out[0] = rope(bk_ref[0]).astype(BF16)


def _prep(qkv, prev_rows, next_rows, conv_w, bq, bk, cos128, sin128, *, tm, n_ctx):
    bsz, n, cw = qkv.shape
    tok = lambda b, i: (b, i, 0)
    lat = lambda b, i: (b, jnp.maximum(i - n_ctx // tm, 0), 0)
    halo = pl.BlockSpec((1, 1, 1, cw), lambda b, i: (b, i, 0, 0))
    return pl.pallas_call(
        _prep_kernel,
        grid=(bsz, n // tm),
        in_specs=[pl.BlockSpec((1, tm, cw), tok), halo, halo,
                  pl.BlockSpec(conv_w.shape, lambda b, i: (0, 0)),
                  pl.BlockSpec((A_W, A_W), lambda b, i: (0, 0)),
                  pl.BlockSpec((1, tm, B_W), tok), pl.BlockSpec((1, tm, B_W), tok),
                  pl.BlockSpec((tm, 128), lambda b, i: (i, 0)),
                  pl.BlockSpec((tm, 128), lambda b, i: (i, 0))],
        out_specs=[pl.BlockSpec((1, tm, A_W), tok)] * 3
        + [pl.BlockSpec((1, tm, B_W), lat), pl.BlockSpec((1, tm, B_W), tok)],
        out_shape=[jax.ShapeDtypeStruct((bsz, n, A_W), F32)] * 3
        + [jax.ShapeDtypeStruct((bsz, n - n_ctx, B_W), BF16), jax.ShapeDtypeStruct((bsz, n, B_W), BF16)],
        compiler_params=_cparams("parallel", "arbitrary"),
    )(qkv, prev_rows, next_rows, conv_w, _block_ones(A_W, A_DK), bq, bk, cos128, sin128)


def _gdn_kernel(q_ref, k_ref, v_ref, ba_ref, bat_ref, alr_ref, dtr_ref, alc_ref, dtc_ref,
                lm_ref, lmt_ref, ltot_ref, e_ref, o_ref, s_ref, *, d, inv_passes):
    rev = d == 1
    nh = A_HEADS

    @pl.when(pl.program_id(1) == 0)
    def _():
        s_ref[...] = jnp.zeros_like(s_ref)

    q, k, v = q_ref[0], k_ref[0], v_ref[0]
    ba, bat = ba_ref[0], bat_ref[0]
    n = q.shape[0]
    beta_c = jax.nn.sigmoid(ba[:, d * nh:(d + 1) * nh])
    g_c = -jnp.exp(alr_ref[...]) * _softplus(ba[:, 2 * nh + d * nh:2 * nh + (d + 1) * nh] + dtr_ref[...])
    g_t = -jnp.exp(alc_ref[...]) * _softplus(bat[2 * nh + d * nh:2 * nh + (d + 1) * nh, :] + dtc_ref[...])
    lm = lm_ref[...]
    gs = _split3(g_c)
    gcum_c = sum(_dot(lm.astype(BF16), p) for p in gs)
    gtot_c = sum(_dot(ltot_ref[...].astype(BF16), p) for p in gs)
    gcum_t = sum(_dot(p, lmt_ref[...].astype(BF16)) for p in _split3(g_t))
    small = jnp.concatenate([jnp.exp(gcum_c), jnp.exp(gtot_c - gcum_c), jnp.exp(gtot_c), beta_c], axis=1)
    wide = _mm_lhs2(small, e_ref[...])
    eg, ekt, egt, beta_e = (wide[:, j * A_W:(j + 1) * A_W] for j in range(4))
    kb = k * beta_e
    vb = v * beta_e
    kbe = kb * eg
    qd = q * eg
    ktail = k * ekt

    ri = lax.broadcasted_iota(I32, (n, n), 0)
    ci = lax.broadcasted_iota(I32, (n, n), 1)
    eye = (ri == ci).astype(F32)
    same = lambda s: (ri >> s) == (ci >> s)
    d16 = same(4).astype(F32)
    o32 = (same(5) & ~same(4)).astype(F32)
    o64 = (same(6) & ~same(5)).astype(F32)
    incl = lm > 0.0
    strict = lm * (1.0 - eye)

    order = range(n // A_CHUNK - 1, -1, -1) if rev else range(n // A_CHUNK)
    sls = [slice(h * A_DK, (h + 1) * A_DK) for h in range(nh)]
    heads = [None] * nh
    for g0 in range(0, nh, GDN_HEAD_GROUP):
        hs = range(g0, g0 + GDN_HEAD_GROUP)
        decay = {h: jnp.exp(jnp.where(incl, gcum_c[:, h:h + 1] - gcum_t[h:h + 1, :], -1e30)) for h in hs}
        a = {h: _mm(kb[:, sls[h]], k[:, sls[h]], 1, _dot_nt) * decay[h] * strict for h in hs}
        p = {h: -(a[h] * d16) for h in hs}
        x = {h: eye + p[h] for h in hs}
        for _ in range(3):
            p = {h: _mm(p[h], p[h], inv_passes) for h in hs}
            x = {h: x[h] + _mm(x[h], p[h], inv_passes) for h in hs}
        for om in (o32, o64):
            y = {h: _mm(x[h], a[h] * om, inv_passes) for h in hs}
            x = {h: x[h] - _mm(y[h], x[h], inv_passes) for h in hs}
        uw = {h: _mm(x[h], jnp.concatenate([vb[:, sls[h]], kbe[:, sls[h]]], axis=1)) for h in hs}
        attn = {h: _mm(q[:, sls[h]], k[:, sls[h]], 1, _dot_nt) * decay[h] for h in hs}
        s = {h: s_ref[h] for h in hs}
        parts = {h: {} for h in hs}
        for c in order:
            r = slice(c * A_CHUNK, (c + 1) * A_CHUNK)
            v_new = {h: uw[h][r, :A_DV] - _mm(uw[h][r, A_DV:], s[h]) for h in hs}
            for h in hs:
                parts[h][c] = _mm(qd[r, sls[h]], s[h]) + _mm(attn[h][r, r], v_new[h])
            s = {h: s[h] * egt[c * A_CHUNK:c * A_CHUNK + 1, sls[h]]
                 + _mm(ktail[r, sls[h]], v_new[h], 1, _dot_tn) for h in hs}
        for h in hs:
            s_ref[h] = s[h]
            heads[h] = jnp.concatenate([parts[h][c] for c in range(n // A_CHUNK)], axis=0)
    o_ref[0] = jnp.concatenate(heads, axis=1)


def _gdn_constants(d):
    n = GDN_BLOCK
    i = np.arange(n)
    same = (i[:, None] // A_CHUNK) == (i[None, :] // A_CHUNK)
    tri = (i[None, :] >= i[:, None]) if d == 1 else (i[None, :] <= i[:, None])
    lm = (same & tri).astype(np.float32)
    e = (np.arange(A_W)[None, :] // A_DK == np.arange(A_HEADS)[:, None]).astype(np.float32)
    e4 = np.kron(np.eye(4, dtype=np.float32), e)
    return (jnp.asarray(lm), jnp.asarray(lm.T.copy()), jnp.asarray(same.astype(np.float32)),
            jnp.asarray(e4, dtype=BF16))


def _gdn(q, k, v, ba, bat, a_log, dt_bias, *, d, n_ctx, inv_passes):
    bsz, n, _ = q.shape
    nb, nc = n // GDN_BLOCK, n_ctx // GDN_BLOCK
    blk = (lambda s: jnp.where(s < nc, nc - 1 - s, nb - 1 + nc - s)) if d == 1 else (lambda s: s)
    tok = lambda b, s: (b, blk(s), 0)
    full = lambda shape: pl.BlockSpec(shape, lambda b, s: (0,) * len(shape))
    lm, lmt, ltot, e = _gdn_constants(d)
    return pl.pallas_call(
        functools.partial(_gdn_kernel, d=d, inv_passes=inv_passes),
        grid=(bsz, nb),
        in_specs=[pl.BlockSpec((1, GDN_BLOCK, A_W), tok)] * 3
        + [pl.BlockSpec((1, GDN_BLOCK, ba.shape[2]), tok),
           pl.BlockSpec((1, bat.shape[1], GDN_BLOCK), lambda b, s: (b, 0, blk(s))),
           full((1, A_HEADS)), full((1, A_HEADS)), full((A_HEADS, 1)), full((A_HEADS, 1)),
           full(lm.shape), full(lm.shape), full(lm.shape), full(e.shape)],
        out_specs=pl.BlockSpec((1, GDN_BLOCK, A_W), tok),
        out_shape=jax.ShapeDtypeStruct((bsz, n, A_W), F32),
        scratch_shapes=[pltpu.VMEM((A_HEADS, A_DK, A_DV), F32)],
        compiler_params=_cparams("parallel", "arbitrary"),
    )(q, k, v, ba, bat, a_log[d].reshape(1, -1), dt_bias[d].reshape(1, -1),
      a_log[d].reshape(-1, 1), dt_bias[d].reshape(-1, 1), lm, lmt, ltot, e)


def _attn_kernel(q_ref, k_ref, v_ref, lq1_ref, lk1_ref, lq2_ref, lk2_ref, o_ref, *, lam_init, tk):
    q = q_ref[0]
    tq = q.shape[0]
    nk = k_ref.shape[1]
    lam = (jnp.exp(jnp.sum(lq1_ref[...] * lk1_ref[...], keepdims=True))
           - jnp.exp(jnp.sum(lq2_ref[...] * lk2_ref[...], keepdims=True)) + lam_init)
    def over_keys(op, t):
        return op(op(t.reshape(tk // ATTN_SLAB, ATTN_SLAB, tq), axis=0), axis=0, keepdims=True)

    qs = [q[:, m * B_DH:(m + 1) * B_DH] for m in range(2)]
    mx = [jnp.full((1, tq), -jnp.inf, F32)] * 2
    l = [jnp.zeros((1, tq), F32)] * 2
    acc = [jnp.zeros((2 * B_DH, tq), F32)] * 2
    for c in range(nk // tk):
        ks = slice(c * tk, (c + 1) * tk)
        s = [_dot_nt(k_ref[0, ks, m * B_DH:(m + 1) * B_DH], qs[m]) for m in range(2)]
        m_new = [jnp.maximum(mx[m], over_keys(jnp.max, s[m])) for m in range(2)]
        alpha = [jnp.exp(mx[m] - m_new[m]) for m in range(2)]
        p = [jnp.exp(s[m] - m_new[m]) for m in range(2)]
        l = [alpha[m] * l[m] + over_keys(jnp.sum, p[m]) for m in range(2)]
        acc = [alpha[m] * acc[m] + _dot(v_ref[0, :, ks], p[m].astype(BF16)) for m in range(2)]
        mx = m_new
    o_ref[0] = (acc[0] / l[0] - lam * (acc[1] / l[1])).T


def _diff_attention(bqr, bkr, bv, lam_params, *, lam_init, tq, tk):
    bsz, n_lat, _ = bqr.shape
    n = bkr.shape[1]
    hw = 2 * B_DH
    assert n_lat % tq == 0 and n % tk == 0 and tk % ATTN_SLAB == 0
    lam_spec = pl.BlockSpec((1, B_DH), lambda b, h, i: (0, 0))
    return pl.pallas_call(
        functools.partial(_attn_kernel, lam_init=lam_init, tk=tk),
        grid=(bsz, B_HEADS, n_lat // tq),
        in_specs=[pl.BlockSpec((1, tq, hw), lambda b, h, i: (b, i, h)),
                  pl.BlockSpec((1, n, hw), lambda b, h, i: (b, 0, h)),
                  pl.BlockSpec((1, hw, n), lambda b, h, i: (b, h, 0))] + [lam_spec] * 4,
        out_specs=pl.BlockSpec((1, tq, hw), lambda b, h, i: (b, i, h)),
        out_shape=jax.ShapeDtypeStruct((bsz, n_lat, B_W), F32),
        compiler_params=_cparams("parallel", "parallel", "arbitrary"),
    )(bqr, bkr, bv, *[p.reshape(1, B_DH) for p in lam_params])


def _outproj_kernel(of_ref, ob_ref, gate_ref, d_ref, x_ref, g1_ref, gw_ref, sw_ref, bd64_ref,
                    bd128_ref, w_ref, o_ref, *, lam_init):
    o = of_ref[0] + ob_ref[0]
    ms = _mm_lhs2(o * o, bd64_ref[...]) * (1.0 / A_DV)
    y1 = (o * lax.rsqrt(ms + EPS) * gw_ref[...]) * _silu(gate_ref[0])
    dd = d_ref[0]
    ms2 = _mm_lhs2(dd * dd, bd128_ref[...]) * (1.0 / (2 * B_DH))
    y2 = (dd * lax.rsqrt(ms2 + EPS) * sw_ref[...]) * (1.0 - lam_init)
    ycat = jnp.concatenate([y1, y2], axis=1).astype(BF16)
    o_ref[0] = x_ref[0] + g1_ref[0] * _dot(ycat, w_ref[...])


def _outproj(o_f, o_b, gate, d_lat, x, g1, gdn_norm_w, subln_w, w_out, *, lam_init, n_ctx, tm):
    bsz, n_lat, d = x.shape
    off = n_ctx // tm
    tok = lambda b, i: (b, i, 0)
    tok_off = lambda b, i: (b, i + off, 0)
    full = lambda shape: pl.BlockSpec(shape, lambda b, i: (0,) * len(shape))
    return pl.pallas_call(
        functools.partial(_outproj_kernel, lam_init=lam_init),
        grid=(bsz, n_lat // tm),
        in_specs=[pl.BlockSpec((1, tm, A_W), tok_off)] * 3
        + [pl.BlockSpec((1, tm, B_W), tok), pl.BlockSpec((1, tm, d), tok),
           pl.BlockSpec((1, 1, d), lambda b, i: (b, 0, 0)),
           full((1, A_W)), full((1, B_W)), full((A_W, A_W)), full((B_W, B_W)), full(w_out.shape)],
        out_specs=pl.BlockSpec((1, tm, d), tok),
        out_shape=jax.ShapeDtypeStruct((bsz, n_lat, d), F32),
        compiler_params=_cparams("parallel", "arbitrary"),
    )(o_f, o_b, gate, d_lat, x, g1, jnp.tile(gdn_norm_w, A_HEADS).reshape(1, A_W),
      jnp.tile(subln_w, B_HEADS).reshape(1, B_W), _block_ones(A_W, A_DV), _block_ones(B_W, 2 * B_DH),
      w_out.astype(BF16))


def _dft1_kernel(x_ref, nw_ref, sc_ref, sh_ref, f_ref, twr_ref, twi_ref, re_ref, im_ref, *, d, passes):
    n1 = f_ref.shape[1]
    f = f_ref[...]
    for t in range(re_ref.shape[2]):
        h = _normmod(x_ref[0, :, t * d:(t + 1) * d], nw_ref[...], sc_ref[0], sh_ref[0])
        a = _mm(f, h, passes)
        a_re, a_im = a[:n1], a[n1:]
        twr = jnp.concatenate([twr_ref[:, t * 128:(t + 1) * 128]] * (d // 128), axis=1)
        twi = jnp.concatenate([twi_ref[:, t * 128:(t + 1) * 128]] * (d // 128), axis=1)
        re_ref[0, :, t, :] = a_re * twr - a_im * twi
        im_ref[0, :, t, :] = a_re * twi + a_im * twr


def _dft2_kernel(re_ref, im_ref, x_ref, g1_ref, f_ref, cc_ref, sc_ref, w_ref, o_ref, *, d, passes):
    n2 = f_ref.shape[1]
    f = f_ref[...]
    dg = d // C_GROUPS
    for t in range(re_ref.shape[1]):
        pr = _mm(f, re_ref[0, t], passes)
        pi = _mm(f, im_ref[0, t], passes)
        x_re = pr[:n2] + pi[n2:]
        x_im = pi[:n2] - pr[n2:]
        y = jnp.concatenate(
            [_mm(x_re[:, g * dg:(g + 1) * dg], cc_ref[...], passes)
             + _mm(x_im[:, g * dg:(g + 1) * dg], sc_ref[...], passes) for g in range(C_GROUPS)], axis=1)
        o_ref[0, :, t, :] = x_ref[0, :, t, :] + g1_ref[0] * _dot(y.astype(BF16), w_ref[...])


def _fourier_layer(x, nw, sc, sh, g1, w_out, *, passes=3, nb=8, kb=8):
    bsz, n, d = x.shape
    n1, n2 = DFT_N1, n // DFT_N1
    dg = d // C_GROUPS
    k1 = np.arange(n1)
    ang1 = 2 * np.pi * np.outer(k1, k1) / n1
    f1 = jnp.asarray(np.concatenate([np.cos(ang1), -np.sin(ang1)], 0), F32)
    angt = 2 * np.pi * np.outer(k1, np.arange(n2)) / n
    twr = jnp.asarray(np.repeat(np.cos(angt), 128, axis=1), F32)
    twi = jnp.asarray(np.repeat(-np.sin(angt), 128, axis=1), F32)
    k2 = np.arange(n2)
    ang2 = 2 * np.pi * np.outer(k2, k2) / n2
    f2 = jnp.asarray(np.concatenate([np.cos(ang2), np.sin(ang2)], 0), F32)
    angc = 2 * np.pi * np.outer(np.arange(dg), np.arange(dg)) / dg
    scale = 1.0 / math.sqrt(n * dg)
    cc = jnp.asarray(np.cos(angc) * scale, F32)
    sn = jnp.asarray(np.sin(angc) * scale, F32)

    assert n2 % SUB == 0 and nb % SUB == 0 and kb % SUB == 0
    win1 = pl.BlockSpec((1, n1, nb, d), lambda b, j: (b, 0, j, 0))
    vec = pl.BlockSpec((1, 1, d), lambda b, j: (b, 0, 0))
    full = lambda shape: pl.BlockSpec(shape, lambda b, j: (0,) * len(shape))
    a_re, a_im = pl.pallas_call(
        functools.partial(_dft1_kernel, d=d, passes=passes),
        grid=(bsz, n2 // nb),
        in_specs=[pl.BlockSpec((1, n1, nb * d), lambda b, j: (b, 0, j)), full((1, d)), vec, vec, full(f1.shape),
                  pl.BlockSpec((n1, nb * 128), lambda b, j: (0, j)),
                  pl.BlockSpec((n1, nb * 128), lambda b, j: (0, j))],
        out_specs=[win1] * 2,
        out_shape=[jax.ShapeDtypeStruct((bsz, n1, n2, d), F32)] * 2,
        compiler_params=_cparams("parallel", "arbitrary"),
    )(x.reshape(bsz, n1, n2 * d), nw.reshape(1, d), sc, sh, f1, twr, twi)

    win2 = pl.BlockSpec((1, n2, kb, d), lambda b, j: (b, 0, j, 0))
    out = pl.pallas_call(
        functools.partial(_dft2_kernel, d=d, passes=passes),
        grid=(bsz, n1 // kb),
        in_specs=[pl.BlockSpec((1, kb, n2, d), lambda b, j: (b, j, 0, 0))] * 2
        + [win2, vec, full(f2.shape), full(cc.shape), full(sn.shape), full(w_out.shape)],
        out_specs=win2,
        out_shape=jax.ShapeDtypeStruct((bsz, n2, n1, d), F32),
        compiler_params=_cparams("parallel", "arbitrary"),
    )(a_re, a_im, x.reshape(bsz, n2, n1, d), g1, f2, cc, sn, w_out.astype(BF16))
    return out.reshape(bsz, n, d)


def _top16(s, payload=None, order=None):
    rows = lax.broadcasted_iota(I32, s.shape, 0).astype(F32) if order is None else order
    big = 1e9
    vals, idxs = [], []
    for _ in range(PEER_TOPK):
        m = jnp.max(s, axis=0, keepdims=True)
        am = jnp.min(jnp.where(s == m, rows, big), axis=0, keepdims=True)
        sel = rows == am
        vals.append(m)
        if payload is None:
            idxs.append(am)
        else:
            idxs.append(jnp.max(jnp.where(sel, payload, -1.0), axis=0, keepdims=True))
        s = jnp.where(sel, -jnp.inf, s)
    return jnp.concatenate(vals, axis=0), jnp.concatenate(idxs, axis=0)


def _staircase_candidates(s1, i1, s2, i2):
    assert PEER_TOPK == 16 and SUB == 8
    sub_i = lax.broadcasted_iota(I32, (SUB,) + s1.shape[1:], 0)
    sub = sub_i.astype(F32)
    lo8 = lambda t: t[:SUB]
    rot = lambda t, k: pltpu.roll(t, k, 0)

    def pack(v1, v2, comb, pad):
        a_hi = rot(v1[SUB:], 2)
        return [
            comb(v1[0:1], v2),
            comb(v1[1:2], lo8(v2)),
            jnp.where(sub_i < 5, comb(v1[2:3], lo8(v2)), comb(v1[4:5], rot(lo8(v2), 5))),
            jnp.where(sub_i < 4, comb(v1[3:4], lo8(v2)),
                      jnp.where(sub_i < 6, comb(v1[5:6], rot(lo8(v2), 4)), comb(v1[6:7], rot(lo8(v2), 6)))),
            jnp.where(sub_i < 2, comb(v1[7:8], lo8(v2)), comb(a_hi, v2[0:1])),
            jnp.where(sub_i < 2, comb(a_hi, v2[0:1]), pad),
        ]

    cand = jnp.concatenate(pack(s1, s2, lambda x, y: x + y, -jnp.inf), axis=0)
    cand_i = jnp.concatenate(pack(i1, i2, lambda x, y: x * PEER_NKEYS + y, -1.0), axis=0)
    order = jnp.concatenate([
        sub, sub + SUB, 16.0 + sub,
        jnp.where(sub_i < 5, 32.0 + sub, 64.0 + (sub - 5.0)),
        jnp.where(sub_i < 4, 48.0 + sub, jnp.where(sub_i < 6, 80.0 + (sub - 4.0), 96.0 + (sub - 6.0))),
        jnp.where(sub_i < 2, 112.0 + sub, 16.0 * (sub + 6.0)),
        jnp.where(sub_i < 2, 16.0 * (sub + 14.0), 1e6 + sub)], axis=0)
    return cand, cand_i, order


def _peer_topk_kernel(x_ref, nw_ref, sc_ref, sh_ref, wh_ref, wl_ref, keys_ref,
                      h_out, row_out, par_out, gate_out):
    h = _normmod(x_ref[0], nw_ref[...], sc_ref[0], sh_ref[0])
    h_out[0] = h
    hh, hl = _split(h)
    q = _dot(hh, wh_ref[...]) + (_dot(hh, wl_ref[...]) + _dot(hl, wh_ref[...]))
    half = PEER_DKEY // 2
    idx_rows, gate_rows = [], []
    for hd in range(PEER_HEADS):
        tops = []
        for p in range(2):
            j = hd * 2 + p
            st = _mm(keys_ref[j], q[:, j * half:(j + 1) * half], 3, _dot_nt)
            tops.append(_top16(st))
        (s1, i1), (s2, i2) = tops
        top_s, top_i = _top16(*_staircase_candidates(s1, i1, s2, i2))
        ex = jnp.exp(top_s - top_s[0:1])
        gate_rows.append(ex / jnp.sum(ex, axis=0, keepdims=True))
        idx_rows.append(top_i)
    idx = jnp.concatenate(idx_rows, axis=0).T.astype(I32)
    n_pairs = PEER_NKEYS * PEER_NKEYS // 2
    row_out[0] = (idx & (n_pairs - 1)) * SUB
    par_out[0] = (idx >= n_pairs).astype(I32)
    gate_out[0] = jnp.concatenate(gate_rows, axis=0).T


def _peer_topk(x, nw, sc, sh, w_q, keys, *, tb):
    bsz, n, d = x.shape
    nk = PEER_HEADS * PEER_TOPK
    wh, wl = _split_outside(w_q)
    keys2 = keys.reshape(PEER_HEADS * 2, PEER_NKEYS, PEER_DKEY // 2)
    tok = lambda b, i: (b, i, 0)
    vec = pl.BlockSpec((1, 1, d), lambda b, i: (b, 0, 0))
    full = lambda shape: pl.BlockSpec(shape, lambda b, i: (0,) * len(shape))
    return pl.pallas_call(
        _peer_topk_kernel,
        grid=(bsz, n // tb),
        in_specs=[pl.BlockSpec((1, tb, d), tok), full((1, d)), vec, vec, full(wh.shape), full(wl.shape),
                  full(keys2.shape)],
        out_specs=[pl.BlockSpec((1, tb, d), tok)] + [pl.BlockSpec((1, tb, nk), tok)] * 3,
        out_shape=[jax.ShapeDtypeStruct((bsz, n, d), F32), jax.ShapeDtypeStruct((bsz, n, nk), I32),
                   jax.ShapeDtypeStruct((bsz, n, nk), I32), jax.ShapeDtypeStruct((bsz, n, nk), F32)],
        compiler_params=_cparams("parallel", "arbitrary"),
    )(x, nw.reshape(1, d), sc, sh, wh, wl, keys2)


def _pack_kernel(hi_ref, lo_ref, o_ref):
    pairs, d = hi_ref.shape
    bits = lambda v: lax.bitcast_convert_type(v.astype(BF16).astype(F32), U32)
    packed = bits(hi_ref[...]) | (bits(lo_ref[...]) >> 16)
    for s in range(d // 128):
        o_ref[pl.ds(s, pairs, stride=SUB), :] = packed[:, s * 128:(s + 1) * 128]


def _pack_table(tabs, layer, *, pairs=256):
    _, e, d = tabs.shape
    assert d // 128 == SUB
    nblk = e // 2 // pairs
    return pl.pallas_call(
        _pack_kernel,
        grid=(nblk,),
        in_specs=[pl.BlockSpec((None, pairs, d), lambda i: (layer, i, 0)),
                  pl.BlockSpec((None, pairs, d), lambda i: (layer, i + nblk, 0))],
        out_specs=pl.BlockSpec((pairs * SUB, 128), lambda i: (i, 0)),
        out_shape=jax.ShapeDtypeStruct((e // 2 * SUB, 128), U32),
        compiler_params=_cparams("parallel"),
    )(tabs, tabs)


def _gather_tiles(tab_ref, row_ref, t, nk):
    tiles = [tab_ref[pl.ds(pl.multiple_of(row_ref[t, kk], SUB), SUB), :] for kk in range(nk)]
    return pltpu.bitcast(jnp.concatenate(tiles, axis=0), BF16)


def _gather_constants(nk):
    c = np.arange(nk * PAIR_COLS)
    pair = (c[None, :] // PAIR_COLS == np.arange(nk)[:, None])
    hi = (c % 2 == 1)[None, :]
    smask = ((c[None, :] % PAIR_COLS) // 2 == np.arange(SUB)[:, None]).astype(np.float32)
    as_bf16 = lambda m: jnp.asarray(m.astype(np.float32), dtype=BF16)
    by_half = np.concatenate([pair & hi, pair & ~hi], axis=0)
    return dict(spread=as_bf16(by_half), collapse=as_bf16(by_half.T), smask=jnp.asarray(smask))


def _gelu_tanh(x):
    return 0.5 * x * (1.0 + jnp.tanh(math.sqrt(2.0 / math.pi) * (x + 0.044715 * (x * x * x))))


def _peer_act_kernel(row_ref, par_ref, h_ref, gate_ref, tab_ref, coll_ref, smask_ref,
                     w_out, rsum_ref, h8_ref):
    tb, nk = gate_ref.shape
    smask = smask_ref[...]
    for s in range(SUB):
        h8_ref[pl.ds(s, tb, stride=SUB), :] = h_ref[:, s * 128:(s + 1) * 128]

    def token(t, carry):
        g = _gather_tiles(tab_ref, row_ref, t, nk)
        hh, hl = _split(h8_ref[pl.ds(pl.multiple_of(t * SUB, SUB), SUB), :])
        r = _dot_nt(jnp.concatenate([hh, hl], axis=0), g)
        rsum_ref[pl.ds(t, 1), :] = jnp.sum((r[:SUB] + r[SUB:]) * smask, axis=0, keepdims=True)
        return carry

    lax.fori_loop(0, tb, token, 0, unroll=2 * GATHER_UNROLL)
    sums = _mm_lhs2(rsum_ref[...], coll_ref[...])
    act = jnp.where(par_ref[...] == 0, sums[:, :nk], sums[:, nk:])
    w_out[...] = gate_ref[...] * _gelu_tanh(act)


def _peer_mix_kernel(row_ref, par_ref, w_ref, x_ref, g2_ref, tab_ref, spread_ref, smask_ref,
                     o_ref, wcols_ref, acc_ref):
    tb, nk = w_ref.shape
    smask = smask_ref[...]
    par = par_ref[...].astype(F32)
    w = w_ref[...]
    wcols_ref[...] = _mm_lhs2(jnp.concatenate([w * (1.0 - par), w * par], axis=1), spread_ref[...])

    def token(t, carry):
        g = _gather_tiles(tab_ref, row_ref, t, nk)
        wh, wl = _split(wcols_ref[pl.ds(t, 1), :] * smask)
        res = _dot(jnp.concatenate([wh, wl], axis=0), g)
        acc_ref[pl.ds(pl.multiple_of(t * SUB, SUB), SUB), :] = res[:SUB] + res[SUB:]
        return carry

    lax.fori_loop(0, tb, token, 0, unroll=2 * GATHER_UNROLL)
    for s in range(SUB):
        cs = slice(s * 128, (s + 1) * 128)
        o_ref[:, cs] = x_ref[:, cs] + g2_ref[0][:, cs] * acc_ref[pl.ds(s, tb, stride=SUB), :]


def _peer_gather(x, h, row, par, gate, g2, u_tabs, v_tabs, layer, *, tb):
    bsz, n, d = x.shape
    u_tab, v_tab = _pack_table(u_tabs, layer), _pack_table(v_tabs, layer)
    t_all = bsz * n
    nk = row.shape[-1]
    sl = d // 128
    assert sl == SUB
    row2, par2 = row.reshape(t_all, nk), par.reshape(t_all, nk)
    cst = _gather_constants(nk)
    ncol = nk * PAIR_COLS
    tab_spec = pl.BlockSpec(u_tab.shape, lambda i: (0, 0), pipeline_mode=pl.Buffered(1))
    smem = pl.BlockSpec((tb, nk), lambda i: (i, 0), memory_space=pltpu.SMEM)
    tokk = pl.BlockSpec((tb, nk), lambda i: (i, 0))
    tokd = pl.BlockSpec((tb, d), lambda i: (i, 0))
    tiles = pltpu.VMEM((tb * sl, 128), F32)
    full = lambda a: pl.BlockSpec(a.shape, lambda i: (0,) * a.ndim)
    w = pl.pallas_call(
        _peer_act_kernel,
        grid=(t_all // tb,),
        in_specs=[smem, tokk, tokd, tokk, tab_spec, full(cst["collapse"]), full(cst["smask"])],
        out_specs=tokk,
        out_shape=jax.ShapeDtypeStruct((t_all, nk), F32),
        scratch_shapes=[pltpu.VMEM((tb, ncol), F32), tiles],
        compiler_params=_cparams("arbitrary"),
    )(row2, par2, h.reshape(t_all, d), gate.reshape(t_all, nk), u_tab, cst["collapse"], cst["smask"])
    out = pl.pallas_call(
        _peer_mix_kernel,
        grid=(t_all // tb,),
        in_specs=[smem, tokk, tokk, tokd, pl.BlockSpec((1, 1, d), lambda i: ((i * tb) // n, 0, 0)),
                  tab_spec, full(cst["spread"]), full(cst["smask"])],
        out_specs=tokd,
        out_shape=jax.ShapeDtypeStruct((t_all, d), F32),
        scratch_shapes=[pltpu.VMEM((tb, ncol), F32), tiles],
        compiler_params=_cparams("arbitrary"),
    )(row2, par2, w, x.reshape(t_all, d), g2, v_tab, cst["spread"], cst["smask"])
    return out.reshape(bsz, n, d)


def _peer_layer(x, nw, sc, sh, g2, w_q, keys, u_tabs, v_tabs, layer):
    h, row, par, gate = _peer_topk(x, nw, sc, sh, w_q, keys, tb=128)
    return _peer_gather(x, h, row, par, gate, g2, u_tabs, v_tabs, layer, tb=128)


def _final_norm_kernel(x_ref, w_ref, o_ref):
    x = x_ref[0]
    o_ref[0] = x * lax.rsqrt(jnp.mean(x * x, axis=-1, keepdims=True) + EPS) * w_ref[...]


def _final_norm(x, w, *, tm):
    bsz, n, d = x.shape
    return pl.pallas_call(
        _final_norm_kernel,
        grid=(bsz, n // tm),
        in_specs=[pl.BlockSpec((1, tm, d), lambda b, i: (b, i, 0)), pl.BlockSpec((1, d), lambda b, i: (0, 0))],
        out_specs=pl.BlockSpec((1, tm, d), lambda b, i: (b, i, 0)),
        out_shape=jax.ShapeDtypeStruct((bsz, n, d), F32),
        compiler_params=_cparams("parallel", "arbitrary"),
    )(x, w.reshape(1, d))


def _rope_tables(n_ctx, n_lat):
    pos = np.arange(n_lat)
    axis_dim = B_DH // 2
    inv = ROPE_BASE ** (-np.arange(0, axis_dim, 2, dtype=np.float32) / axis_dim)
    ang = np.concatenate([(pos // GRID_W)[:, None] * inv, (pos % GRID_W)[:, None] * inv], axis=-1)
    ang = np.concatenate([np.zeros((n_ctx, ang.shape[1])), ang], axis=0).astype(np.float32)
    cos = np.cos(ang)
    sin = np.sin(ang)
    cos64 = np.concatenate([cos, cos], axis=1)
    sin64 = np.concatenate([-sin, sin], axis=1)
    return (jnp.asarray(np.tile(cos64, (1, 2)), F32), jnp.asarray(np.tile(sin64, (1, 2)), F32))


def _mixer_ab_layer(x, ctx, mod, mod_c, layer, norm1_w, w_in, conv_w, a_log, dt_bias, gdn_norm_w,
                    lam_params, subln_w, w_out):
    bsz, n_lat, d = x.shape
    n_ctx = ctx.shape[1]
    n = n_ctx + n_lat
    tm = GDN_BLOCK
    lam_init = 0.8 - 0.6 * math.exp(-0.3 * layer)

    assert n_ctx % tm == 0 and n_lat % tm == 0
    seg = lambda j: jnp.stack([jnp.broadcast_to(mod_c[j], (bsz, d)), mod[:, j]], axis=1)[:, :, None, :]
    cuts = np.cumsum([0, 3 * A_W, A_W, 4 * A_HEADS, B_W, B_W, B_W])
    weights = [w_in[:, a:b] for a, b in zip(cuts[:-1], cuts[1:])]
    qkv, gate, ba, bq, bk, bv, edges = _in_proj(
        ctx, x, norm1_w, seg(1), seg(0), weights, [F32, F32, F32, F32, F32, BF16], tm=tm)

    first, last = edges[:, :, 0], edges[:, :, 1]
    zero = jnp.zeros_like(last[:, :1])
    prev_rows = jnp.concatenate([zero, last[:, :-1]], axis=1)
    next_rows = jnp.concatenate([first[:, 1:], zero], axis=1)
    blk = jnp.arange(n // tm)[None, :, None]
    prev_rows = jnp.where(blk == n_ctx // tm, 0.0, prev_rows)[:, :, None, :]
    next_rows = jnp.where(blk == n_ctx // tm - 1, 0.0, next_rows)[:, :, None, :]
    cos128, sin128 = _rope_tables(n_ctx, n_lat)
    q, k, v, bqr, bkr = _prep(qkv, prev_rows, next_rows, conv_w, bq, bk, cos128, sin128, tm=tm, n_ctx=n_ctx)

    bat = jnp.swapaxes(ba, 1, 2)
    o_f = _gdn(q, k, v, ba, bat, a_log, dt_bias, d=0, n_ctx=n_ctx, inv_passes=1)
    o_b = _gdn(q, k, v, ba, bat, a_log, dt_bias, d=1, n_ctx=n_ctx, inv_passes=1)
    d_lat = _diff_attention(bqr, bkr, jnp.swapaxes(bv, 1, 2), lam_params, lam_init=lam_init, tq=512,
                            tk=ATTN_TK if n % ATTN_TK == 0 else n)
    return _outproj(o_f, o_b, gate, d_lat, x, mod[:, 2][:, None, :], gdn_norm_w, subln_w, w_out,
                    lam_init=lam_init, n_ctx=n_ctx, tm=tm)


def kernel(x, c, ctx, c_ctx, ada_w, ada_b, norm1_w, norm2_w, w_in, conv_w, a_log, dt_bias, gdn_norm_w,
           lam_q1, lam_k1, lam_q2, lam_k2, subln_w, w_out_ab, w_out_f, peer_wq, peer_keys, peer_u,
           peer_v, final_norm_w):
    bsz, n_lat, d = x.shape
    depth = ada_w.shape[0]
    last_ctx_reader = 2 * ((depth - 1) // 2)
    cmat = jnp.concatenate([c, c_ctx[None, :], jnp.zeros((8 - bsz - 1, d), F32)], axis=0)
    for i in range(depth):
        if i < last_ctx_reader:
            raise NotImplementedError("context stream advance (depth > 2) is not implemented")
        mod_all = _modulation(cmat, ada_w[i], ada_b[i])
        mod = mod_all[:bsz].reshape(bsz, 6, d)
        mod_c = mod_all[bsz].reshape(6, d)
        vec = lambda j: mod[:, j][:, None, :]
        j = i // 2
        if i % 2 == 0:
            x = _mixer_ab_layer(x, ctx, mod, mod_c, i, norm1_w[i], w_in[j], conv_w[j], a_log[j], dt_bias[j],
                                gdn_norm_w[j], (lam_q1[j], lam_k1[j], lam_q2[j], lam_k2[j]), subln_w[j],
                                w_out_ab[j])
        else:
            x = _fourier_layer(x, norm1_w[i], vec(1), vec(0), vec(2), w_out_f[j])
        x = _peer_layer(x, norm2_w[i], vec(4), vec(3), vec(5), peer_wq[i], peer_keys[i], peer_u, peer_v, i)
    return _final_norm(x, final_norm_w, tm=512)
```
